```python
import math
import jax
import jax.numpy as jnp
from jax import lax
import numpy as np

D_MODEL = 1024
BATCH = 4
SEQ = 8192
DEPTH = 2
DEC_BATCH = 8
DEC_SEQ = 16
PAST_LEN = 2048

CHUNK = 64
N_MIXERS = 2
N_GMLP_LAYERS = (DEPTH + 1) // 2
N_ATTN_LAYERS = DEPTH // 2
GMLP_CHUNK = 128
GMLP_WIDTH = 2 * D_MODEL
GMLP_GROUPS = 8
GMLP_GROUP_DIM = GMLP_WIDTH // GMLP_GROUPS
DIFF_HEADS = 8
DIFF_HEAD_DIM = D_MODEL // (2 * DIFF_HEADS)
DIFF_VALUE_DIM = 2 * DIFF_HEAD_DIM
ROT_DIM = DIFF_HEAD_DIM // 4
ROPE_THETA = 500000.0
Q_BLOCK = 128
MOE_GROUPS = 4
MOE_EXPERTS_PER_GROUP = 8
N_EXPERTS = MOE_GROUPS * MOE_EXPERTS_PER_GROUP
MOE_TOP_K = 2
EXPERT_HIDDEN = D_MODEL // 2
MOE_BLOCK = 128
EPS = 1e-6

kernel_name = 'streaming_gmlp_diffattn_hmoe'


def rms_norm(x, g):
    xf = x.astype(jnp.float32)
    y = xf * lax.rsqrt(jnp.mean(xf * xf, axis=-1, keepdims=True) + EPS)
    return (y * g.astype(jnp.float32)).astype(x.dtype)


def partial_rope(x, pos):
    half = ROT_DIM // 2
    inv_freq = jnp.power(ROPE_THETA, -jnp.arange(half, dtype=jnp.float32) * (2.0 / ROT_DIM))
    ang = pos.astype(jnp.float32)[:, None] * inv_freq[None, :]
    cos = jnp.cos(ang)[None, :, None, None, :]
    sin = jnp.sin(ang)[None, :, None, None, :]
    xf = x.astype(jnp.float32)
    x1 = xf[..., :half]
    x2 = xf[..., half:ROT_DIM]
    out = jnp.concatenate([x1 * cos - x2 * sin, x1 * sin + x2 * cos, xf[..., ROT_DIM:]], axis=-1)
    return out.astype(x.dtype)


def gmlp_mixer(h, w_in, v_norm, w_s, b_s, w_out):
    B, L, _ = h.shape
    z = jax.nn.gelu(h @ w_in)
    u, v = jnp.split(z, 2, axis=-1)
    v = rms_norm(v, v_norm)
    lc = min(L, GMLP_CHUNK)
    n_chunks = L // lc
    pos = jnp.arange(lc)
    mask = (pos[None, :] // CHUNK) <= (pos[:, None] // CHUNK)
    ws = jnp.where(mask[None], w_s[:, :lc, :lc], 0.0)
    vc = v.reshape(B, n_chunks, lc, GMLP_GROUPS, GMLP_GROUP_DIM)
    s = jnp.einsum('gij,bcjgd->bcigd', ws, vc) + b_s[:, :lc].T[None, None, :, :, None]
    y = (u * s.reshape(B, L, GMLP_WIDTH)) @ w_out
    return y, v[:, L - lc:]


def diff_block(q, k, v, mask, lam):
    s = jnp.einsum('bqhcd,bkhcd->bhcqk', q.astype(jnp.float32), k.astype(jnp.float32)) * (DIFF_HEAD_DIM ** -0.5)
    s = jnp.where(mask[None, None, None], s, -1e30)
    p = jax.nn.softmax(s, axis=-1)
    a = p[:, :, 0] - lam * p[:, :, 1]
    return jnp.einsum('bhqk,bkhe->bqhe', a, v.astype(jnp.float32))


def diff_attention(h, pos, cache_k, cache_v, w_in, q_norm, k_norm, lam_q1, lam_k1, lam_q2, lam_k2,
                   sub_norm, w_out, lam_init):
    B, L, _ = h.shape
    q, k, v = jnp.split(h @ w_in, 3, axis=-1)
    q = partial_rope(rms_norm(q.reshape(B, L, DIFF_HEADS, 2, DIFF_HEAD_DIM), q_norm), pos)
    k = partial_rope(rms_norm(k.reshape(B, L, DIFF_HEADS, 2, DIFF_HEAD_DIM), k_norm), pos)
    v = v.reshape(B, L, DIFF_HEADS, DIFF_VALUE_DIM)
    f32 = jnp.float32
    lam = (jnp.exp(jnp.sum(lam_q1.astype(f32) * lam_k1.astype(f32)))
           - jnp.exp(jnp.sum(lam_q2.astype(f32) * lam_k2.astype(f32))) + lam_init)
    if cache_k is None:
        n_blocks = L // Q_BLOCK
        key_chunk = jnp.arange(L) // CHUNK
        qb = jnp.moveaxis(q.reshape(B, n_blocks, Q_BLOCK, DIFF_HEADS, 2, DIFF_HEAD_DIM), 1, 0)

        def one_block(args):
            q_blk, blk = args
            q_chunk = (blk * Q_BLOCK + jnp.arange(Q_BLOCK)) // CHUNK
            mask = key_chunk[None, :] <= q_chunk[:, None]
            return diff_block(q_blk, k, v, mask, lam)

        o = lax.map(one_block, (qb, jnp.arange(n_blocks)))
        o = jnp.moveaxis(o, 0, 1).reshape(B, L, DIFF_HEADS, DIFF_VALUE_DIM)
    else:
        past = cache_k.shape[1]
        k_all = jnp.concatenate(
            [cache_k.reshape(B, past, DIFF_HEADS, 2, DIFF_HEAD_DIM).astype(k.dtype), k], axis=1)
        v_all = jnp.concatenate([cache_v.astype(v.dtype), v], axis=1)
        mask = jnp.ones((L, past + L), dtype=bool)
        o = diff_block(q, k_all, v_all, mask, lam)
    o = rms_norm(o, sub_norm) * (1.0 - lam_init)
    y = o.astype(h.dtype).reshape(B, L, DIFF_HEADS * DIFF_VALUE_DIM) @ w_out
    return y, k.reshape(B, L, DIFF_HEADS, DIFF_VALUE_DIM), v


def hier_moe(h, w_group, w_router, w_gate, w_up, w_down):
    B, L, D = h.shape
    T = B * L
    xt = h.reshape(T, D)
    xf = xt.astype(jnp.float32)
    p_group = jax.nn.softmax(xf @ w_group.astype(jnp.float32), axis=-1)
    g_sel = jnp.argmax(p_group, axis=-1)
    p_g = jnp.take_along_axis(p_group, g_sel[:, None], axis=1)
    logits_e = (xf @ w_router.astype(jnp.float32)).reshape(T, MOE_GROUPS, MOE_EXPERTS_PER_GROUP)
    logits_in = jnp.take_along_axis(logits_e, g_sel[:, None, None], axis=1)[:, 0]
    top_p, top_j = lax.top_k(jax.nn.softmax(logits_in, axis=-1), MOE_TOP_K)
    gates = p_g * top_p / jnp.sum(top_p, axis=-1, keepdims=True)
    expert = g_sel[:, None] * MOE_EXPERTS_PER_GROUP + top_j
    A = T * MOE_TOP_K
    flat_e = expert.reshape(A)
    flat_tok = jnp.repeat(jnp.arange(T, dtype=jnp.int32), MOE_TOP_K)
    order = jnp.argsort(flat_e)
    e_sorted = flat_e[order]
    tok_sorted = flat_tok[order]
    gate_sorted = gates.reshape(A)[order]
    counts = jnp.bincount(flat_e, length=N_EXPERTS)
    padded = (counts + MOE_BLOCK - 1) // MOE_BLOCK * MOE_BLOCK
    start = jnp.cumsum(counts) - counts
    pad_end = jnp.cumsum(padded)
    pad_start = pad_end - padded
    dest = pad_start[e_sorted] + jnp.arange(A) - start[e_sorted]
    n_blocks = -(-A // MOE_BLOCK) + N_EXPERTS
    rows = n_blocks * MOE_BLOCK
    row_tok = jnp.full((rows,), T, dtype=jnp.int32).at[dest].set(tok_sorted)
    x_pad = jnp.concatenate([xt, jnp.zeros((1, D), xt.dtype)], axis=0)
    xb = x_pad[row_tok].reshape(n_blocks, MOE_BLOCK, D)
    block_expert = jnp.minimum(
        jnp.searchsorted(pad_end, jnp.arange(n_blocks) * MOE_BLOCK, side='right'), N_EXPERTS - 1)

    def run_block(args):
        xblk, e = args
        return (jax.nn.silu(xblk @ w_gate[e]) * (xblk @ w_up[e])) @ w_down[e]

    yb = lax.map(run_block, (xb, block_expert)).reshape(rows, D)
    out = jnp.zeros((T, D), jnp.float32).at[tok_sorted].add(
        yb[dest].astype(jnp.float32) * gate_sorted[:, None])
    return out.astype(h.dtype).reshape(B, L, D)


def setup_inputs(seed: int = 0) -> dict:
    key = jax.random.key(seed)
    ks = jax.random.split(key, 26)

    def nrm(k, shape, scale):
        return jax.random.normal(k, shape, jnp.float32) * scale

    def gain(k, shape):
        return 1.0 + 0.02 * jax.random.normal(k, shape, jnp.float32)

    na, ng = N_ATTN_LAYERS, N_GMLP_LAYERS
    hv = DIFF_HEADS * DIFF_VALUE_DIM
    return {
        'x_prompt': nrm(ks[0], (BATCH, SEQ, D_MODEL), 1.0),
        'x_sample': nrm(ks[1], (DEC_BATCH, DEC_SEQ, D_MODEL), 1.0),
        'cache_attn_k': nrm(ks[2], (na, DEC_BATCH, PAST_LEN, DIFF_HEADS, DIFF_VALUE_DIM), 1.0),
        'cache_attn_v': nrm(ks[3], (na, DEC_BATCH, PAST_LEN, DIFF_HEADS, DIFF_VALUE_DIM), 1.0),
        'mix_norm': gain(ks[4], (DEPTH, D_MODEL)),
        'ffn_norm': gain(ks[5], (DEPTH, D_MODEL)),
        'gmlp_w_in': nrm(ks[6], (ng, D_MODEL, 2 * GMLP_WIDTH), D_MODEL ** -0.5),
        'gmlp_v_norm': gain(ks[7], (ng, GMLP_WIDTH)),
        'gmlp_w_s': nrm(ks[8], (ng, GMLP_GROUPS, GMLP_CHUNK, GMLP_CHUNK), GMLP_CHUNK ** -0.5),
        'gmlp_b_s': 1.0 + nrm(ks[9], (ng, GMLP_GROUPS, GMLP_CHUNK), 0.1),
        'gmlp_w_out': nrm(ks[10], (ng, GMLP_WIDTH, D_MODEL), GMLP_WIDTH ** -0.5),
        'attn_w_in': nrm(ks[11], (na, D_MODEL, 3 * hv), D_MODEL ** -0.5),
        'attn_q_norm': gain(ks[12], (na, DIFF_HEAD_DIM)),
        'attn_k_norm': gain(ks[13], (na, DIFF_HEAD_DIM)),
        'attn_lam_q1': nrm(ks[14], (na, DIFF_HEAD_DIM), 0.1),
        'attn_lam_k1': nrm(ks[15], (na, DIFF_HEAD_DIM), 0.1),
        'attn_lam_q2': nrm(ks[16], (na, DIFF_HEAD_DIM), 0.1),
        'attn_lam_k2': nrm(ks[17], (na, DIFF_HEAD_DIM), 0.1),
        'attn_sub_norm': gain(ks[18], (na, DIFF_VALUE_DIM)),
        'attn_w_out': nrm(ks[19], (na, hv, D_MODEL), hv ** -0.5),
        'moe_w_group': nrm(ks[20], (DEPTH, D_MODEL, MOE_GROUPS), D_MODEL ** -0.5),
        'moe_w_router': nrm(ks[21], (DEPTH, D_MODEL, N_EXPERTS), D_MODEL ** -0.5),
        'moe_w_gate': nrm(ks[22], (DEPTH, N_EXPERTS, D_MODEL, EXPERT_HIDDEN), D_MODEL ** -0.5),
        'moe_w_up': nrm(ks[23], (DEPTH, N_EXPERTS, D_MODEL, EXPERT_HIDDEN), D_MODEL ** -0.5),
        'moe_w_down': nrm(ks[24], (DEPTH, N_EXPERTS, EXPERT_HIDDEN, D_MODEL), EXPERT_HIDDEN ** -0.5),
    }


def reference(x_prompt, x_sample, cache_attn_k, cache_attn_v, mix_norm, ffn_norm,
              gmlp_w_in, gmlp_v_norm, gmlp_w_s, gmlp_b_s, gmlp_w_out,
              attn_w_in, attn_q_norm, attn_k_norm, attn_lam_q1, attn_lam_k1, attn_lam_q2, attn_lam_k2,
              attn_sub_norm, attn_w_out,
              moe_w_group, moe_w_router, moe_w_gate, moe_w_up, moe_w_down):
    past = cache_attn_k.shape[2]
    pos_p = jnp.arange(x_prompt.shape[1])
    pos_s = past + jnp.arange(x_sample.shape[1])
    xp, xs = x_prompt, x_sample
    gv_p, gv_s, kp_l, vp_l, ks_l, vs_l = [], [], [], [], [], []
    for i in range(DEPTH):
        j = i // N_MIXERS
        hp = rms_norm(xp, mix_norm[i])
        hs = rms_norm(xs, mix_norm[i])
        if i % N_MIXERS == 0:
            w = (gmlp_w_in[j], gmlp_v_norm[j], gmlp_w_s[j], gmlp_b_s[j], gmlp_w_out[j])
            yp, vrow_p = gmlp_mixer(hp, *w)
            ys, vrow_s = gmlp_mixer(hs, *w)
            gv_p.append(vrow_p)
            gv_s.append(vrow_s)
        else:
            lam_init = 0.8 - 0.6 * math.exp(-0.3 * i)
            w = (attn_w_in[j], attn_q_norm[j], attn_k_norm[j], attn_lam_q1[j], attn_lam_k1[j],
                 attn_lam_q2[j], attn_lam_k2[j], attn_sub_norm[j], attn_w_out[j], lam_init)
            yp, k_p, v_p = diff_attention(hp, pos_p, None, None, *w)
            ys, k_s, v_s = diff_attention(hs, pos_s, cache_attn_k[j], cache_attn_v[j], *w)
            kp_l.append(k_p)
            vp_l.append(v_p)
            ks_l.append(k_s)
            vs_l.append(v_s)
        xp = xp + yp
        xs = xs + ys
        mw = (moe_w_group[i], moe_w_router[i], moe_w_gate[i], moe_w_up[i], moe_w_down[i])
        xp = xp + hier_moe(rms_norm(xp, ffn_norm[i]), *mw)
        xs = xs + hier_moe(rms_norm(xs, ffn_norm[i]), *mw)
    return (xp, xs, jnp.stack(gv_p), jnp.stack(gv_s), jnp.stack(kp_l), jnp.stack(vp_l),
            jnp.stack(ks_l), jnp.stack(vs_l))
```

```python
import functools
import math

import jax
import jax.numpy as jnp
from jax import lax
from jax.experimental import pallas as pl
from jax.experimental.pallas import tpu as pltpu

D_MODEL = 1024
DEPTH = 2
CHUNK = 64
GMLP_CHUNK = 128
GMLP_WIDTH = 2 * D_MODEL
GMLP_GROUPS = 8
GMLP_GROUP_DIM = GMLP_WIDTH // GMLP_GROUPS
DIFF_HEADS = 8
DIFF_HEAD_DIM = D_MODEL // (2 * DIFF_HEADS)
DIFF_VALUE_DIM = 2 * DIFF_HEAD_DIM
ROT_DIM = DIFF_HEAD_DIM // 4
ROPE_THETA = 500000.0
MOE_GROUPS = 4
MOE_EXPERTS_PER_GROUP = 8
N_EXPERTS = MOE_GROUPS * MOE_EXPERTS_PER_GROUP
EXPERT_HIDDEN = D_MODEL // 2
EPS = 1e-6

LANES = 128
SUBLANES = 8
BF16_TILE_ROWS = 2 * SUBLANES
ROW_TILES = D_MODEL // LANES
VMEM_LIMIT_BYTES = 56 * 1024 * 1024

ROUTE_LANES = LANES
GROUP_LANE0 = N_EXPERTS
ROUTE_COLS = 8
NEG_BIG = -1e30

MIX_TILE = 256
ATT_TILE = 256
MOE_BLOCK_ROWS = 256
ROW_DMA_TILE = 512
INDEX_SLICE_WORDS = 1024

F32 = jnp.float32
BF16 = jnp.bfloat16


def _params(n_axes=1):
    return pltpu.CompilerParams(dimension_semantics=("arbitrary",) * n_axes,
                                vmem_limit_bytes=VMEM_LIMIT_BYTES)


def _rms(x, g):
    return x * lax.rsqrt(jnp.mean(x * x, axis=-1, keepdims=True) + EPS) * g


def _dot(a, b):
    return jnp.dot(a, b, preferred_element_type=F32)


def _const_spec(shape):
    return pl.BlockSpec(shape, lambda *_: (0,) * len(shape))


def _route_epilogue(x1, ffn_g, wr_hi, wr_lo, base_ref, hn3_ref, route_ref, counts_ref):
    tm = x1.shape[0]

    @pl.when(pl.program_id(0) == 0)
    def _():
        base_ref[...] = jnp.zeros_like(base_ref)

    hn = _rms(x1, ffn_g)
    for s in range(ROW_TILES):
        hn3_ref[pl.ds(s, tm, stride=ROW_TILES), :] = hn[:, s * LANES:(s + 1) * LANES]

    h_hi = hn.astype(BF16)
    h_lo = (hn - h_hi.astype(F32)).astype(BF16)
    logit = _dot(h_hi, wr_hi) + _dot(h_lo, wr_hi) + _dot(h_hi, wr_lo)

    lane = lax.broadcasted_iota(jnp.int32, (tm, ROUTE_LANES), 1)
    far = jnp.int32(4 * ROUTE_LANES)
    lg = jnp.where(lane >= GROUP_LANE0, jnp.where(lane < GROUP_LANE0 + MOE_GROUPS, logit, NEG_BIG), NEG_BIG)
    mg = jnp.max(lg, axis=1, keepdims=True)
    g_lane = jnp.min(jnp.where(lg == mg, lane, far), axis=1, keepdims=True)
    g_sel = g_lane - GROUP_LANE0
    p_g = 1.0 / jnp.sum(jnp.exp(lg - mg), axis=1, keepdims=True)

    lo_lane = g_sel * MOE_EXPERTS_PER_GROUP
    le = jnp.where(lane >= lo_lane, jnp.where(lane < lo_lane + MOE_EXPERTS_PER_GROUP, logit, NEG_BIG), NEG_BIG)
    m1 = jnp.max(le, axis=1, keepdims=True)
    j1 = jnp.min(jnp.where(le == m1, lane, far), axis=1, keepdims=True)
    le2 = jnp.where(lane == j1, NEG_BIG, le)
    m2 = jnp.max(le2, axis=1, keepdims=True)
    j2 = jnp.min(jnp.where(le2 == m2, lane, far), axis=1, keepdims=True)
    r = jnp.exp(m2 - m1)
    gate1 = p_g / (1.0 + r)
    gate2 = p_g * r / (1.0 + r)

    hit1 = lane == j1
    hit2 = lane == j2
    onehot = jnp.where(hit1, 1.0, jnp.where(hit2, 1.0, 0.0))
    row = lax.broadcasted_iota(jnp.int32, (tm, tm), 0)
    col = lax.broadcasted_iota(jnp.int32, (tm, tm), 1)
    earlier = jnp.where(row > col, 1.0, 0.0).astype(BF16)
    prefix = _dot(earlier, onehot.astype(BF16)) + base_ref[...]
    rank1 = jnp.sum(jnp.where(hit1, prefix, 0.0), axis=1, keepdims=True)
    rank2 = jnp.sum(jnp.where(hit2, prefix, 0.0), axis=1, keepdims=True)
    base_new = base_ref[...] + jnp.sum(onehot, axis=0, keepdims=True)
    base_ref[...] = base_new
    counts_ref[...] = base_new

    c = lax.broadcasted_iota(jnp.int32, (tm, ROUTE_COLS), 1)
    rec = jnp.where(c == 0, j1.astype(F32),
          jnp.where(c == 1, j2.astype(F32),
          jnp.where(c == 2, gate1,
          jnp.where(c == 3, gate2,
          jnp.where(c == 4, rank1,
          jnp.where(c == 5, rank2, 0.0))))))
    route_ref[...] = rec


def _route_out_shapes(t):
    return (jax.ShapeDtypeStruct((t * ROW_TILES, LANES), F32),
            jax.ShapeDtypeStruct((t, ROUTE_COLS), F32),
            jax.ShapeDtypeStruct((1, ROUTE_LANES), F32))


def _route_out_specs(tm):
    return (pl.BlockSpec((tm * ROW_TILES, LANES), lambda i: (i, 0)),
            pl.BlockSpec((tm, ROUTE_COLS), lambda i: (i, 0)),
            pl.BlockSpec((1, ROUTE_LANES), lambda i: (0, 0)))


def _route_in_specs():
    return [_const_spec((1, D_MODEL)), _const_spec((D_MODEL, ROUTE_LANES)), _const_spec((D_MODEL, ROUTE_LANES))]


def _gelu_tanh(x):
    cdf = 0.5 * (1.0 + jnp.tanh(math.sqrt(2.0 / math.pi) * (x + 0.044715 * (x * x * x))))
    return x * cdf


def _gmlp_kernel(x_ref, mixg_ref, win_ref, vng_ref, ws_ref, bexp_ref, wout_ref,
                 ffng_ref, wrhi_ref, wrlo_ref,
                 x1_ref, hn3_ref, route_ref, counts_ref, vlast_ref, base_ref, *, tiles_per_batch):
    tm = x_ref.shape[0]
    x = x_ref[...]
    h = _rms(x, mixg_ref[...]).astype(BF16)
    z = _gelu_tanh(_dot(h, win_ref[...]))
    u = z[:, :GMLP_WIDTH]
    vn = _rms(z[:, GMLP_WIDTH:], vng_ref[...])
    vb = vn.astype(BF16)
    gated = []
    for c in range(tm // GMLP_CHUNK):
        rows = slice(c * GMLP_CHUNK, (c + 1) * GMLP_CHUNK)
        s = jnp.concatenate(
            [_dot(ws_ref[g], vb[rows, g * GMLP_GROUP_DIM:(g + 1) * GMLP_GROUP_DIM])
             for g in range(GMLP_GROUPS)], axis=1) + bexp_ref[...]
        gated.append((u[rows] * s).astype(BF16))
    y = _dot(jnp.concatenate(gated, axis=0), wout_ref[...])
    x1 = x + y
    x1_ref[...] = x1

    @pl.when(pl.program_id(0) % tiles_per_batch == tiles_per_batch - 1)
    def _():
        vlast_ref[0] = vn[tm - GMLP_CHUNK:]

    _route_epilogue(x1, ffng_ref[...], wrhi_ref[...], wrlo_ref[...], base_ref, hn3_ref, route_ref, counts_ref)


def _gmlp_layer(x, n_batch, mix_g, w_in, vn_g, ws_eff, b_exp, w_out, ffn_g, wr_hi, wr_lo):
    t = x.shape[0]
    tm = min(MIX_TILE, t)
    tiles_per_batch = t // n_batch // tm
    gw = GMLP_WIDTH
    kern = functools.partial(_gmlp_kernel, tiles_per_batch=tiles_per_batch)
    return pl.pallas_call(
        kern,
        grid=(t // tm,),
        in_specs=[pl.BlockSpec((tm, D_MODEL), lambda i: (i, 0)),
                  _const_spec((1, D_MODEL)), _const_spec((D_MODEL, 2 * gw)), _const_spec((1, gw)),
                  _const_spec((GMLP_GROUPS, GMLP_CHUNK, GMLP_CHUNK)), _const_spec((GMLP_CHUNK, gw)),
                  _const_spec((gw, D_MODEL))] + _route_in_specs(),
        out_specs=(pl.BlockSpec((tm, D_MODEL), lambda i: (i, 0)),) + _route_out_specs(tm)
                  + (pl.BlockSpec((1, GMLP_CHUNK, gw), lambda i: (i // tiles_per_batch, 0, 0)),),
        out_shape=(jax.ShapeDtypeStruct((t, D_MODEL), F32),) + _route_out_shapes(t)
                  + (jax.ShapeDtypeStruct((n_batch, GMLP_CHUNK, gw), F32),),
        scratch_shapes=[pltpu.VMEM((1, ROUTE_LANES), F32)],
        compiler_params=_params(),
        name="gmlp_mixer",
    )(x, mix_g, w_in, vn_g, ws_eff, b_exp, w_out, ffn_g, wr_hi, wr_lo)


def _qk_norm_rope(t, group_ones, gain, cos, sin_lo, sin_hi):
    ms = _dot((t * t).astype(BF16), group_ones) * (1.0 / DIFF_HEAD_DIM)
    tn = t * lax.rsqrt(ms + EPS) * gain
    heads = []
    for h in range(DIFF_HEADS):
        th = tn[:, h * LANES:(h + 1) * LANES]
        heads.append(th * cos + pltpu.roll(th, LANES - ROT_DIM // 2, 1) * sin_lo
                     + pltpu.roll(th, ROT_DIM // 2, 1) * sin_hi)
    return jnp.concatenate(heads, axis=1)


def _qkv_kernel(x_ref, mixg_ref, win_ref, ones_ref, qg_ref, kg_ref, cos_ref, slo_ref, shi_ref, *outs, prompt):
    tm = x_ref.shape[0]
    h = _rms(x_ref[...], mixg_ref[...]).astype(BF16)
    qkv = _dot(h, win_ref[...])
    rope = (cos_ref[...], slo_ref[...], shi_ref[...])
    q = _qk_norm_rope(qkv[:, :D_MODEL], ones_ref[...], qg_ref[...], *rope) * (DIFF_HEAD_DIM ** -0.5)
    k = _qk_norm_rope(qkv[:, D_MODEL:2 * D_MODEL], ones_ref[...], kg_ref[...], *rope)
    v = qkv[:, 2 * D_MODEL:]
    if not prompt:
        q_ref, kout_ref, vout_ref = outs
        q_ref[...] = q
        kout_ref[...] = k
        vout_ref[...] = v
        return
    kout_ref, vout_ref, kb_ref, vt_ref, qt_ref = outs
    kout_ref[...] = k
    vout_ref[...] = v
    for hd in range(DIFF_HEADS):
        lanes = slice(hd * LANES, (hd + 1) * LANES)
        kb_ref[0, hd] = k[:, lanes].astype(BF16)
        for c in range(tm // ATT_TILE):
            rows = slice(c * ATT_TILE, (c + 1) * ATT_TILE)
            vt_ref[0, hd, c] = v[rows, lanes].T.astype(BF16)
            qt_ref[0, hd, c] = q[rows, lanes].T.astype(BF16)


def _qkv_layer(x, n_batch, mix_g, w_in, group_ones, q_g, k_g, cos, sin_lo, sin_hi, prompt):
    t = x.shape[0]
    tm = min(MIX_TILE, t)
    seq = t // n_batch
    tpb = seq // tm if prompt else 1
    n_tab = cos.shape[0] // tm
    tab_spec = pl.BlockSpec((tm, LANES), lambda i: (i % n_tab, 0))
    row_spec = pl.BlockSpec((tm, D_MODEL), lambda i: (i, 0))
    in_specs = [row_spec, _const_spec((1, D_MODEL)), _const_spec((D_MODEL, 3 * D_MODEL)),
                _const_spec((D_MODEL, D_MODEL)), _const_spec((1, D_MODEL)), _const_spec((1, D_MODEL)),
                tab_spec, tab_spec, tab_spec]
    row_shape = jax.ShapeDtypeStruct((t, D_MODEL), F32)
    if prompt:
        nq = seq // ATT_TILE
        cpt = tm // ATT_TILE
        out_specs = (row_spec, row_spec,
                     pl.BlockSpec((1, DIFF_HEADS, tm, LANES), lambda i: (i // tpb, 0, i % tpb, 0)),
                     pl.BlockSpec((1, DIFF_HEADS, cpt, LANES, ATT_TILE), lambda i: (i // tpb, 0, i % tpb, 0, 0)),
                     pl.BlockSpec((1, DIFF_HEADS, cpt, LANES, ATT_TILE), lambda i: (i // tpb, 0, i % tpb, 0, 0)))
        out_shape = (row_shape, row_shape,
                     jax.ShapeDtypeStruct((n_batch, DIFF_HEADS, seq, LANES), BF16),
                     jax.ShapeDtypeStruct((n_batch, DIFF_HEADS, nq, LANES, ATT_TILE), BF16),
                     jax.ShapeDtypeStruct((n_batch, DIFF_HEADS, nq, LANES, ATT_TILE), BF16))
    else:
        out_specs = (row_spec, row_spec, row_spec)
        out_shape = (row_shape, row_shape, row_shape)
    return pl.pallas_call(
        functools.partial(_qkv_kernel, prompt=prompt),
        grid=(t // tm,), in_specs=in_specs, out_specs=out_specs, out_shape=out_shape,
        compiler_params=_params(), name="qkv_prompt" if prompt else "qkv_sample",
    )(x, mix_g, w_in, group_ones, q_g, k_g, cos, sin_lo, sin_hi)


def _flash_kernel(lam_ref, qt_ref, k_ref, vt_ref, sg_ref, o_ref, *, lam_init):
    tq = ATT_TILE
    i = pl.program_id(2)
    qt = qt_ref[0, 0, 0]
    frow = lax.broadcasted_iota(jnp.int32, (LANES, tq), 0)
    zero = jnp.zeros_like(qt)
    qbd = jnp.concatenate([jnp.where(frow < DIFF_HEAD_DIM, qt, zero),
                           jnp.where(frow >= DIFF_HEAD_DIM, qt, zero)], axis=1)

    def step(j, carry, masked):
        m, l, acc = carry
        kt = k_ref[0, 0, pl.ds(pl.multiple_of(j * tq, tq), tq), :]
        s = _dot(kt, qbd)
        if masked:
            kpos = lax.broadcasted_iota(jnp.int32, (tq, 2 * tq), 0)
            qpos = lax.broadcasted_iota(jnp.int32, (tq, 2 * tq), 1) % tq
            s = jnp.where(kpos // CHUNK <= qpos // CHUNK, s, NEG_BIG)
        m_new = jnp.maximum(m, jnp.max(s, axis=0, keepdims=True))
        p = jnp.exp(s - m_new)
        alpha = jnp.exp(m - m_new)
        l = alpha * l + jnp.sum(p, axis=0, keepdims=True)
        acc = alpha * acc + _dot(vt_ref[0, 0, j], p.astype(BF16))
        return m_new, l, acc

    init = (jnp.full((1, 2 * tq), NEG_BIG, F32), jnp.zeros((1, 2 * tq), F32), jnp.zeros((LANES, 2 * tq), F32))
    carry = lax.fori_loop(0, i, lambda j, c: step(j, c, False), init)
    _, l, acc = step(i, carry, True)
    lam = lam_ref[0]
    o = acc[:, :tq] / l[:, :tq] - lam * (acc[:, tq:] / l[:, tq:])
    on = o * lax.rsqrt(jnp.mean(o * o, axis=0, keepdims=True) + EPS) * sg_ref[...] * (1.0 - lam_init)
    o_ref[0] = on.T.astype(o_ref.dtype)


def _flash_attention(lam, qt, kb, vt, sub_g_col, lam_init):
    n_batch, _, nq, _, tq = qt.shape
    seq = nq * tq
    return pl.pallas_call(
        functools.partial(_flash_kernel, lam_init=lam_init),
        grid=(n_batch, DIFF_HEADS, nq),
        in_specs=[pl.BlockSpec(memory_space=pltpu.SMEM),
                  pl.BlockSpec((1, 1, 1, LANES, tq), lambda b, h, i: (b, h, i, 0, 0)),
                  pl.BlockSpec((1, 1, seq, LANES), lambda b, h, i: (b, h, 0, 0)),
                  pl.BlockSpec((1, 1, nq, LANES, tq), lambda b, h, i: (b, h, 0, 0, 0)),
                  pl.BlockSpec((LANES, 1), lambda b, h, i: (0, 0))],
        out_specs=pl.BlockSpec((1, tq, LANES), lambda b, h, i: (b, i, h)),
        out_shape=jax.ShapeDtypeStruct((n_batch, seq, D_MODEL), BF16),
        compiler_params=_params(3), name="diff_flash",
    )(lam, qt, kb, vt, sub_g_col)


def _sample_attn_kernel(lam_ref, q_ref, kn_ref, vn_ref, ck_ref, cv_ref, sg_ref, o_ref, *, lam_init, past):
    q = q_ref[0]
    kn = kn_ref[0]
    vn = vn_ref[0]
    lam = lam_ref[0]
    lane = lax.broadcasted_iota(jnp.int32, (q.shape[0], LANES), 1)
    contract_last = (((1,), (1,)), ((), ()))
    outs = []
    for h in range(DIFF_HEADS):
        lanes = slice(h * LANES, (h + 1) * LANES)
        kc = ck_ref[0, pl.ds(h, past, stride=DIFF_HEADS), :].astype(BF16)
        vc = cv_ref[0, pl.ds(h, past, stride=DIFF_HEADS), :].astype(BF16)
        qh = q[:, lanes]
        knh = kn[:, lanes].astype(BF16)
        vnh = vn[:, lanes].astype(BF16)
        sub = []
        for c in range(2):
            keep = (lane < DIFF_HEAD_DIM) if c == 0 else (lane >= DIFF_HEAD_DIM)
            qc = jnp.where(keep, qh, 0.0).astype(BF16)
            s_old = lax.dot_general(qc, kc, contract_last, preferred_element_type=F32)
            s_new = lax.dot_general(qc, knh, contract_last, preferred_element_type=F32)
            m = jnp.maximum(jnp.max(s_old, axis=1, keepdims=True), jnp.max(s_new, axis=1, keepdims=True))
            p_old = jnp.exp(s_old - m)
            p_new = jnp.exp(s_new - m)
            l = jnp.sum(p_old, axis=1, keepdims=True) + jnp.sum(p_new, axis=1, keepdims=True)
            sub.append((_dot(p_old.astype(BF16), vc) + _dot(p_new.astype(BF16), vnh)) / l)
        o = sub[0] - lam * sub[1]
        outs.append(_rms(o, sg_ref[...]) * (1.0 - lam_init))
    o_ref[0] = jnp.concatenate(outs, axis=1).astype(o_ref.dtype)


def _sample_attention(lam, q, kn, vn, cache_k, cache_v, sub_g_row, lam_init):
    nb, rows, _ = q.shape
    past = cache_k.shape[1] // DIFF_HEADS
    new_spec = pl.BlockSpec((1, rows, D_MODEL), lambda b: (b, 0, 0))
    cache_spec = pl.BlockSpec((1, past * DIFF_HEADS, LANES), lambda b: (b, 0, 0))
    return pl.pallas_call(
        functools.partial(_sample_attn_kernel, lam_init=lam_init, past=past),
        grid=(nb,),
        in_specs=[pl.BlockSpec(memory_space=pltpu.SMEM), new_spec, new_spec, new_spec, cache_spec, cache_spec,
                  _const_spec((1, LANES))],
        out_specs=new_spec, out_shape=jax.ShapeDtypeStruct((nb, rows, D_MODEL), BF16),
        compiler_params=_params(), name="sample_attn",
    )(lam, q, kn, vn, cache_k, cache_v, sub_g_row)


def _attn_out_kernel(a_ref, x_ref, wout_ref, ffng_ref, wrhi_ref, wrlo_ref,
                     x1_ref, hn3_ref, route_ref, counts_ref, base_ref):
    x1 = x_ref[...] + _dot(a_ref[...], wout_ref[...])
    x1_ref[...] = x1
    _route_epilogue(x1, ffng_ref[...], wrhi_ref[...], wrlo_ref[...], base_ref, hn3_ref, route_ref, counts_ref)


def _attn_out_layer(a, x, w_out, ffn_g, wr_hi, wr_lo):
    t = x.shape[0]
    tm = min(MIX_TILE, t)
    row_spec = pl.BlockSpec((tm, D_MODEL), lambda i: (i, 0))
    return pl.pallas_call(
        _attn_out_kernel,
        grid=(t // tm,),
        in_specs=[row_spec, row_spec, _const_spec((D_MODEL, D_MODEL))] + _route_in_specs(),
        out_specs=(row_spec,) + _route_out_specs(tm),
        out_shape=(jax.ShapeDtypeStruct((t, D_MODEL), F32),) + _route_out_shapes(t),
        scratch_shapes=[pltpu.VMEM((1, ROUTE_LANES), F32)],
        compiler_params=_params(), name="attn_out",
    )(a, x, w_out, ffn_g, wr_hi, wr_lo)


def _load_tile_indices(dest_hbm, idx_smem, idx_sem):
    n = idx_smem.shape[0]
    start = pl.multiple_of(pl.program_id(0) * n, n)
    idx_copy = pltpu.make_async_copy(dest_hbm.at[pl.ds(start, n)], idx_smem, idx_sem)
    idx_copy.start()
    idx_copy.wait()


def _index_tile_len(n_indices):
    return -(-n_indices // INDEX_SLICE_WORDS) * INDEX_SLICE_WORDS


def _tiled_indices(per_step):
    n = per_step.shape[1]
    return jnp.pad(per_step, ((0, 0), (0, _index_tile_len(n) - n))).reshape(-1)


def _dispatch_kernel(idx_hbm, hn3_hbm, xs3_hbm, idx_smem, zero_buf, idx_sem, row_sem, *, td, n_fill):
    i = pl.program_id(0)
    _load_tile_indices(idx_hbm, idx_smem, idx_sem)
    zero_buf[...] = jnp.zeros_like(zero_buf)

    def scatter(t, carry):
        src = hn3_hbm.at[i * td + t]
        pltpu.make_async_copy(src, xs3_hbm.at[idx_smem[2 * t]], row_sem).start()
        pltpu.make_async_copy(src, xs3_hbm.at[idx_smem[2 * t + 1]], row_sem).start()
        return carry

    def fill(p, carry):
        pltpu.make_async_copy(zero_buf, xs3_hbm.at[idx_smem[2 * td + p]], row_sem).start()
        return carry

    lax.fori_loop(0, td, scatter, 0)
    lax.fori_loop(0, n_fill, fill, 0)
    done = xs3_hbm.at[pl.ds(0, 2 * td + n_fill)]
    pltpu.make_async_copy(done, done, row_sem).wait()


def _dispatch(dest, pad_rows, hn3, n_rows):
    t = hn3.shape[0]
    td = min(ROW_DMA_TILE, t)
    steps = t // td
    n_fill = pad_rows.shape[0] // steps
    table = jnp.concatenate([dest.reshape(steps, 2 * td), pad_rows.reshape(steps, n_fill)], axis=1)
    return pl.pallas_call(
        functools.partial(_dispatch_kernel, td=td, n_fill=n_fill),
        grid=(steps,),
        in_specs=[pl.BlockSpec(memory_space=pl.ANY), pl.BlockSpec(memory_space=pl.ANY)],
        out_specs=pl.BlockSpec(memory_space=pl.ANY),
        out_shape=jax.ShapeDtypeStruct((n_rows, ROW_TILES, LANES), F32),
        scratch_shapes=[pltpu.SMEM((_index_tile_len(2 * td + n_fill),), jnp.int32),
                        pltpu.VMEM((ROW_TILES, LANES), F32),
                        pltpu.SemaphoreType.DMA, pltpu.SemaphoreType.DMA],
        compiler_params=_params(), name="moe_dispatch",
    )(_tiled_indices(table), hn3)


def _expert_kernel(bexp_ref, nvalid_ref, xs_ref, wg_ref, wu_ref, wd_ref, yb_ref):
    blk = xs_ref.shape[0] // ROW_TILES
    nvalid = nvalid_ref[pl.program_id(0)]

    @pl.when(nvalid > 0)
    def _():
        x = jnp.concatenate([xs_ref[pl.ds(s, blk, stride=ROW_TILES), :] for s in range(ROW_TILES)], axis=1)
        row = lax.broadcasted_iota(jnp.int32, (blk, 1), 0)
        xb = jnp.where(row < nvalid, x, 0.0).astype(BF16)
        hg = _dot(xb, wg_ref[0])
        hu = _dot(xb, wu_ref[0])
        act = (hg * (1.0 / (1.0 + jnp.exp(-hg))) * hu).astype(BF16)
        y = _dot(act, wd_ref[0])
        for s in range(ROW_TILES):
            yb_ref[pl.ds(s, blk, stride=ROW_TILES), :] = y[:, s * LANES:(s + 1) * LANES]

    @pl.when(nvalid <= 0)
    def _():
        yb_ref[...] = jnp.zeros_like(yb_ref)


def _experts(block_expert, block_nvalid, xs2, wg, wu, wd):
    n_blocks = block_expert.shape[0]
    blk = xs2.shape[0] // ROW_TILES // n_blocks
    rows_spec = pl.BlockSpec((blk * ROW_TILES, LANES), lambda i, be, nv: (i, 0))
    return pl.pallas_call(
        _expert_kernel,
        grid_spec=pltpu.PrefetchScalarGridSpec(
            num_scalar_prefetch=2, grid=(n_blocks,),
            in_specs=[rows_spec,
                      pl.BlockSpec((1, D_MODEL, EXPERT_HIDDEN), lambda i, be, nv: (be[i], 0, 0)),
                      pl.BlockSpec((1, D_MODEL, EXPERT_HIDDEN), lambda i, be, nv: (be[i], 0, 0)),
                      pl.BlockSpec((1, EXPERT_HIDDEN, D_MODEL), lambda i, be, nv: (be[i], 0, 0))],
            out_specs=rows_spec),
        out_shape=jax.ShapeDtypeStruct(xs2.shape, F32),
        compiler_params=_params(), name="moe_experts",
    )(block_expert, block_nvalid, xs2, wg, wu, wd)


def _combine_kernel(dest_hbm, x1_ref, route_ref, yb3_hbm, out_ref, idx_smem, buf0, buf1, idx_sem, row_sem):
    tc = x1_ref.shape[0]
    _load_tile_indices(dest_hbm, idx_smem, idx_sem)

    def body(t, carry):
        dst = pl.ds(pl.multiple_of(t * ROW_TILES, ROW_TILES), ROW_TILES)
        pltpu.make_async_copy(yb3_hbm.at[idx_smem[2 * t]], buf0.at[dst], row_sem).start()
        pltpu.make_async_copy(yb3_hbm.at[idx_smem[2 * t + 1]], buf1.at[dst], row_sem).start()
        return carry

    lax.fori_loop(0, tc, body, 0)
    pltpu.make_async_copy(buf0, buf0, row_sem).wait()
    pltpu.make_async_copy(buf1, buf1, row_sem).wait()
    g1 = route_ref[:, 2:3]
    g2 = route_ref[:, 3:4]
    for s in range(ROW_TILES):
        lanes = slice(s * LANES, (s + 1) * LANES)
        y1 = buf0[pl.ds(s, tc, stride=ROW_TILES), :]
        y2 = buf1[pl.ds(s, tc, stride=ROW_TILES), :]
        out_ref[:, lanes] = x1_ref[:, lanes] + (g1 * y1 + g2 * y2)


def _combine(dest, x1, route, yb3):
    t = x1.shape[0]
    tc = min(ROW_DMA_TILE, t)
    dest_flat = _tiled_indices(dest.reshape(t // tc, 2 * tc))
    return pl.pallas_call(
        _combine_kernel,
        grid=(t // tc,),
        in_specs=[pl.BlockSpec(memory_space=pl.ANY),
                  pl.BlockSpec((tc, D_MODEL), lambda i: (i, 0)),
                  pl.BlockSpec((tc, ROUTE_COLS), lambda i: (i, 0)),
                  pl.BlockSpec(memory_space=pl.ANY)],
        out_specs=pl.BlockSpec((tc, D_MODEL), lambda i: (i, 0)),
        out_shape=jax.ShapeDtypeStruct((t, D_MODEL), F32),
        scratch_shapes=[pltpu.SMEM((_index_tile_len(2 * tc),), jnp.int32),
                        pltpu.VMEM((tc * ROW_TILES, LANES), F32), pltpu.VMEM((tc * ROW_TILES, LANES), F32),
                        pltpu.SemaphoreType.DMA, pltpu.SemaphoreType.DMA],
        compiler_params=_params(), name="moe_combine",
    )(dest_flat, x1, route, yb3)


def _count_le(sorted_ends, values):
    return jnp.sum((sorted_ends[None, :] <= values[:, None]).astype(jnp.int32), axis=1)


def _moe_block_rows(t):
    mean_rows_per_expert = 2 * t // N_EXPERTS
    return min(MOE_BLOCK_ROWS, max(BF16_TILE_ROWS, mean_rows_per_expert))


def _moe(x1, hn3, route, counts, wg, wu, wd):
    t = x1.shape[0]
    blk = _moe_block_rows(t)
    expert = route[:, 0:2].astype(jnp.int32)
    rank = route[:, 4:6].astype(jnp.int32)
    cnt = counts[0, :N_EXPERTS].astype(jnp.int32)
    padded = (cnt + blk - 1) // blk * blk
    pad_end = jnp.cumsum(padded)
    pad_start = pad_end - padded
    dest = pad_start[expert] + rank
    n_blocks = -(-2 * t // blk) + N_EXPERTS
    blk_start = jnp.arange(n_blocks, dtype=jnp.int32) * blk
    block_expert = jnp.minimum(_count_le(pad_end, blk_start), N_EXPERTS - 1)
    block_nvalid = jnp.clip(pad_start[block_expert] + cnt[block_expert] - blk_start, 0, blk).astype(jnp.int32)
    n_rows = n_blocks * blk
    gap_start = jnp.concatenate([pad_start + cnt, pad_end[-1:]])
    gap_len = jnp.concatenate([padded - cnt, n_rows - pad_end[-1:]])
    gap_end = jnp.cumsum(gap_len)
    p = jnp.arange(n_rows - 2 * t, dtype=jnp.int32)
    seg = _count_le(gap_end, p)
    pad_rows = (gap_start[seg] + p - (gap_end[seg] - gap_len[seg])).astype(jnp.int32)
    xs3 = _dispatch(dest, pad_rows, hn3.reshape(t, ROW_TILES, LANES), n_rows)
    yb2 = _experts(block_expert, block_nvalid, xs3.reshape(n_rows * ROW_TILES, LANES), wg, wu, wd)
    return _combine(dest, x1, route, yb2.reshape(n_rows, ROW_TILES, LANES))


def _router_weights(w_group, w_router):
    w = jnp.concatenate([w_router, w_group,
                         jnp.zeros((D_MODEL, ROUTE_LANES - N_EXPERTS - MOE_GROUPS), F32)], axis=1)
    hi = w.astype(BF16)
    return hi, (w - hi.astype(F32)).astype(BF16)


def _rope_tables(pos):
    half = ROT_DIM // 2
    inv_freq = jnp.power(ROPE_THETA, -jnp.arange(half, dtype=F32) * (2.0 / ROT_DIM))
    ang = pos.astype(F32)[:, None] * inv_freq[None, :]
    cos, sin = jnp.cos(ang), jnp.sin(ang)
    n = pos.shape[0]
    ones = jnp.ones((n, DIFF_HEAD_DIM - ROT_DIM), F32)
    zeros = jnp.zeros((n, DIFF_HEAD_DIM - ROT_DIM), F32)
    zh = jnp.zeros((n, half), F32)
    sub_cos = jnp.concatenate([cos, cos, ones], axis=1)
    sub_lo = jnp.concatenate([-sin, zh, zeros], axis=1)
    sub_hi = jnp.concatenate([zh, sin, zeros], axis=1)
    return tuple(jnp.concatenate([a, a], axis=1) for a in (sub_cos, sub_lo, sub_hi))


def _spatial_weights(w_s, b_s, lc):
    pos = jnp.arange(lc)
    mask = (pos[None, :] // CHUNK) <= (pos[:, None] // CHUNK)
    ws = jnp.where(mask[None], w_s[:, :lc, :lc], 0.0)
    reps = GMLP_CHUNK // lc
    eye = jnp.eye(reps, dtype=F32)
    ws_eff = jnp.einsum("ab,gij->gaibj", eye, ws).reshape(GMLP_GROUPS, GMLP_CHUNK, GMLP_CHUNK)
    b_rows = jnp.tile(b_s[:, :lc], (1, reps))
    b_exp = jnp.repeat(b_rows.T, GMLP_GROUP_DIM, axis=1)
    return ws_eff.astype(BF16), b_exp


def kernel(x_prompt, x_sample, cache_attn_k, cache_attn_v, mix_norm, ffn_norm, gmlp_w_in, gmlp_v_norm, gmlp_w_s, gmlp_b_s, gmlp_w_out, attn_w_in, attn_q_norm, attn_k_norm, attn_lam_q1, attn_lam_k1, attn_lam_q2, attn_lam_k2, attn_sub_norm, attn_w_out, moe_w_group, moe_w_router, moe_w_gate, moe_w_up, moe_w_down):
    nb_p, seq, _ = x_prompt.shape
    nb_s, dec, _ = x_sample.shape
    past = cache_attn_k.shape[2]
    xp = x_prompt.reshape(nb_p * seq, D_MODEL)
    xs = x_sample.reshape(nb_s * dec, D_MODEL)
    row = lambda a: a.reshape(1, -1)

    router0 = _router_weights(moe_w_group[0], moe_w_router[0])
    gm = (row(mix_norm[0]), gmlp_w_in[0].astype(BF16), row(gmlp_v_norm[0]))
    gm_tail = (gmlp_w_out[0].astype(BF16), row(ffn_norm[0])) + router0
    experts0 = (moe_w_gate[0].astype(BF16), moe_w_up[0].astype(BF16), moe_w_down[0].astype(BF16))
    ws_p, b_p = _spatial_weights(gmlp_w_s[0], gmlp_b_s[0], GMLP_CHUNK)
    ws_s, b_s = _spatial_weights(gmlp_w_s[0], gmlp_b_s[0], dec)
    x1p, hn3p, routep, countsp, gv_p = _gmlp_layer(xp, nb_p, *gm, ws_p, b_p, *gm_tail)
    x1s, hn3s, routes, countss, gv_s = _gmlp_layer(xs, 1, *gm, ws_s, b_s, *gm_tail)
    xp = _moe(x1p, hn3p, routep, countsp, *experts0)
    xs = _moe(x1s, hn3s, routes, countss, *experts0)

    lam_init = 0.8 - 0.6 * math.exp(-0.3 * 1)
    lam = (jnp.exp(jnp.sum(attn_lam_q1[0] * attn_lam_k1[0])) - jnp.exp(jnp.sum(attn_lam_q2[0] * attn_lam_k2[0]))
           + lam_init).reshape(1).astype(F32)
    grp = jnp.arange(D_MODEL) // DIFF_HEAD_DIM
    group_ones = (grp[:, None] == grp[None, :]).astype(BF16)
    qk = (row(mix_norm[1]), attn_w_in[0].astype(BF16), group_ones,
          row(jnp.tile(attn_q_norm[0], 2 * DIFF_HEADS)), row(jnp.tile(attn_k_norm[0], 2 * DIFF_HEADS)))
    router1 = _router_weights(moe_w_group[1], moe_w_router[1])
    at_tail = (attn_w_out[0].astype(BF16), row(ffn_norm[1])) + router1
    experts1 = (moe_w_gate[1].astype(BF16), moe_w_up[1].astype(BF16), moe_w_down[1].astype(BF16))

    kp, vp, kb, vt, qt = _qkv_layer(xp, nb_p, *qk, *_rope_tables(jnp.arange(seq)), prompt=True)
    ap = _flash_attention(lam, qt, kb, vt, attn_sub_norm[0].reshape(LANES, 1), lam_init)
    x1p, hn3p, routep, countsp = _attn_out_layer(ap.reshape(nb_p * seq, D_MODEL), xp, *at_tail)
    xp = _moe(x1p, hn3p, routep, countsp, *experts1)

    pos_s = jnp.tile(past + jnp.arange(dec), nb_s)
    qs, ks, vs = _qkv_layer(xs, nb_s, *qk, *_rope_tables(pos_s), prompt=False)
    shp = (nb_s, dec, D_MODEL)
    a_s = _sample_attention(lam, qs.reshape(shp), ks.reshape(shp), vs.reshape(shp),
                            cache_attn_k[0].reshape(nb_s, past * DIFF_HEADS, LANES),
                            cache_attn_v[0].reshape(nb_s, past * DIFF_HEADS, LANES),
                            row(attn_sub_norm[0]), lam_init)
    x1s, hn3s, routes, countss = _attn_out_layer(a_s.reshape(nb_s * dec, D_MODEL), xs, *at_tail)
    xs = _moe(x1s, hn3s, routes, countss, *experts1)

    hv = (DIFF_HEADS, DIFF_VALUE_DIM)
    return (xp.reshape(nb_p, seq, D_MODEL), xs.reshape(nb_s, dec, D_MODEL),
            gv_p[None], gv_s.reshape(1, nb_s, dec, GMLP_WIDTH),
            kp.reshape(1, nb_p, seq, *hv), vp.reshape(1, nb_p, seq, *hv),
            ks.reshape(1, nb_s, dec, *hv), vs.reshape(1, nb_s, dec, *hv))
```

```python
import functools
import math

import jax
import jax.numpy as jnp
from jax import lax
from jax.experimental import pallas as pl
from jax.experimental.pallas import tpu as pltpu

D_MODEL = 1024
DEPTH = 2
CHUNK = 64
GMLP_CHUNK = 128
GMLP_WIDTH = 2 * D_MODEL
GMLP_GROUPS = 8
GMLP_GROUP_DIM = GMLP_WIDTH // GMLP_GROUPS
DIFF_HEADS = 8
DIFF_HEAD_DIM = D_MODEL // (2 * DIFF_HEADS)
DIFF_VALUE_DIM = 2 * DIFF_HEAD_DIM
ROT_DIM = DIFF_HEAD_DIM // 4
ROPE_THETA = 500000.0
MOE_GROUPS = 4
MOE_EXPERTS_PER_GROUP = 8
N_EXPERTS = MOE_GROUPS * MOE_EXPERTS_PER_GROUP
EXPERT_HIDDEN = D_MODEL // 2
EPS = 1e-6

LANES = 128
SUBLANES = 8
BF16_TILE_ROWS = 2 * SUBLANES
ROW_TILES = D_MODEL // LANES
VMEM_LIMIT_BYTES = 56 * 1024 * 1024

ROUTE_LANES = LANES
GROUP_LANE0 = N_EXPERTS
ROUTE_COLS = 8
NEG_BIG = -1e30

MIX_TILE = 256
ATT_TILE = 256
FLASH_HEADS_PER_STEP = 2
MOE_BLOCK_ROWS = 256
ROW_DMA_TILE = 512
INDEX_SLICE_WORDS = 1024

F32 = jnp.float32
BF16 = jnp.bfloat16


def _params(n_axes=1):
    return pltpu.CompilerParams(dimension_semantics=("arbitrary",) * n_axes,
                                vmem_limit_bytes=VMEM_LIMIT_BYTES)


def _rms(x, g):
    return x * lax.rsqrt(jnp.mean(x * x, axis=-1, keepdims=True) + EPS) * g


def _dot(a, b):
    return jnp.dot(a, b, preferred_element_type=F32)


def _dot_f32(a, b):
    return jnp.dot(a, b, preferred_element_type=F32, precision=lax.Precision.HIGHEST)


def _const_spec(shape):
    return pl.BlockSpec(shape, lambda *_: (0,) * len(shape), pipeline_mode=pl.Buffered(1))


def _route_epilogue(x1, ffn_g, wr_hi, wr_lo, base_ref, hn3_ref, route_ref, counts_ref):
    tm = x1.shape[0]

    @pl.when(pl.program_id(0) == 0)
    def _():
        base_ref[...] = jnp.zeros_like(base_ref)

    hn = _rms(x1, ffn_g)
    for s in range(ROW_TILES):
        hn3_ref[pl.ds(s, tm, stride=ROW_TILES), :] = hn[:, s * LANES:(s + 1) * LANES]

    h_hi = hn.astype(BF16)
    h_lo = (hn - h_hi.astype(F32)).astype(BF16)
    logit = _dot(h_hi, wr_hi) + _dot(h_lo, wr_hi) + _dot(h_hi, wr_lo)

    lane = lax.broadcasted_iota(jnp.int32, (tm, ROUTE_LANES), 1)
    far = jnp.int32(4 * ROUTE_LANES)
    lg = jnp.where(lane >= GROUP_LANE0, jnp.where(lane < GROUP_LANE0 + MOE_GROUPS, logit, NEG_BIG), NEG_BIG)
    mg = jnp.max(lg, axis=1, keepdims=True)
    g_lane = jnp.min(jnp.where(lg == mg, lane, far), axis=1, keepdims=True)
    g_sel = g_lane - GROUP_LANE0
    p_g = 1.0 / jnp.sum(jnp.exp(lg - mg), axis=1, keepdims=True)

    lo_lane = g_sel * MOE_EXPERTS_PER_GROUP
    le = jnp.where(lane >= lo_lane, jnp.where(lane < lo_lane + MOE_EXPERTS_PER_GROUP, logit, NEG_BIG), NEG_BIG)
    m1 = jnp.max(le, axis=1, keepdims=True)
    j1 = jnp.min(jnp.where(le == m1, lane, far), axis=1, keepdims=True)
    le2 = jnp.where(lane == j1, NEG_BIG, le)
    m2 = jnp.max(le2, axis=1, keepdims=True)
    j2 = jnp.min(jnp.where(le2 == m2, lane, far), axis=1, keepdims=True)
    r = jnp.exp(m2 - m1)
    gate1 = p_g / (1.0 + r)
    gate2 = p_g * r / (1.0 + r)

    hit1 = lane == j1
    hit2 = lane == j2
    onehot = jnp.where(hit1, 1.0, jnp.where(hit2, 1.0, 0.0))
    row = lax.broadcasted_iota(jnp.int32, (tm, tm), 0)
    col = lax.broadcasted_iota(jnp.int32, (tm, tm), 1)
    earlier = jnp.where(row > col, 1.0, 0.0).astype(BF16)
    prefix = _dot(earlier, onehot.astype(BF16)) + base_ref[...]
    rank1 = jnp.sum(jnp.where(hit1, prefix, 0.0), axis=1, keepdims=True)
    rank2 = jnp.sum(jnp.where(hit2, prefix, 0.0), axis=1, keepdims=True)
    base_new = base_ref[...] + jnp.sum(onehot, axis=0, keepdims=True)
    base_ref[...] = base_new
    counts_ref[...] = base_new

    c = lax.broadcasted_iota(jnp.int32, (tm, ROUTE_COLS), 1)
    rec = jnp.where(c == 0, j1.astype(F32),
          jnp.where(c == 1, j2.astype(F32),
          jnp.where(c == 2, gate1,
          jnp.where(c == 3, gate2,
          jnp.where(c == 4, rank1,
          jnp.where(c == 5, rank2, 0.0))))))
    route_ref[...] = rec


def _route_out_shapes(t):
    return (jax.ShapeDtypeStruct((t * ROW_TILES, LANES), F32),
            jax.ShapeDtypeStruct((t, ROUTE_COLS), F32),
            jax.ShapeDtypeStruct((1, ROUTE_LANES), F32))


def _route_out_specs(tm):
    return (pl.BlockSpec((tm * ROW_TILES, LANES), lambda i: (i, 0)),
            pl.BlockSpec((tm, ROUTE_COLS), lambda i: (i, 0)),
            pl.BlockSpec((1, ROUTE_LANES), lambda i: (0, 0)))


def _route_in_specs():
    return [_const_spec((1, D_MODEL)), _const_spec((D_MODEL, ROUTE_LANES)), _const_spec((D_MODEL, ROUTE_LANES))]


def _gelu_tanh(x):
    cdf = 0.5 * (1.0 + jnp.tanh(math.sqrt(2.0 / math.pi) * (x + 0.044715 * (x * x * x))))
    return x * cdf


def _gmlp_kernel(x_ref, mixg_ref, win_ref, vng_ref, ws_ref, bexp_ref, wout_ref,
                 ffng_ref, wrhi_ref, wrlo_ref,
                 x1_ref, hn3_ref, route_ref, counts_ref, vlast_ref, base_ref, *, tiles_per_batch):
    tm = x_ref.shape[0]
    mm = win_ref.dtype
    dot = _dot if mm == BF16 else _dot_f32
    x = x_ref[...]
    h = _rms(x, mixg_ref[...]).astype(mm)
    z = _gelu_tanh(dot(h, win_ref[...]))
    u = z[:, :GMLP_WIDTH]
    vn = _rms(z[:, GMLP_WIDTH:], vng_ref[...])
    vb = vn.astype(mm)
    gated = []
    for c in range(tm // GMLP_CHUNK):
        rows = slice(c * GMLP_CHUNK, (c + 1) * GMLP_CHUNK)
        s = jnp.concatenate(
            [dot(ws_ref[g], vb[rows, g * GMLP_GROUP_DIM:(g + 1) * GMLP_GROUP_DIM])
             for g in range(GMLP_GROUPS)], axis=1) + bexp_ref[...]
        gated.append((u[rows] * s).astype(mm))
    y = dot(jnp.concatenate(gated, axis=0), wout_ref[...])
    x1 = x + y
    x1_ref[...] = x1

    @pl.when(pl.program_id(0) % tiles_per_batch == tiles_per_batch - 1)
    def _():
        vlast_ref[0] = vn[tm - GMLP_CHUNK:]

    _route_epilogue(x1, ffng_ref[...], wrhi_ref[...], wrlo_ref[...], base_ref, hn3_ref, route_ref, counts_ref)


def _gmlp_layer(x, n_batch, mix_g, w_in, vn_g, ws_eff, b_exp, w_out, ffn_g, wr_hi, wr_lo):
    t = x.shape[0]
    tm = min(MIX_TILE, t)
    tiles_per_batch = t // n_batch // tm
    gw = GMLP_WIDTH
    kern = functools.partial(_gmlp_kernel, tiles_per_batch=tiles_per_batch)
    return pl.pallas_call(
        kern,
        grid=(t // tm,),
        in_specs=[pl.BlockSpec((tm, D_MODEL), lambda i: (i, 0)),
                  _const_spec((1, D_MODEL)), _const_spec((D_MODEL, 2 * gw)), _const_spec((1, gw)),
                  _const_spec((GMLP_GROUPS, GMLP_CHUNK, GMLP_CHUNK)), _const_spec((GMLP_CHUNK, gw)),
                  _const_spec((gw, D_MODEL))] + _route_in_specs(),
        out_specs=(pl.BlockSpec((tm, D_MODEL), lambda i: (i, 0)),) + _route_out_specs(tm)
                  + (pl.BlockSpec((1, GMLP_CHUNK, gw), lambda i: (i // tiles_per_batch, 0, 0)),),
        out_shape=(jax.ShapeDtypeStruct((t, D_MODEL), F32),) + _route_out_shapes(t)
                  + (jax.ShapeDtypeStruct((n_batch, GMLP_CHUNK, gw), F32),),
        scratch_shapes=[pltpu.VMEM((1, ROUTE_LANES), F32)],
        compiler_params=_params(),
        name="gmlp_mixer",
    )(x, mix_g, w_in, vn_g, ws_eff, b_exp, w_out, ffn_g, wr_hi, wr_lo)


def _qk_norm_rope(t, group_ones, gain, cos, sin_lo, sin_hi):
    ms = _dot((t * t).astype(BF16), group_ones) * (1.0 / DIFF_HEAD_DIM)
    tn = t * lax.rsqrt(ms + EPS) * gain
    heads = []
    for h in range(DIFF_HEADS):
        th = tn[:, h * LANES:(h + 1) * LANES]
        heads.append(th * cos + pltpu.roll(th, LANES - ROT_DIM // 2, 1) * sin_lo
                     + pltpu.roll(th, ROT_DIM // 2, 1) * sin_hi)
    return jnp.concatenate(heads, axis=1)


def _qkv_kernel(x_ref, mixg_ref, win_ref, ones_ref, qg_ref, kg_ref, cos_ref, slo_ref, shi_ref, *outs, prompt):
    tm = x_ref.shape[0]
    h = _rms(x_ref[...], mixg_ref[...]).astype(BF16)
    qkv = _dot(h, win_ref[...])
    rope = (cos_ref[...], slo_ref[...], shi_ref[...])
    q_scale = DIFF_HEAD_DIM ** -0.5 * (math.log2(math.e) if prompt else 1.0)
    q = _qk_norm_rope(qkv[:, :D_MODEL], ones_ref[...], qg_ref[...], *rope) * q_scale
    k = _qk_norm_rope(qkv[:, D_MODEL:2 * D_MODEL], ones_ref[...], kg_ref[...], *rope)
    v = qkv[:, 2 * D_MODEL:]
    if not prompt:
        q_ref, kout_ref, vout_ref = outs
        q_ref[...] = q
        kout_ref[...] = k
        vout_ref[...] = v
        return
    kout_ref, vout_ref, kb_ref, vt_ref, qt_ref = outs
    kout_ref[...] = k
    vout_ref[...] = v
    for hd in range(DIFF_HEADS):
        lanes = slice(hd * LANES, (hd + 1) * LANES)
        kb_ref[0, hd] = k[:, lanes].astype(BF16)
        for c in range(tm // ATT_TILE):
            rows = slice(c * ATT_TILE, (c + 1) * ATT_TILE)
            vt_ref[0, hd, c] = v[rows, lanes].T.astype(BF16)
            qt_ref[0, hd, c] = q[rows, lanes].T.astype(BF16)


def _qkv_layer(x, n_batch, mix_g, w_in, group_ones, q_g, k_g, cos, sin_lo, sin_hi, prompt):
    t = x.shape[0]
    tm = min(MIX_TILE, t)
    seq = t // n_batch
    tpb = seq // tm if prompt else 1
    n_tab = cos.shape[0] // tm
    tab_spec = pl.BlockSpec((tm, LANES), lambda i: (i % n_tab, 0))
    row_spec = pl.BlockSpec((tm, D_MODEL), lambda i: (i, 0))
    in_specs = [row_spec, _const_spec((1, D_MODEL)), _const_spec((D_MODEL, 3 * D_MODEL)),
                _const_spec((D_MODEL, D_MODEL)), _const_spec((1, D_MODEL)), _const_spec((1, D_MODEL)),
                tab_spec, tab_spec, tab_spec]
    row_shape = jax.ShapeDtypeStruct((t, D_MODEL), F32)
    if prompt:
        nq = seq // ATT_TILE
        cpt = tm // ATT_TILE
        out_specs = (row_spec, row_spec,
                     pl.BlockSpec((1, DIFF_HEADS, tm, LANES), lambda i: (i // tpb, 0, i % tpb, 0)),
                     pl.BlockSpec((1, DIFF_HEADS, cpt, LANES, ATT_TILE), lambda i: (i // tpb, 0, i % tpb, 0, 0)),
                     pl.BlockSpec((1, DIFF_HEADS, cpt, LANES, ATT_TILE), lambda i: (i // tpb, 0, i % tpb, 0, 0)))
        out_shape = (row_shape, row_shape,
                     jax.ShapeDtypeStruct((n_batch, DIFF_HEADS, seq, LANES), BF16),
                     jax.ShapeDtypeStruct((n_batch, DIFF_HEADS, nq, LANES, ATT_TILE), BF16),
                     jax.ShapeDtypeStruct((n_batch, DIFF_HEADS, nq, LANES, ATT_TILE), BF16))
    else:
        out_specs = (row_spec, row_spec, row_spec)
        out_shape = (row_shape, row_shape, row_shape)
    return pl.pallas_call(
        functools.partial(_qkv_kernel, prompt=prompt),
        grid=(t // tm,), in_specs=in_specs, out_specs=out_specs, out_shape=out_shape,
        compiler_params=_params(), name="qkv_prompt" if prompt else "qkv_sample",
    )(x, mix_g, w_in, group_ones, q_g, k_g, cos, sin_lo, sin_hi)


def _flash_kernel(lam_ref, qt_ref, k_ref, vt_ref, sg_ref, o_ref, *, lam_init):
    tq = ATT_TILE
    n_heads = qt_ref.shape[1]
    i = pl.program_id(2)
    frow = lax.broadcasted_iota(jnp.int32, (LANES, tq), 0)
    qbd = []
    for g in range(n_heads):
        qt = qt_ref[0, g, 0]
        zero = jnp.zeros_like(qt)
        qbd.append(jnp.concatenate([jnp.where(frow < DIFF_HEAD_DIM, qt, zero),
                                    jnp.where(frow >= DIFF_HEAD_DIM, qt, zero)], axis=1))

    def head_step(g, j, carry, masked):
        m, l, acc = carry
        kt = k_ref[0, g, pl.ds(pl.multiple_of(j * tq, tq), tq), :]
        s = _dot(kt, qbd[g])
        if masked:
            kpos = lax.broadcasted_iota(jnp.int32, (tq, 2 * tq), 0)
            qpos = lax.broadcasted_iota(jnp.int32, (tq, 2 * tq), 1) % tq
            s = jnp.where(kpos // CHUNK <= qpos // CHUNK, s, NEG_BIG)
        m_new = jnp.maximum(m, jnp.max(s, axis=0, keepdims=True))
        p = jnp.exp2(s - m_new)
        alpha = jnp.exp2(m - m_new)
        l = alpha * l + jnp.sum(p, axis=0, keepdims=True)
        acc = alpha * acc + _dot(vt_ref[0, g, j], p.astype(BF16))
        return m_new, l, acc

    def step(j, carries, masked):
        return tuple(head_step(g, j, carries[g], masked) for g in range(n_heads))

    init = (jnp.full((1, 2 * tq), NEG_BIG, F32), jnp.zeros((1, 2 * tq), F32), jnp.zeros((LANES, 2 * tq), F32))
    carries = lax.fori_loop(0, i, lambda j, c: step(j, c, False), (init,) * n_heads)
    carries = step(i, carries, True)
    lam = lam_ref[0]
    for g in range(n_heads):
        _, l, acc = carries[g]
        o = acc[:, :tq] / l[:, :tq] - lam * (acc[:, tq:] / l[:, tq:])
        on = o * lax.rsqrt(jnp.mean(o * o, axis=0, keepdims=True) + EPS) * sg_ref[...] * (1.0 - lam_init)
        o_ref[0, :, g * LANES:(g + 1) * LANES] = on.T.astype(o_ref.dtype)


def _flash_attention(lam, qt, kb, vt, sub_g_col, lam_init):
    n_batch, _, nq, _, tq = qt.shape
    seq = nq * tq
    hg = FLASH_HEADS_PER_STEP
    return pl.pallas_call(
        functools.partial(_flash_kernel, lam_init=lam_init),
        grid=(n_batch, DIFF_HEADS // hg, nq),
        in_specs=[pl.BlockSpec(memory_space=pltpu.SMEM),
                  pl.BlockSpec((1, hg, 1, LANES, tq), lambda b, h, i: (b, h, i, 0, 0)),
                  pl.BlockSpec((1, hg, seq, LANES), lambda b, h, i: (b, h, 0, 0)),
                  pl.BlockSpec((1, hg, nq, LANES, tq), lambda b, h, i: (b, h, 0, 0, 0)),
                  pl.BlockSpec((LANES, 1), lambda b, h, i: (0, 0))],
        out_specs=pl.BlockSpec((1, tq, hg * LANES), lambda b, h, i: (b, i, h)),
        out_shape=jax.ShapeDtypeStruct((n_batch, seq, D_MODEL), BF16),
        compiler_params=_params(3), name="diff_flash",
    )(lam, qt, kb, vt, sub_g_col)


def _sample_attn_kernel(lam_ref, q_ref, kn_ref, vn_ref, ck_ref, cv_ref, sg_ref, o_ref, *, lam_init, past):
    q = q_ref[0]
    kn = kn_ref[0]
    vn = vn_ref[0]
    lam = lam_ref[0]
    lane = lax.broadcasted_iota(jnp.int32, (q.shape[0], LANES), 1)
    contract_last = (((1,), (1,)), ((), ()))
    outs = []
    for h in range(DIFF_HEADS):
        lanes = slice(h * LANES, (h + 1) * LANES)
        kc = ck_ref[0, pl.ds(h, past, stride=DIFF_HEADS), :].astype(BF16)
        vc = cv_ref[0, pl.ds(h, past, stride=DIFF_HEADS), :].astype(BF16)
        qh = q[:, lanes]
        knh = kn[:, lanes].astype(BF16)
        vnh = vn[:, lanes].astype(BF16)
        sub = []
        for c in range(2):
            keep = (lane < DIFF_HEAD_DIM) if c == 0 else (lane >= DIFF_HEAD_DIM)
            qc = jnp.where(keep, qh, 0.0).astype(BF16)
            s_old = lax.dot_general(qc, kc, contract_last, preferred_element_type=F32)
            s_new = lax.dot_general(qc, knh, contract_last, preferred_element_type=F32)
            m = jnp.maximum(jnp.max(s_old, axis=1, keepdims=True), jnp.max(s_new, axis=1, keepdims=True))
            p_old = jnp.exp(s_old - m)
            p_new = jnp.exp(s_new - m)
            l = jnp.sum(p_old, axis=1, keepdims=True) + jnp.sum(p_new, axis=1, keepdims=True)
            sub.append((_dot(p_old.astype(BF16), vc) + _dot(p_new.astype(BF16), vnh)) / l)
        o = sub[0] - lam * sub[1]
        outs.append(_rms(o, sg_ref[...]) * (1.0 - lam_init))
    o_ref[0] = jnp.concatenate(outs, axis=1).astype(o_ref.dtype)


def _sample_attention(lam, q, kn, vn, cache_k, cache_v, sub_g_row, lam_init):
    nb, rows, _ = q.shape
    past = cache_k.shape[1] // DIFF_HEADS
    new_spec = pl.BlockSpec((1, rows, D_MODEL), lambda b: (b, 0, 0))
    cache_spec = pl.BlockSpec((1, past * DIFF_HEADS, LANES), lambda b: (b, 0, 0))
    return pl.pallas_call(
        functools.partial(_sample_attn_kernel, lam_init=lam_init, past=past),
        grid=(nb,),
        in_specs=[pl.BlockSpec(memory_space=pltpu.SMEM), new_spec, new_spec, new_spec, cache_spec, cache_spec,
                  _const_spec((1, LANES))],
        out_specs=new_spec, out_shape=jax.ShapeDtypeStruct((nb, rows, D_MODEL), BF16),
        compiler_params=_params(), name="sample_attn",
    )(lam, q, kn, vn, cache_k, cache_v, sub_g_row)


def _attn_out_kernel(a_ref, x_ref, wout_ref, ffng_ref, wrhi_ref, wrlo_ref,
                     x1_ref, hn3_ref, route_ref, counts_ref, base_ref):
    x1 = x_ref[...] + _dot(a_ref[...], wout_ref[...])
    x1_ref[...] = x1
    _route_epilogue(x1, ffng_ref[...], wrhi_ref[...], wrlo_ref[...], base_ref, hn3_ref, route_ref, counts_ref)


def _attn_out_layer(a, x, w_out, ffn_g, wr_hi, wr_lo):
    t = x.shape[0]
    tm = min(MIX_TILE, t)
    row_spec = pl.BlockSpec((tm, D_MODEL), lambda i: (i, 0))
    return pl.pallas_call(
        _attn_out_kernel,
        grid=(t // tm,),
        in_specs=[row_spec, row_spec, _const_spec((D_MODEL, D_MODEL))] + _route_in_specs(),
        out_specs=(row_spec,) + _route_out_specs(tm),
        out_shape=(jax.ShapeDtypeStruct((t, D_MODEL), F32),) + _route_out_shapes(t),
        scratch_shapes=[pltpu.VMEM((1, ROUTE_LANES), F32)],
        compiler_params=_params(), name="attn_out",
    )(a, x, w_out, ffn_g, wr_hi, wr_lo)


def _load_tile_indices(dest_hbm, idx_smem, idx_sem):
    n = idx_smem.shape[0]
    start = pl.multiple_of(pl.program_id(0) * n, n)
    idx_copy = pltpu.make_async_copy(dest_hbm.at[pl.ds(start, n)], idx_smem, idx_sem)
    idx_copy.start()
    idx_copy.wait()


def _index_tile_len(n_indices):
    return -(-n_indices // INDEX_SLICE_WORDS) * INDEX_SLICE_WORDS


def _tiled_indices(per_step):
    n = per_step.shape[1]
    return jnp.pad(per_step, ((0, 0), (0, _index_tile_len(n) - n))).reshape(-1)


def _dispatch_kernel(idx_hbm, hn_ref, xs3_hbm, idx_smem, zero_buf, idx_sem, row_sem, *, td, n_fill):
    _load_tile_indices(idx_hbm, idx_smem, idx_sem)
    zero_buf[...] = jnp.zeros_like(zero_buf)

    def scatter(t, carry):
        src = hn_ref.at[pl.ds(pl.multiple_of(t * ROW_TILES, ROW_TILES), ROW_TILES)]
        pltpu.make_async_copy(src, xs3_hbm.at[idx_smem[2 * t]], row_sem).start()
        pltpu.make_async_copy(src, xs3_hbm.at[idx_smem[2 * t + 1]], row_sem).start()
        return carry

    def fill(p, carry):
        pltpu.make_async_copy(zero_buf, xs3_hbm.at[idx_smem[2 * td + p]], row_sem).start()
        return carry

    lax.fori_loop(0, td, scatter, 0)
    lax.fori_loop(0, n_fill, fill, 0)
    done = xs3_hbm.at[pl.ds(0, 2 * td + n_fill)]
    pltpu.make_async_copy(done, done, row_sem).wait()


def _dispatch(dest, pad_rows, hn2, n_rows):
    t = hn2.shape[0] // ROW_TILES
    td = min(ROW_DMA_TILE, t)
    steps = t // td
    n_fill = pad_rows.shape[0] // steps
    table = jnp.concatenate([dest.reshape(steps, 2 * td), pad_rows.reshape(steps, n_fill)], axis=1)
    return pl.pallas_call(
        functools.partial(_dispatch_kernel, td=td, n_fill=n_fill),
        grid=(steps,),
        in_specs=[pl.BlockSpec(memory_space=pl.ANY),
                  pl.BlockSpec((td * ROW_TILES, LANES), lambda i: (i, 0))],
        out_specs=pl.BlockSpec(memory_space=pl.ANY),
        out_shape=jax.ShapeDtypeStruct((n_rows, ROW_TILES, LANES), F32),
        scratch_shapes=[pltpu.SMEM((_index_tile_len(2 * td + n_fill),), jnp.int32),
                        pltpu.VMEM((ROW_TILES, LANES), F32),
                        pltpu.SemaphoreType.DMA, pltpu.SemaphoreType.DMA],
        compiler_params=_params(), name="moe_dispatch",
    )(_tiled_indices(table), hn2)


def _expert_kernel(bexp_ref, nvalid_ref, xs_ref, wg_ref, wu_ref, wd_ref, yb_ref):
    blk = xs_ref.shape[0] // ROW_TILES
    nvalid = nvalid_ref[pl.program_id(0)]

    @pl.when(nvalid > 0)
    def _():
        x = jnp.concatenate([xs_ref[pl.ds(s, blk, stride=ROW_TILES), :] for s in range(ROW_TILES)], axis=1)
        row = lax.broadcasted_iota(jnp.int32, (blk, 1), 0)
        xb = jnp.where(row < nvalid, x, 0.0).astype(BF16)
        hg = _dot(xb, wg_ref[0])
        hu = _dot(xb, wu_ref[0])
        act = (hg * (1.0 / (1.0 + jnp.exp(-hg))) * hu).astype(BF16)
        y = _dot(act, wd_ref[0])
        for s in range(ROW_TILES):
            yb_ref[pl.ds(s, blk, stride=ROW_TILES), :] = y[:, s * LANES:(s + 1) * LANES]

    @pl.when(nvalid <= 0)
    def _():
        yb_ref[...] = jnp.zeros_like(yb_ref)


def _experts(block_expert, block_nvalid, xs2, wg, wu, wd):
    n_blocks = block_expert.shape[0]
    blk = xs2.shape[0] // ROW_TILES // n_blocks
    rows_spec = pl.BlockSpec((blk * ROW_TILES, LANES), lambda i, be, nv: (i, 0))
    return pl.pallas_call(
        _expert_kernel,
        grid_spec=pltpu.PrefetchScalarGridSpec(
            num_scalar_prefetch=2, grid=(n_blocks,),
            in_specs=[rows_spec,
                      pl.BlockSpec((1, D_MODEL, EXPERT_HIDDEN), lambda i, be, nv: (be[i], 0, 0)),
                      pl.BlockSpec((1, D_MODEL, EXPERT_HIDDEN), lambda i, be, nv: (be[i], 0, 0)),
                      pl.BlockSpec((1, EXPERT_HIDDEN, D_MODEL), lambda i, be, nv: (be[i], 0, 0))],
            out_specs=rows_spec),
        out_shape=jax.ShapeDtypeStruct(xs2.shape, F32),
        compiler_params=_params(), name="moe_experts",
    )(block_expert, block_nvalid, xs2, wg, wu, wd)


def _combine_kernel(dest_hbm, x1_ref, route_ref, yb3_hbm, out_ref, idx_smem, buf0, buf1, idx_sem, row_sem):
    tc = x1_ref.shape[0]
    _load_tile_indices(dest_hbm, idx_smem, idx_sem)

    def body(t, carry):
        dst = pl.ds(pl.multiple_of(t * ROW_TILES, ROW_TILES), ROW_TILES)
        pltpu.make_async_copy(yb3_hbm.at[idx_smem[2 * t]], buf0.at[dst], row_sem).start()
        pltpu.make_async_copy(yb3_hbm.at[idx_smem[2 * t + 1]], buf1.at[dst], row_sem).start()
        return carry

    lax.fori_loop(0, tc, body, 0)
    pltpu.make_async_copy(buf0, buf0, row_sem).wait()
    pltpu.make_async_copy(buf1, buf1, row_sem).wait()
    g1 = route_ref[:, 2:3]
    g2 = route_ref[:, 3:4]
    for s in range(ROW_TILES):
        lanes = slice(s * LANES, (s + 1) * LANES)
        y1 = buf0[pl.ds(s, tc, stride=ROW_TILES), :]
        y2 = buf1[pl.ds(s, tc, stride=ROW_TILES), :]
        out_ref[:, lanes] = x1_ref[:, lanes] + (g1 * y1 + g2 * y2)


def _combine(dest, x1, route, yb3):
    t = x1.shape[0]
    tc = min(ROW_DMA_TILE, t)
    dest_flat = _tiled_indices(dest.reshape(t // tc, 2 * tc))
    return pl.pallas_call(
        _combine_kernel,
        grid=(t // tc,),
        in_specs=[pl.BlockSpec(memory_space=pl.ANY),
                  pl.BlockSpec((tc, D_MODEL), lambda i: (i, 0)),
                  pl.BlockSpec((tc, ROUTE_COLS), lambda i: (i, 0)),
                  pl.BlockSpec(memory_space=pl.ANY)],
        out_specs=pl.BlockSpec((tc, D_MODEL), lambda i: (i, 0)),
        out_shape=jax.ShapeDtypeStruct((t, D_MODEL), F32),
        scratch_shapes=[pltpu.SMEM((_index_tile_len(2 * tc),), jnp.int32),
                        pltpu.VMEM((tc * ROW_TILES, LANES), F32), pltpu.VMEM((tc * ROW_TILES, LANES), F32),
                        pltpu.SemaphoreType.DMA, pltpu.SemaphoreType.DMA],
        compiler_params=_params(), name="moe_combine",
    )(dest_flat, x1, route, yb3)


def _count_le(sorted_ends, values):
    return jnp.sum((sorted_ends[None, :] <= values[:, None]).astype(jnp.int32), axis=1)


def _moe_block_rows(t):
    mean_rows_per_expert = 2 * t // N_EXPERTS
    return min(MOE_BLOCK_ROWS, max(BF16_TILE_ROWS, mean_rows_per_expert))


def _moe(x1, hn3, route, counts, wg, wu, wd):
    t = x1.shape[0]
    blk = _moe_block_rows(t)
    expert = route[:, 0:2].astype(jnp.int32)
    rank = route[:, 4:6].astype(jnp.int32)
    cnt = counts[0, :N_EXPERTS].astype(jnp.int32)
    padded = (cnt + blk - 1) // blk * blk
    pad_end = jnp.cumsum(padded)
    pad_start = pad_end - padded
    dest = pad_start[expert] + rank
    n_blocks = -(-2 * t // blk) + N_EXPERTS
    blk_start = jnp.arange(n_blocks, dtype=jnp.int32) * blk
    block_expert = jnp.minimum(_count_le(pad_end, blk_start), N_EXPERTS - 1)
    block_nvalid = jnp.clip(pad_start[block_expert] + cnt[block_expert] - blk_start, 0, blk).astype(jnp.int32)
    n_rows = n_blocks * blk
    gap_start = jnp.concatenate([pad_start + cnt, pad_end[-1:]])
    gap_len = jnp.concatenate([padded - cnt, n_rows - pad_end[-1:]])
    gap_end = jnp.cumsum(gap_len)
    p = jnp.arange(n_rows - 2 * t, dtype=jnp.int32)
    seg = _count_le(gap_end, p)
    pad_rows = (gap_start[seg] + p - (gap_end[seg] - gap_len[seg])).astype(jnp.int32)
    xs3 = _dispatch(dest, pad_rows, hn3, n_rows)
    yb2 = _experts(block_expert, block_nvalid, xs3.reshape(n_rows * ROW_TILES, LANES), wg, wu, wd)
    return _combine(dest, x1, route, yb2.reshape(n_rows, ROW_TILES, LANES))


def _router_weights(w_group, w_router):
    w = jnp.concatenate([w_router, w_group,
                         jnp.zeros((D_MODEL, ROUTE_LANES - N_EXPERTS - MOE_GROUPS), F32)], axis=1)
    hi = w.astype(BF16)
    return hi, (w - hi.astype(F32)).astype(BF16)


def _rope_tables(pos):
    half = ROT_DIM // 2
    inv_freq = jnp.power(ROPE_THETA, -jnp.arange(half, dtype=F32) * (2.0 / ROT_DIM))
    ang = pos.astype(F32)[:, None] * inv_freq[None, :]
    cos, sin = jnp.cos(ang), jnp.sin(ang)
    n = pos.shape[0]
    ones = jnp.ones((n, DIFF_HEAD_DIM - ROT_DIM), F32)
    zeros = jnp.zeros((n, DIFF_HEAD_DIM - ROT_DIM), F32)
    zh = jnp.zeros((n, half), F32)
    sub_cos = jnp.concatenate([cos, cos, ones], axis=1)
    sub_lo = jnp.concatenate([-sin, zh, zeros], axis=1)
    sub_hi = jnp.concatenate([zh, sin, zeros], axis=1)
    return tuple(jnp.concatenate([a, a], axis=1) for a in (sub_cos, sub_lo, sub_hi))


def _spatial_weights(w_s, b_s, lc):
    pos = jnp.arange(lc)
    mask = (pos[None, :] // CHUNK) <= (pos[:, None] // CHUNK)
    ws = jnp.where(mask[None], w_s[:, :lc, :lc], 0.0)
    reps = GMLP_CHUNK // lc
    eye = jnp.eye(reps, dtype=F32)
    ws_eff = jnp.einsum("ab,gij->gaibj", eye, ws).reshape(GMLP_GROUPS, GMLP_CHUNK, GMLP_CHUNK)
    b_rows = jnp.tile(b_s[:, :lc], (1, reps))
    b_exp = jnp.repeat(b_rows.T, GMLP_GROUP_DIM, axis=1)
    return ws_eff, b_exp


def kernel(x_prompt, x_sample, cache_attn_k, cache_attn_v, mix_norm, ffn_norm, gmlp_w_in, gmlp_v_norm, gmlp_w_s, gmlp_b_s, gmlp_w_out, attn_w_in, attn_q_norm, attn_k_norm, attn_lam_q1, attn_lam_k1, attn_lam_q2, attn_lam_k2, attn_sub_norm, attn_w_out, moe_w_group, moe_w_router, moe_w_gate, moe_w_up, moe_w_down):
    nb_p, seq, _ = x_prompt.shape
    nb_s, dec, _ = x_sample.shape
    past = cache_attn_k.shape[2]
    xp = x_prompt.reshape(nb_p * seq, D_MODEL)
    xs = x_sample.reshape(nb_s * dec, D_MODEL)
    row = lambda a: a.reshape(1, -1)

    router0 = _router_weights(moe_w_group[0], moe_w_router[0])
    gm_tail = (row(ffn_norm[0]),) + router0
    experts0 = (moe_w_gate[0].astype(BF16), moe_w_up[0].astype(BF16), moe_w_down[0].astype(BF16))
    ws_p, b_p = _spatial_weights(gmlp_w_s[0], gmlp_b_s[0], GMLP_CHUNK)
    ws_s, b_s = _spatial_weights(gmlp_w_s[0], gmlp_b_s[0], dec)
    x1p, hn3p, routep, countsp, gv_p = _gmlp_layer(
        xp, nb_p, row(mix_norm[0]), gmlp_w_in[0].astype(BF16), row(gmlp_v_norm[0]), ws_p.astype(BF16), b_p,
        gmlp_w_out[0].astype(BF16), *gm_tail)
    x1s, hn3s, routes, countss, gv_s = _gmlp_layer(
        xs, 1, row(mix_norm[0]), gmlp_w_in[0], row(gmlp_v_norm[0]), ws_s, b_s, gmlp_w_out[0], *gm_tail)
    xp = _moe(x1p, hn3p, routep, countsp, *experts0)
    xs = _moe(x1s, hn3s, routes, countss, *experts0)

    lam_init = 0.8 - 0.6 * math.exp(-0.3 * 1)
    lam = (jnp.exp(jnp.sum(attn_lam_q1[0] * attn_lam_k1[0])) - jnp.exp(jnp.sum(attn_lam_q2[0] * attn_lam_k2[0]))
           + lam_init).reshape(1).astype(F32)
    grp = jnp.arange(D_MODEL) // DIFF_HEAD_DIM
    group_ones = (grp[:, None] == grp[None, :]).astype(BF16)
    qk = (row(mix_norm[1]), attn_w_in[0].astype(BF16), group_ones,
          row(jnp.tile(attn_q_norm[0], 2 * DIFF_HEADS)), row(jnp.tile(attn_k_norm[0], 2 * DIFF_HEADS)))
    router1 = _router_weights(moe_w_group[1], moe_w_router[1])
    at_tail = (attn_w_out[0].astype(BF16), row(ffn_norm[1])) + router1
    experts1 = (moe_w_gate[1].astype(BF16), moe_w_up[1].astype(BF16), moe_w_down[1].astype(BF16))

    kp, vp, kb, vt, qt = _qkv_layer(xp, nb_p, *qk, *_rope_tables(jnp.arange(seq)), prompt=True)
    ap = _flash_attention(lam, qt, kb, vt, attn_sub_norm[0].reshape(LANES, 1), lam_init)
    x1p, hn3p, routep, countsp = _attn_out_layer(ap.reshape(nb_p * seq, D_MODEL), xp, *at_tail)
    xp = _moe(x1p, hn3p, routep, countsp, *experts1)

    pos_s = jnp.tile(past + jnp.arange(dec), nb_s)
    qs, ks, vs = _qkv_layer(xs, nb_s, *qk, *_rope_tables(pos_s), prompt=False)
    shp = (nb_s, dec, D_MODEL)
    a_s = _sample_attention(lam, qs.reshape(shp), ks.reshape(shp), vs.reshape(shp),
                            cache_attn_k[0].reshape(nb_s, past * DIFF_HEADS, LANES),
                            cache_attn_v[0].reshape(nb_s, past * DIFF_HEADS, LANES),
                            row(attn_sub_norm[0]), lam_init)
    x1s, hn3s, routes, countss = _attn_out_layer(a_s.reshape(nb_s * dec, D_MODEL), xs, *at_tail)
    xs = _moe(x1s, hn3s, routes, countss, *experts1)

    hv = (DIFF_HEADS, DIFF_VALUE_DIM)
    return (xp.reshape(nb_p, seq, D_MODEL), xs.reshape(nb_s, dec, D_MODEL),
            gv_p[None], gv_s.reshape(1, nb_s, dec, GMLP_WIDTH),
            kp.reshape(1, nb_p, seq, *hv), vp.reshape(1, nb_p, seq, *hv),
            ks.reshape(1, nb_s, dec, *hv), vs.reshape(1, nb_s, dec, *hv))
```

```python
import functools
import math

import jax
import jax.numpy as jnp
from jax import lax
from jax.experimental import pallas as pl
from jax.experimental.pallas import tpu as pltpu

D_MODEL = 1024
DEPTH = 2
CHUNK = 64
GMLP_CHUNK = 128
GMLP_WIDTH = 2 * D_MODEL
GMLP_GROUPS = 8
GMLP_GROUP_DIM = GMLP_WIDTH // GMLP_GROUPS
DIFF_HEADS = 8
DIFF_HEAD_DIM = D_MODEL // (2 * DIFF_HEADS)
DIFF_VALUE_DIM = 2 * DIFF_HEAD_DIM
ROT_DIM = DIFF_HEAD_DIM // 4
ROPE_THETA = 500000.0
MOE_GROUPS = 4
MOE_EXPERTS_PER_GROUP = 8
N_EXPERTS = MOE_GROUPS * MOE_EXPERTS_PER_GROUP
EXPERT_HIDDEN = D_MODEL // 2
EPS = 1e-6

LANES = 128
SUBLANES = 8
BF16_TILE_ROWS = 2 * SUBLANES
ROW_TILES = D_MODEL // LANES
VMEM_LIMIT_BYTES = 56 * 1024 * 1024

ROUTE_LANES = LANES
GROUP_LANE0 = N_EXPERTS
ROUTE_COLS = 8
NEG_BIG = -1e30

MIX_TILE = 256
ATT_TILE = 256
FLASH_HEADS_PER_STEP = 2
MOE_BLOCK_ROWS = 256
ROW_DMA_TILE = 512
INDEX_SLICE_WORDS = 1024

F32 = jnp.float32
BF16 = jnp.bfloat16


def _params(n_axes=1):
    return pltpu.CompilerParams(dimension_semantics=("arbitrary",) * n_axes,
                                vmem_limit_bytes=VMEM_LIMIT_BYTES)


def _rms(x, g):
    return x * lax.rsqrt(jnp.mean(x * x, axis=-1, keepdims=True) + EPS) * g


def _dot(a, b):
    return jnp.dot(a, b, preferred_element_type=F32)


def _dot_f32(a, b):
    return jnp.dot(a, b, preferred_element_type=F32, precision=lax.Precision.HIGHEST)


def _const_spec(shape):
    return pl.BlockSpec(shape, lambda *_: (0,) * len(shape), pipeline_mode=pl.Buffered(1))


def _route_epilogue(x1, ffn_g, wr_hi, wr_lo, base_ref, hn3_ref, route_ref, route_t_ref, counts_ref):
    tm = x1.shape[0]

    @pl.when(pl.program_id(0) == 0)
    def _():
        base_ref[...] = jnp.zeros_like(base_ref)

    hn = _rms(x1, ffn_g)
    for s in range(ROW_TILES):
        hn3_ref[pl.ds(s, tm, stride=ROW_TILES), :] = hn[:, s * LANES:(s + 1) * LANES]

    h_hi = hn.astype(BF16)
    h_lo = (hn - h_hi.astype(F32)).astype(BF16)
    logit = _dot(h_hi, wr_hi) + _dot(h_lo, wr_hi) + _dot(h_hi, wr_lo)

    lane = lax.broadcasted_iota(jnp.int32, (tm, ROUTE_LANES), 1)
    far = jnp.int32(4 * ROUTE_LANES)
    lg = jnp.where(lane >= GROUP_LANE0, jnp.where(lane < GROUP_LANE0 + MOE_GROUPS, logit, NEG_BIG), NEG_BIG)
    mg = jnp.max(lg, axis=1, keepdims=True)
    g_lane = jnp.min(jnp.where(lg == mg, lane, far), axis=1, keepdims=True)
    g_sel = g_lane - GROUP_LANE0
    p_g = 1.0 / jnp.sum(jnp.exp(lg - mg), axis=1, keepdims=True)

    lo_lane = g_sel * MOE_EXPERTS_PER_GROUP
    le = jnp.where(lane >= lo_lane, jnp.where(lane < lo_lane + MOE_EXPERTS_PER_GROUP, logit, NEG_BIG), NEG_BIG)
    m1 = jnp.max(le, axis=1, keepdims=True)
    j1 = jnp.min(jnp.where(le == m1, lane, far), axis=1, keepdims=True)
    le2 = jnp.where(lane == j1, NEG_BIG, le)
    m2 = jnp.max(le2, axis=1, keepdims=True)
    j2 = jnp.min(jnp.where(le2 == m2, lane, far), axis=1, keepdims=True)
    r = jnp.exp(m2 - m1)
    gate1 = p_g / (1.0 + r)
    gate2 = p_g * r / (1.0 + r)

    hit1 = lane == j1
    hit2 = lane == j2
    onehot = jnp.where(hit1, 1.0, jnp.where(hit2, 1.0, 0.0))
    row = lax.broadcasted_iota(jnp.int32, (tm, tm), 0)
    col = lax.broadcasted_iota(jnp.int32, (tm, tm), 1)
    earlier = jnp.where(row > col, 1.0, 0.0).astype(BF16)
    prefix = _dot(earlier, onehot.astype(BF16)) + base_ref[...]
    rank1 = jnp.sum(jnp.where(hit1, prefix, 0.0), axis=1, keepdims=True)
    rank2 = jnp.sum(jnp.where(hit2, prefix, 0.0), axis=1, keepdims=True)
    base_new = base_ref[...] + jnp.sum(onehot, axis=0, keepdims=True)
    base_ref[...] = base_new
    counts_ref[...] = base_new

    rec = jnp.where(lane == 0, j1.astype(F32),
          jnp.where(lane == 1, j2.astype(F32),
          jnp.where(lane == 2, gate1,
          jnp.where(lane == 3, gate2,
          jnp.where(lane == 4, rank1,
          jnp.where(lane == 5, rank2, 0.0))))))
    route_ref[...] = rec[:, :ROUTE_COLS]
    route_t_ref[...] = rec.T[:ROUTE_COLS]


def _route_out_shapes(t):
    return (jax.ShapeDtypeStruct((t * ROW_TILES, LANES), F32),
            jax.ShapeDtypeStruct((t, ROUTE_COLS), F32),
            jax.ShapeDtypeStruct((ROUTE_COLS, t), F32),
            jax.ShapeDtypeStruct((1, ROUTE_LANES), F32))


def _route_out_specs(tm):
    return (pl.BlockSpec((tm * ROW_TILES, LANES), lambda i: (i, 0)),
            pl.BlockSpec((tm, ROUTE_COLS), lambda i: (i, 0)),
            pl.BlockSpec((ROUTE_COLS, tm), lambda i: (0, i)),
            pl.BlockSpec((1, ROUTE_LANES), lambda i: (0, 0)))


def _route_in_specs():
    return [_const_spec((1, D_MODEL)), _const_spec((D_MODEL, ROUTE_LANES)), _const_spec((D_MODEL, ROUTE_LANES))]


def _gelu_tanh(x):
    cdf = 0.5 * (1.0 + jnp.tanh(math.sqrt(2.0 / math.pi) * (x + 0.044715 * (x * x * x))))
    return x * cdf


def _gmlp_kernel(x_ref, mixg_ref, win_ref, vng_ref, ws_ref, bexp_ref, wout_ref,
                 ffng_ref, wrhi_ref, wrlo_ref,
                 x1_ref, hn3_ref, route_ref, route_t_ref, counts_ref, vlast_ref, base_ref, *, tiles_per_batch):
    tm = x_ref.shape[0]
    mm = win_ref.dtype
    dot = _dot if mm == BF16 else _dot_f32
    x = x_ref[...]
    h = _rms(x, mixg_ref[...]).astype(mm)
    z = _gelu_tanh(dot(h, win_ref[...]))
    u = z[:, :GMLP_WIDTH]
    vn = _rms(z[:, GMLP_WIDTH:], vng_ref[...])
    vb = vn.astype(mm)
    gated = []
    for c in range(tm // GMLP_CHUNK):
        rows = slice(c * GMLP_CHUNK, (c + 1) * GMLP_CHUNK)
        s = jnp.concatenate(
            [dot(ws_ref[g], vb[rows, g * GMLP_GROUP_DIM:(g + 1) * GMLP_GROUP_DIM])
             for g in range(GMLP_GROUPS)], axis=1) + bexp_ref[...]
        gated.append((u[rows] * s).astype(mm))
    y = dot(jnp.concatenate(gated, axis=0), wout_ref[...])
    x1 = x + y
    x1_ref[...] = x1

    @pl.when(pl.program_id(0) % tiles_per_batch == tiles_per_batch - 1)
    def _():
        vlast_ref[0] = vn[tm - GMLP_CHUNK:]

    _route_epilogue(x1, ffng_ref[...], wrhi_ref[...], wrlo_ref[...], base_ref,
                    hn3_ref, route_ref, route_t_ref, counts_ref)


def _gmlp_layer(x, n_batch, mix_g, w_in, vn_g, ws_eff, b_exp, w_out, ffn_g, wr_hi, wr_lo):
    t = x.shape[0]
    tm = min(MIX_TILE, t)
    tiles_per_batch = t // n_batch // tm
    gw = GMLP_WIDTH
    kern = functools.partial(_gmlp_kernel, tiles_per_batch=tiles_per_batch)
    return pl.pallas_call(
        kern,
        grid=(t // tm,),
        in_specs=[pl.BlockSpec((tm, D_MODEL), lambda i: (i, 0)),
                  _const_spec((1, D_MODEL)), _const_spec((D_MODEL, 2 * gw)), _const_spec((1, gw)),
                  _const_spec((GMLP_GROUPS, GMLP_CHUNK, GMLP_CHUNK)), _const_spec((GMLP_CHUNK, gw)),
                  _const_spec((gw, D_MODEL))] + _route_in_specs(),
        out_specs=(pl.BlockSpec((tm, D_MODEL), lambda i: (i, 0)),) + _route_out_specs(tm)
                  + (pl.BlockSpec((1, GMLP_CHUNK, gw), lambda i: (i // tiles_per_batch, 0, 0)),),
        out_shape=(jax.ShapeDtypeStruct((t, D_MODEL), F32),) + _route_out_shapes(t)
                  + (jax.ShapeDtypeStruct((n_batch, GMLP_CHUNK, gw), F32),),
        scratch_shapes=[pltpu.VMEM((1, ROUTE_LANES), F32)],
        compiler_params=_params(),
        name="gmlp_mixer",
    )(x, mix_g, w_in, vn_g, ws_eff, b_exp, w_out, ffn_g, wr_hi, wr_lo)


def _qk_norm_rope(t, group_ones, gain, cos, sin_lo, sin_hi):
    ms = _dot((t * t).astype(BF16), group_ones) * (1.0 / DIFF_HEAD_DIM)
    tn = t * lax.rsqrt(ms + EPS) * gain
    heads = []
    for h in range(DIFF_HEADS):
        th = tn[:, h * LANES:(h + 1) * LANES]
        heads.append(th * cos + pltpu.roll(th, LANES - ROT_DIM // 2, 1) * sin_lo
                     + pltpu.roll(th, ROT_DIM // 2, 1) * sin_hi)
    return jnp.concatenate(heads, axis=1)


def _qkv_kernel(x_ref, mixg_ref, win_ref, ones_ref, qg_ref, kg_ref, cos_ref, slo_ref, shi_ref, *outs, prompt):
    tm = x_ref.shape[0]
    h = _rms(x_ref[...], mixg_ref[...]).astype(BF16)
    qkv = _dot(h, win_ref[...])
    rope = (cos_ref[...], slo_ref[...], shi_ref[...])
    q_scale = DIFF_HEAD_DIM ** -0.5 * (math.log2(math.e) if prompt else 1.0)
    q = _qk_norm_rope(qkv[:, :D_MODEL], ones_ref[...], qg_ref[...], *rope) * q_scale
    k = _qk_norm_rope(qkv[:, D_MODEL:2 * D_MODEL], ones_ref[...], kg_ref[...], *rope)
    v = qkv[:, 2 * D_MODEL:]
    if not prompt:
        q_ref, kout_ref, vout_ref = outs
        q_ref[...] = q
        kout_ref[...] = k
        vout_ref[...] = v
        return
    kout_ref, vout_ref, kb_ref, vt_ref, qt_ref = outs
    kout_ref[...] = k
    vout_ref[...] = v
    for hd in range(DIFF_HEADS):
        lanes = slice(hd * LANES, (hd + 1) * LANES)
        kb_ref[0, hd] = k[:, lanes].astype(BF16)
        for c in range(tm // ATT_TILE):
            rows = slice(c * ATT_TILE, (c + 1) * ATT_TILE)
            vt_ref[0, hd, c] = v[rows, lanes].T.astype(BF16)
            qt_ref[0, hd, c] = q[rows, lanes].T.astype(BF16)


def _qkv_layer(x, n_batch, mix_g, w_in, group_ones, q_g, k_g, cos, sin_lo, sin_hi, prompt):
    t = x.shape[0]
    tm = min(MIX_TILE, t)
    seq = t // n_batch
    tpb = seq // tm if prompt else 1
    n_tab = cos.shape[0] // tm
    tab_spec = pl.BlockSpec((tm, LANES), lambda i: (i % n_tab, 0))
    row_spec = pl.BlockSpec((tm, D_MODEL), lambda i: (i, 0))
    in_specs = [row_spec, _const_spec((1, D_MODEL)), _const_spec((D_MODEL, 3 * D_MODEL)),
                _const_spec((D_MODEL, D_MODEL)), _const_spec((1, D_MODEL)), _const_spec((1, D_MODEL)),
                tab_spec, tab_spec, tab_spec]
    row_shape = jax.ShapeDtypeStruct((t, D_MODEL), F32)
    if prompt:
        nq = seq // ATT_TILE
        cpt = tm // ATT_TILE
        out_specs = (row_spec, row_spec,
                     pl.BlockSpec((1, DIFF_HEADS, tm, LANES), lambda i: (i // tpb, 0, i % tpb, 0)),
                     pl.BlockSpec((1, DIFF_HEADS, cpt, LANES, ATT_TILE), lambda i: (i // tpb, 0, i % tpb, 0, 0)),
                     pl.BlockSpec((1, DIFF_HEADS, cpt, LANES, ATT_TILE), lambda i: (i // tpb, 0, i % tpb, 0, 0)))
        out_shape = (row_shape, row_shape,
                     jax.ShapeDtypeStruct((n_batch, DIFF_HEADS, seq, LANES), BF16),
                     jax.ShapeDtypeStruct((n_batch, DIFF_HEADS, nq, LANES, ATT_TILE), BF16),
                     jax.ShapeDtypeStruct((n_batch, DIFF_HEADS, nq, LANES, ATT_TILE), BF16))
    else:
        out_specs = (row_spec, row_spec, row_spec)
        out_shape = (row_shape, row_shape, row_shape)
    return pl.pallas_call(
        functools.partial(_qkv_kernel, prompt=prompt),
        grid=(t // tm,), in_specs=in_specs, out_specs=out_specs, out_shape=out_shape,
        compiler_params=_params(), name="qkv_prompt" if prompt else "qkv_sample",
    )(x, mix_g, w_in, group_ones, q_g, k_g, cos, sin_lo, sin_hi)


def _flash_kernel(lam_ref, qt_ref, k_ref, vt_ref, sg_ref, o_ref, s_scr, p_scr, acc_scr, *, lam_init):
    tq = ATT_TILE
    n_heads = qt_ref.shape[1]
    i = pl.program_id(2)
    frow = lax.broadcasted_iota(jnp.int32, (LANES, tq), 0)
    qbd = []
    for g in range(n_heads):
        qt = qt_ref[0, g, 0]
        zero = jnp.zeros_like(qt)
        qbd.append(jnp.concatenate([jnp.where(frow < DIFF_HEAD_DIM, qt, zero),
                                    jnp.where(frow >= DIFF_HEAD_DIM, qt, zero)], axis=1))

    def scores(g, j):
        return _dot(k_ref[0, g, pl.ds(pl.multiple_of(j * tq, tq), tq), :], qbd[g])

    def softmax(s, m, l):
        m_new = jnp.maximum(m, jnp.max(s, axis=0, keepdims=True))
        p = jnp.exp2(s - m_new)
        alpha = jnp.exp2(m - m_new)
        return m_new, alpha * l + jnp.sum(p, axis=0, keepdims=True), alpha, p.astype(BF16)

    for g in range(n_heads):
        s_scr[g, 0] = scores(g, 0)
        p_scr[g, 1] = jnp.zeros(p_scr.shape[2:], p_scr.dtype)
        acc_scr[g] = jnp.zeros(acc_scr.shape[1:], acc_scr.dtype)

    def stage(g, j, cur, oth, m, l):
        pv = _dot(vt_ref[0, g, jnp.maximum(j - 1, 0)], p_scr[g, oth])
        m, l, alpha, p = softmax(s_scr[g, cur], m, l)
        p_scr[g, cur] = p
        acc_scr[g] = alpha * (acc_scr[g] + pv)
        s_scr[g, oth] = scores(g, j + 1)
        return m, l

    def stage_pair(jj, carry):
        out = []
        for g in range(n_heads):
            m, l = stage(g, 2 * jj, 0, 1, *carry[g])
            out.append(stage(g, 2 * jj + 1, 1, 0, m, l))
        return tuple(out)

    init = ((jnp.full((1, 2 * tq), NEG_BIG, F32), jnp.zeros((1, 2 * tq), F32)),) * n_heads
    carry = lax.fori_loop(0, i // 2, stage_pair, init)

    r = 2 * (i // 2)
    odd = i > r
    kchunk = lax.broadcasted_iota(jnp.int32, (tq, 2 * tq), 0) // CHUNK
    qchunk = (lax.broadcasted_iota(jnp.int32, (tq, 2 * tq), 1) % tq) // CHUNK
    causal = kchunk <= qchunk
    lam = lam_ref[0]
    for g in range(n_heads):
        m, l = carry[g]
        pv = _dot(vt_ref[0, g, jnp.maximum(r - 1, 0)], p_scr[g, 1])
        s_r = s_scr[g, 0]
        m, l, alpha, p_r = softmax(jnp.where(causal, s_r, jnp.where(odd, s_r, NEG_BIG)), m, l)
        acc = alpha * (acc_scr[g] + pv)
        s_i = scores(g, i)
        pv = _dot(vt_ref[0, g, r], p_r)
        m, l, alpha, p_i = softmax(jnp.where(causal, jnp.where(odd, s_i, NEG_BIG), NEG_BIG), m, l)
        acc = alpha * (acc + pv) + _dot(vt_ref[0, g, i], p_i)
        o = acc[:, :tq] / l[:, :tq] - lam * (acc[:, tq:] / l[:, tq:])
        on = o * lax.rsqrt(jnp.mean(o * o, axis=0, keepdims=True) + EPS) * sg_ref[...] * (1.0 - lam_init)
        o_ref[0, :, g * LANES:(g + 1) * LANES] = on.T.astype(o_ref.dtype)


def _flash_attention(lam, qt, kb, vt, sub_g_col, lam_init):
    n_batch, _, nq, _, tq = qt.shape
    seq = nq * tq
    hg = FLASH_HEADS_PER_STEP
    return pl.pallas_call(
        functools.partial(_flash_kernel, lam_init=lam_init),
        grid=(n_batch, DIFF_HEADS // hg, nq),
        in_specs=[pl.BlockSpec(memory_space=pltpu.SMEM),
                  pl.BlockSpec((1, hg, 1, LANES, tq), lambda b, h, i: (b, h, i, 0, 0)),
                  pl.BlockSpec((1, hg, seq, LANES), lambda b, h, i: (b, h, 0, 0)),
                  pl.BlockSpec((1, hg, nq, LANES, tq), lambda b, h, i: (b, h, 0, 0, 0)),
                  pl.BlockSpec((LANES, 1), lambda b, h, i: (0, 0))],
        out_specs=pl.BlockSpec((1, tq, hg * LANES), lambda b, h, i: (b, i, h)),
        out_shape=jax.ShapeDtypeStruct((n_batch, seq, D_MODEL), BF16),
        scratch_shapes=[pltpu.VMEM((hg, 2, tq, 2 * tq), F32),
                        pltpu.VMEM((hg, 2, tq, 2 * tq), BF16),
                        pltpu.VMEM((hg, LANES, 2 * tq), F32)],
        compiler_params=_params(3), name="diff_flash",
    )(lam, qt, kb, vt, sub_g_col)


def _sample_attn_kernel(lam_ref, q_ref, kn_ref, vn_ref, ck_ref, cv_ref, sg_ref, o_ref, *, lam_init, past):
    q = q_ref[0]
    kn = kn_ref[0]
    vn = vn_ref[0]
    lam = lam_ref[0]
    lane = lax.broadcasted_iota(jnp.int32, (q.shape[0], LANES), 1)
    contract_last = (((1,), (1,)), ((), ()))
    outs = []
    for h in range(DIFF_HEADS):
        lanes = slice(h * LANES, (h + 1) * LANES)
        kc = ck_ref[0, pl.ds(h, past, stride=DIFF_HEADS), :].astype(BF16)
        vc = cv_ref[0, pl.ds(h, past, stride=DIFF_HEADS), :].astype(BF16)
        qh = q[:, lanes]
        knh = kn[:, lanes].astype(BF16)
        vnh = vn[:, lanes].astype(BF16)
        sub = []
        for c in range(2):
            keep = (lane < DIFF_HEAD_DIM) if c == 0 else (lane >= DIFF_HEAD_DIM)
            qc = jnp.where(keep, qh, 0.0).astype(BF16)
            s_old = lax.dot_general(qc, kc, contract_last, preferred_element_type=F32)
            s_new = lax.dot_general(qc, knh, contract_last, preferred_element_type=F32)
            m = jnp.maximum(jnp.max(s_old, axis=1, keepdims=True), jnp.max(s_new, axis=1, keepdims=True))
            p_old = jnp.exp(s_old - m)
            p_new = jnp.exp(s_new - m)
            l = jnp.sum(p_old, axis=1, keepdims=True) + jnp.sum(p_new, axis=1, keepdims=True)
            sub.append((_dot(p_old.astype(BF16), vc) + _dot(p_new.astype(BF16), vnh)) / l)
        o = sub[0] - lam * sub[1]
        outs.append(_rms(o, sg_ref[...]) * (1.0 - lam_init))
    o_ref[0] = jnp.concatenate(outs, axis=1).astype(o_ref.dtype)


def _sample_attention(lam, q, kn, vn, cache_k, cache_v, sub_g_row, lam_init):
    nb, rows, _ = q.shape
    past = cache_k.shape[1] // DIFF_HEADS
    new_spec = pl.BlockSpec((1, rows, D_MODEL), lambda b: (b, 0, 0))
    cache_spec = pl.BlockSpec((1, past * DIFF_HEADS, LANES), lambda b: (b, 0, 0))
    return pl.pallas_call(
        functools.partial(_sample_attn_kernel, lam_init=lam_init, past=past),
        grid=(nb,),
        in_specs=[pl.BlockSpec(memory_space=pltpu.SMEM), new_spec, new_spec, new_spec, cache_spec, cache_spec,
                  _const_spec((1, LANES))],
        out_specs=new_spec, out_shape=jax.ShapeDtypeStruct((nb, rows, D_MODEL), BF16),
        compiler_params=_params(), name="sample_attn",
    )(lam, q, kn, vn, cache_k, cache_v, sub_g_row)


def _attn_out_kernel(a_ref, x_ref, wout_ref, ffng_ref, wrhi_ref, wrlo_ref,
                     x1_ref, hn3_ref, route_ref, route_t_ref, counts_ref, base_ref):
    x1 = x_ref[...] + _dot(a_ref[...], wout_ref[...])
    x1_ref[...] = x1
    _route_epilogue(x1, ffng_ref[...], wrhi_ref[...], wrlo_ref[...], base_ref,
                    hn3_ref, route_ref, route_t_ref, counts_ref)


def _attn_out_layer(a, x, w_out, ffn_g, wr_hi, wr_lo):
    t = x.shape[0]
    tm = min(MIX_TILE, t)
    row_spec = pl.BlockSpec((tm, D_MODEL), lambda i: (i, 0))
    return pl.pallas_call(
        _attn_out_kernel,
        grid=(t // tm,),
        in_specs=[row_spec, row_spec, _const_spec((D_MODEL, D_MODEL))] + _route_in_specs(),
        out_specs=(row_spec,) + _route_out_specs(tm),
        out_shape=(jax.ShapeDtypeStruct((t, D_MODEL), F32),) + _route_out_shapes(t),
        scratch_shapes=[pltpu.VMEM((1, ROUTE_LANES), F32)],
        compiler_params=_params(), name="attn_out",
    )(a, x, w_out, ffn_g, wr_hi, wr_lo)


def _load_tile_indices(dest_hbm, idx_smem, idx_sem):
    n = idx_smem.shape[0]
    start = pl.multiple_of(pl.program_id(0) * n, n)
    idx_copy = pltpu.make_async_copy(dest_hbm.at[pl.ds(start, n)], idx_smem, idx_sem)
    idx_copy.start()
    idx_copy.wait()


def _index_tile_len(n_indices):
    return -(-n_indices // INDEX_SLICE_WORDS) * INDEX_SLICE_WORDS


def _per_step_dest(dest, tokens_per_step):
    steps = dest.shape[1] // tokens_per_step
    return dest.reshape(2, steps, tokens_per_step).transpose(1, 0, 2).reshape(steps, 2 * tokens_per_step)


def _tiled_indices(per_step):
    n = per_step.shape[1]
    return jnp.pad(per_step, ((0, 0), (0, _index_tile_len(n) - n))).reshape(-1)


def _dispatch_kernel(idx_hbm, hn_ref, xs3_hbm, idx_smem, zero_buf, idx_sem, row_sem, *, td, n_fill):
    _load_tile_indices(idx_hbm, idx_smem, idx_sem)
    zero_buf[...] = jnp.zeros_like(zero_buf)

    def scatter(t, carry):
        src = hn_ref.at[pl.ds(pl.multiple_of(t * ROW_TILES, ROW_TILES), ROW_TILES)]
        pltpu.make_async_copy(src, xs3_hbm.at[idx_smem[t]], row_sem).start(priority=0)
        pltpu.make_async_copy(src, xs3_hbm.at[idx_smem[td + t]], row_sem).start(priority=1)
        return carry

    def fill(p, carry):
        pltpu.make_async_copy(zero_buf, xs3_hbm.at[idx_smem[2 * td + p]], row_sem).start()
        return carry

    lax.fori_loop(0, td, scatter, 0)
    lax.fori_loop(0, n_fill, fill, 0)
    done = xs3_hbm.at[pl.ds(0, 2 * td + n_fill)]
    pltpu.make_async_copy(done, done, row_sem).wait()


def _dispatch(dest, pad_rows, hn2, n_rows):
    t = hn2.shape[0] // ROW_TILES
    td = min(ROW_DMA_TILE, t)
    steps = t // td
    n_fill = pad_rows.shape[0] // steps
    table = jnp.concatenate([_per_step_dest(dest, td), pad_rows.reshape(steps, n_fill)], axis=1)
    return pl.pallas_call(
        functools.partial(_dispatch_kernel, td=td, n_fill=n_fill),
        grid=(steps,),
        in_specs=[pl.BlockSpec(memory_space=pl.ANY),
                  pl.BlockSpec((td * ROW_TILES, LANES), lambda i: (i, 0))],
        out_specs=pl.BlockSpec(memory_space=pl.ANY),
        out_shape=jax.ShapeDtypeStruct((n_rows, ROW_TILES, LANES), F32),
        scratch_shapes=[pltpu.SMEM((_index_tile_len(2 * td + n_fill),), jnp.int32),
                        pltpu.VMEM((ROW_TILES, LANES), F32),
                        pltpu.SemaphoreType.DMA, pltpu.SemaphoreType.DMA],
        compiler_params=_params(), name="moe_dispatch",
    )(_tiled_indices(table), hn2)


def _expert_kernel(bexp_ref, nvalid_ref, xs_ref, wg_ref, wu_ref, wd_ref, yb_ref, wg_bf, wu_bf, wd_bf):
    blk = xs_ref.shape[0] // ROW_TILES
    i = pl.program_id(0)
    nvalid = nvalid_ref[i]

    @pl.when(jnp.logical_or(i == 0, bexp_ref[i] != bexp_ref[jnp.maximum(i - 1, 0)]))
    def _():
        wg_bf[...] = wg_ref[0, 0].astype(BF16)
        wu_bf[...] = wu_ref[0, 0].astype(BF16)
        wd_bf[...] = wd_ref[0, 0].astype(BF16)

    @pl.when(nvalid > 0)
    def _():
        x = jnp.concatenate([xs_ref[pl.ds(s, blk, stride=ROW_TILES), :] for s in range(ROW_TILES)], axis=1)
        row = lax.broadcasted_iota(jnp.int32, (blk, 1), 0)
        xb = jnp.where(row < nvalid, x, 0.0).astype(BF16)
        hg = _dot(xb, wg_bf[...])
        hu = _dot(xb, wu_bf[...])
        act = (hg * (1.0 / (1.0 + jnp.exp(-hg))) * hu).astype(BF16)
        y = _dot(act, wd_bf[...])
        for s in range(ROW_TILES):
            yb_ref[pl.ds(s, blk, stride=ROW_TILES), :] = y[:, s * LANES:(s + 1) * LANES]

    @pl.when(nvalid <= 0)
    def _():
        yb_ref[...] = jnp.zeros_like(yb_ref)


def _experts(block_expert, block_nvalid, xs2, layer, wg, wu, wd):
    n_blocks = block_expert.shape[0]
    blk = xs2.shape[0] // ROW_TILES // n_blocks
    rows_spec = pl.BlockSpec((blk * ROW_TILES, LANES), lambda i, be, nv: (i, 0))
    up_spec = pl.BlockSpec((1, 1, D_MODEL, EXPERT_HIDDEN), lambda i, be, nv: (layer, be[i], 0, 0))
    down_spec = pl.BlockSpec((1, 1, EXPERT_HIDDEN, D_MODEL), lambda i, be, nv: (layer, be[i], 0, 0))
    return pl.pallas_call(
        _expert_kernel,
        grid_spec=pltpu.PrefetchScalarGridSpec(
            num_scalar_prefetch=2, grid=(n_blocks,),
            in_specs=[rows_spec, up_spec, up_spec, down_spec],
            out_specs=rows_spec,
            scratch_shapes=[pltpu.VMEM((D_MODEL, EXPERT_HIDDEN), BF16), pltpu.VMEM((D_MODEL, EXPERT_HIDDEN), BF16),
                            pltpu.VMEM((EXPERT_HIDDEN, D_MODEL), BF16)]),
        out_shape=jax.ShapeDtypeStruct(xs2.shape, F32),
        compiler_params=_params(), name="moe_experts",
    )(block_expert, block_nvalid, xs2, wg, wu, wd)


def _combine_kernel(dest_hbm, x1_ref, route_ref, yb3_hbm, out_ref, idx_smem, buf0, buf1, idx_sem, row_sem):
    tc = x1_ref.shape[0]
    _load_tile_indices(dest_hbm, idx_smem, idx_sem)

    def body(t, carry):
        dst = pl.ds(pl.multiple_of(t * ROW_TILES, ROW_TILES), ROW_TILES)
        pltpu.make_async_copy(yb3_hbm.at[idx_smem[t]], buf0.at[dst], row_sem).start(priority=0)
        pltpu.make_async_copy(yb3_hbm.at[idx_smem[tc + t]], buf1.at[dst], row_sem).start(priority=1)
        return carry

    lax.fori_loop(0, tc, body, 0)
    pltpu.make_async_copy(buf0, buf0, row_sem).wait()
    pltpu.make_async_copy(buf1, buf1, row_sem).wait()
    g1 = route_ref[:, 2:3]
    g2 = route_ref[:, 3:4]
    for s in range(ROW_TILES):
        lanes = slice(s * LANES, (s + 1) * LANES)
        y1 = buf0[pl.ds(s, tc, stride=ROW_TILES), :]
        y2 = buf1[pl.ds(s, tc, stride=ROW_TILES), :]
        out_ref[:, lanes] = x1_ref[:, lanes] + (g1 * y1 + g2 * y2)


def _combine(dest, x1, route, yb3):
    t = x1.shape[0]
    tc = min(ROW_DMA_TILE, t)
    dest_flat = _tiled_indices(_per_step_dest(dest, tc))
    return pl.pallas_call(
        _combine_kernel,
        grid=(t // tc,),
        in_specs=[pl.BlockSpec(memory_space=pl.ANY),
                  pl.BlockSpec((tc, D_MODEL), lambda i: (i, 0)),
                  pl.BlockSpec((tc, ROUTE_COLS), lambda i: (i, 0)),
                  pl.BlockSpec(memory_space=pl.ANY)],
        out_specs=pl.BlockSpec((tc, D_MODEL), lambda i: (i, 0)),
        out_shape=jax.ShapeDtypeStruct((t, D_MODEL), F32),
        scratch_shapes=[pltpu.SMEM((_index_tile_len(2 * tc),), jnp.int32),
                        pltpu.VMEM((tc * ROW_TILES, LANES), F32), pltpu.VMEM((tc * ROW_TILES, LANES), F32),
                        pltpu.SemaphoreType.DMA, pltpu.SemaphoreType.DMA],
        compiler_params=_params(), name="moe_combine",
    )(dest_flat, x1, route, yb3)


def _count_le(sorted_ends, values):
    return jnp.sum((sorted_ends[None, :] <= values[:, None]).astype(jnp.int32), axis=1)


def _moe_block_rows(t):
    mean_rows_per_expert = 2 * t // N_EXPERTS
    return min(MOE_BLOCK_ROWS, max(BF16_TILE_ROWS, mean_rows_per_expert))


def _moe(x1, hn3, route, route_t, counts, layer, wg, wu, wd):
    t = x1.shape[0]
    blk = _moe_block_rows(t)
    expert = route_t[0:2].astype(jnp.int32)
    rank = route_t[4:6].astype(jnp.int32)
    cnt = counts[0, :N_EXPERTS].astype(jnp.int32)
    padded = (cnt + blk - 1) // blk * blk
    pad_end = jnp.cumsum(padded)
    pad_start = pad_end - padded
    dest = pad_start[expert] + rank
    n_blocks = -(-2 * t // blk) + N_EXPERTS
    blk_start = jnp.arange(n_blocks, dtype=jnp.int32) * blk
    block_expert = jnp.minimum(_count_le(pad_end, blk_start), N_EXPERTS - 1)
    block_nvalid = jnp.clip(pad_start[block_expert] + cnt[block_expert] - blk_start, 0, blk).astype(jnp.int32)
    n_rows = n_blocks * blk
    gap_start = jnp.concatenate([pad_start + cnt, pad_end[-1:]])
    gap_len = jnp.concatenate([padded - cnt, n_rows - pad_end[-1:]])
    gap_end = jnp.cumsum(gap_len)
    p = jnp.arange(n_rows - 2 * t, dtype=jnp.int32)
    seg = _count_le(gap_end, p)
    pad_rows = (gap_start[seg] + p - (gap_end[seg] - gap_len[seg])).astype(jnp.int32)
    xs3 = _dispatch(dest, pad_rows, hn3, n_rows)
    yb2 = _experts(block_expert, block_nvalid, xs3.reshape(n_rows * ROW_TILES, LANES), layer, wg, wu, wd)
    return _combine(dest, x1, route, yb2.reshape(n_rows, ROW_TILES, LANES))


def _router_weights(w_group, w_router):
    w = jnp.concatenate([w_router, w_group,
                         jnp.zeros((D_MODEL, ROUTE_LANES - N_EXPERTS - MOE_GROUPS), F32)], axis=1)
    hi = w.astype(BF16)
    return hi, (w - hi.astype(F32)).astype(BF16)


def _rope_tables(pos):
    half = ROT_DIM // 2
    inv_freq = jnp.power(ROPE_THETA, -jnp.arange(half, dtype=F32) * (2.0 / ROT_DIM))
    ang = pos.astype(F32)[:, None] * inv_freq[None, :]
    cos, sin = jnp.cos(ang), jnp.sin(ang)
    n = pos.shape[0]
    ones = jnp.ones((n, DIFF_HEAD_DIM - ROT_DIM), F32)
    zeros = jnp.zeros((n, DIFF_HEAD_DIM - ROT_DIM), F32)
    zh = jnp.zeros((n, half), F32)
    sub_cos = jnp.concatenate([cos, cos, ones], axis=1)
    sub_lo = jnp.concatenate([-sin, zh, zeros], axis=1)
    sub_hi = jnp.concatenate([zh, sin, zeros], axis=1)
    return tuple(jnp.concatenate([a, a], axis=1) for a in (sub_cos, sub_lo, sub_hi))


def _spatial_weights(w_s, b_s, lc):
    pos = jnp.arange(lc)
    mask = (pos[None, :] // CHUNK) <= (pos[:, None] // CHUNK)
    ws = jnp.where(mask[None], w_s[:, :lc, :lc], 0.0)
    reps = GMLP_CHUNK // lc
    eye = jnp.eye(reps, dtype=F32)
    ws_eff = jnp.einsum("ab,gij->gaibj", eye, ws).reshape(GMLP_GROUPS, GMLP_CHUNK, GMLP_CHUNK)
    b_rows = jnp.tile(b_s[:, :lc], (1, reps))
    b_exp = jnp.repeat(b_rows.T, GMLP_GROUP_DIM, axis=1)
    return ws_eff, b_exp


def kernel(x_prompt, x_sample, cache_attn_k, cache_attn_v, mix_norm, ffn_norm, gmlp_w_in, gmlp_v_norm, gmlp_w_s, gmlp_b_s, gmlp_w_out, attn_w_in, attn_q_norm, attn_k_norm, attn_lam_q1, attn_lam_k1, attn_lam_q2, attn_lam_k2, attn_sub_norm, attn_w_out, moe_w_group, moe_w_router, moe_w_gate, moe_w_up, moe_w_down):
    nb_p, seq, _ = x_prompt.shape
    nb_s, dec, _ = x_sample.shape
    past = cache_attn_k.shape[2]
    assert cache_attn_k.shape[0] == DEPTH // 2 == 1 and mix_norm.shape[0] == DEPTH
    xp = x_prompt.reshape(nb_p * seq, D_MODEL)
    xs = x_sample.reshape(nb_s * dec, D_MODEL)
    row = lambda a: a.reshape(1, -1)

    router0 = _router_weights(moe_w_group[0], moe_w_router[0])
    gm_tail = (row(ffn_norm[0]),) + router0
    experts0 = (0, moe_w_gate, moe_w_up, moe_w_down)
    ws_p, b_p = _spatial_weights(gmlp_w_s[0], gmlp_b_s[0], GMLP_CHUNK)
    ws_s, b_s = _spatial_weights(gmlp_w_s[0], gmlp_b_s[0], dec)
    *mixed_p, gv_p = _gmlp_layer(
        xp, nb_p, row(mix_norm[0]), gmlp_w_in[0].astype(BF16), row(gmlp_v_norm[0]), ws_p.astype(BF16), b_p,
        gmlp_w_out[0].astype(BF16), *gm_tail)
    *mixed_s, gv_s = _gmlp_layer(
        xs, 1, row(mix_norm[0]), gmlp_w_in[0], row(gmlp_v_norm[0]), ws_s, b_s, gmlp_w_out[0], *gm_tail)
    xp = _moe(*mixed_p, *experts0)
    xs = _moe(*mixed_s, *experts0)

    lam_init = 0.8 - 0.6 * math.exp(-0.3 * 1)
    lam = (jnp.exp(jnp.sum(attn_lam_q1[0] * attn_lam_k1[0])) - jnp.exp(jnp.sum(attn_lam_q2[0] * attn_lam_k2[0]))
           + lam_init).reshape(1).astype(F32)
    grp = jnp.arange(D_MODEL) // DIFF_HEAD_DIM
    group_ones = (grp[:, None] == grp[None, :]).astype(BF16)
    qk = (row(mix_norm[1]), attn_w_in[0].astype(BF16), group_ones,
          row(jnp.tile(attn_q_norm[0], 2 * DIFF_HEADS)), row(jnp.tile(attn_k_norm[0], 2 * DIFF_HEADS)))
    router1 = _router_weights(moe_w_group[1], moe_w_router[1])
    at_tail = (attn_w_out[0].astype(BF16), row(ffn_norm[1])) + router1
    experts1 = (1, moe_w_gate, moe_w_up, moe_w_down)

    kp, vp, kb, vt, qt = _qkv_layer(xp, nb_p, *qk, *_rope_tables(jnp.arange(seq)), prompt=True)
    ap = _flash_attention(lam, qt, kb, vt, attn_sub_norm[0].reshape(LANES, 1), lam_init)
    xp = _moe(*_attn_out_layer(ap.reshape(nb_p * seq, D_MODEL), xp, *at_tail), *experts1)

    pos_s = jnp.tile(past + jnp.arange(dec), nb_s)
    qs, ks, vs = _qkv_layer(xs, nb_s, *qk, *_rope_tables(pos_s), prompt=False)
    shp = (nb_s, dec, D_MODEL)
    a_s = _sample_attention(lam, qs.reshape(shp), ks.reshape(shp), vs.reshape(shp),
                            cache_attn_k.reshape(nb_s, past * DIFF_HEADS, LANES),
                            cache_attn_v.reshape(nb_s, past * DIFF_HEADS, LANES),
                            row(attn_sub_norm[0]), lam_init)
    xs = _moe(*_attn_out_layer(a_s.reshape(nb_s * dec, D_MODEL), xs, *at_tail), *experts1)

    hv = (DIFF_HEADS, DIFF_VALUE_DIM)
    return (xp.reshape(nb_p, seq, D_MODEL), xs.reshape(nb_s, dec, D_MODEL),
            gv_p[None], gv_s.reshape(1, nb_s, dec, GMLP_WIDTH),
            kp.reshape(1, nb_p, seq, *hv), vp.reshape(1, nb_p, seq, *hv),
            ks.reshape(1, nb_s, dec, *hv), vs.reshape(1, nb_s, dec, *hv))
```

```python
import functools
import math

import jax
import jax.numpy as jnp
from jax import lax
from jax.experimental import pallas as pl
from jax.experimental.pallas import tpu as pltpu

D_MODEL = 1024
DEPTH = 2
CHUNK = 64
GMLP_CHUNK = 128
GMLP_WIDTH = 2 * D_MODEL
GMLP_GROUPS = 8
GMLP_GROUP_DIM = GMLP_WIDTH // GMLP_GROUPS
DIFF_HEADS = 8
DIFF_HEAD_DIM = D_MODEL // (2 * DIFF_HEADS)
DIFF_VALUE_DIM = 2 * DIFF_HEAD_DIM
ROT_DIM = DIFF_HEAD_DIM // 4
ROPE_THETA = 500000.0
MOE_GROUPS = 4
MOE_EXPERTS_PER_GROUP = 8
N_EXPERTS = MOE_GROUPS * MOE_EXPERTS_PER_GROUP
EXPERT_HIDDEN = D_MODEL // 2
EPS = 1e-6

LANES = 128
SUBLANES = 8
BF16_TILE_ROWS = 2 * SUBLANES
ROW_TILES = D_MODEL // LANES
VMEM_LIMIT_BYTES = 56 * 1024 * 1024

ROUTE_LANES = LANES
GROUP_LANE0 = N_EXPERTS
ROUTE_COLS = 8
NEG_BIG = -1e30

MIX_TILE = 512
OUT_PROJ_TILE = 256
ATT_TILE = 256
FLASH_HEADS_PER_STEP = 4
MOE_BLOCK_ROWS = 256
ROW_DMA_TILE = 512
INDEX_SLICE_WORDS = 1024

F32 = jnp.float32
BF16 = jnp.bfloat16


def _params(n_axes=1):
    return pltpu.CompilerParams(dimension_semantics=("arbitrary",) * n_axes,
                                vmem_limit_bytes=VMEM_LIMIT_BYTES)


def _rms(x, g):
    return x * lax.rsqrt(jnp.mean(x * x, axis=-1, keepdims=True) + EPS) * g


def _dot(a, b):
    return jnp.dot(a, b, preferred_element_type=F32)


def _dot_f32(a, b):
    return jnp.dot(a, b, preferred_element_type=F32, precision=lax.Precision.HIGHEST)


def _const_spec(shape):
    return pl.BlockSpec(shape, lambda *_: (0,) * len(shape), pipeline_mode=pl.Buffered(1))


def _route_epilogue(x1, ffn_g, wr_hi, wr_lo, base_ref, hn3_ref, route_ref, route_t_ref, counts_ref):
    tm = x1.shape[0]

    @pl.when(pl.program_id(0) == 0)
    def _():
        base_ref[...] = jnp.zeros_like(base_ref)

    hn = _rms(x1, ffn_g)
    for s in range(ROW_TILES):
        hn3_ref[pl.ds(s, tm, stride=ROW_TILES), :] = hn[:, s * LANES:(s + 1) * LANES]

    h_hi = hn.astype(BF16)
    h_lo = (hn - h_hi.astype(F32)).astype(BF16)
    both = _dot(h_hi, jnp.concatenate([wr_hi, wr_lo], axis=1))
    logit = both[:, :ROUTE_LANES] + both[:, ROUTE_LANES:] + _dot(h_lo, wr_hi)

    lane = lax.broadcasted_iota(jnp.int32, (tm, ROUTE_LANES), 1)
    far = jnp.int32(4 * ROUTE_LANES)
    lg = jnp.where(lane >= GROUP_LANE0, jnp.where(lane < GROUP_LANE0 + MOE_GROUPS, logit, NEG_BIG), NEG_BIG)
    mg = jnp.max(lg, axis=1, keepdims=True)
    g_lane = jnp.min(jnp.where(lg == mg, lane, far), axis=1, keepdims=True)
    g_sel = g_lane - GROUP_LANE0
    p_g = 1.0 / jnp.sum(jnp.exp(lg - mg), axis=1, keepdims=True)

    lo_lane = g_sel * MOE_EXPERTS_PER_GROUP
    le = jnp.where(lane >= lo_lane, jnp.where(lane < lo_lane + MOE_EXPERTS_PER_GROUP, logit, NEG_BIG), NEG_BIG)
    m1 = jnp.max(le, axis=1, keepdims=True)
    j1 = jnp.min(jnp.where(le == m1, lane, far), axis=1, keepdims=True)
    le2 = jnp.where(lane == j1, NEG_BIG, le)
    m2 = jnp.max(le2, axis=1, keepdims=True)
    j2 = jnp.min(jnp.where(le2 == m2, lane, far), axis=1, keepdims=True)
    r = jnp.exp(m2 - m1)
    gate1 = p_g / (1.0 + r)
    gate2 = p_g * r / (1.0 + r)

    hit1 = lane == j1
    hit2 = lane == j2
    onehot = jnp.where(hit1, 1.0, jnp.where(hit2, 1.0, 0.0))
    row = lax.broadcasted_iota(jnp.int32, (tm, tm), 0)
    col = lax.broadcasted_iota(jnp.int32, (tm, tm), 1)
    earlier = jnp.where(row > col, 1.0, 0.0).astype(BF16)
    prefix = _dot(earlier, onehot.astype(BF16)) + base_ref[...]
    rank1 = jnp.sum(jnp.where(hit1, prefix, 0.0), axis=1, keepdims=True)
    rank2 = jnp.sum(jnp.where(hit2, prefix, 0.0), axis=1, keepdims=True)
    base_new = base_ref[...] + jnp.sum(onehot, axis=0, keepdims=True)
    base_ref[...] = base_new
    counts_ref[...] = base_new

    rec = jnp.where(lane == 0, j1.astype(F32),
          jnp.where(lane == 1, j2.astype(F32),
          jnp.where(lane == 2, gate1,
          jnp.where(lane == 3, gate2,
          jnp.where(lane == 4, rank1,
          jnp.where(lane == 5, rank2, 0.0))))))
    route_ref[...] = rec[:, :ROUTE_COLS]
    route_t_ref[...] = rec.T[:ROUTE_COLS]


def _route_out_shapes(t):
    return (jax.ShapeDtypeStruct((t * ROW_TILES, LANES), F32),
            jax.ShapeDtypeStruct((t, ROUTE_COLS), F32),
            jax.ShapeDtypeStruct((ROUTE_COLS, t), F32),
            jax.ShapeDtypeStruct((1, ROUTE_LANES), F32))


def _route_out_specs(tm):
    return (pl.BlockSpec((tm * ROW_TILES, LANES), lambda i: (i, 0)),
            pl.BlockSpec((tm, ROUTE_COLS), lambda i: (i, 0)),
            pl.BlockSpec((ROUTE_COLS, tm), lambda i: (0, i)),
            pl.BlockSpec((1, ROUTE_LANES), lambda i: (0, 0)))


def _route_in_specs():
    return [_const_spec((1, D_MODEL)), _const_spec((D_MODEL, ROUTE_LANES)), _const_spec((D_MODEL, ROUTE_LANES))]


def _gelu_tanh(x):
    cdf = 0.5 * (1.0 + jnp.tanh(math.sqrt(2.0 / math.pi) * (x + 0.044715 * (x * x * x))))
    return x * cdf


def _gmlp_kernel(x_ref, mixg_ref, win_ref, vng_ref, ws_ref, bexp_ref, wout_ref,
                 ffng_ref, wrhi_ref, wrlo_ref,
                 x1_ref, hn3_ref, route_ref, route_t_ref, counts_ref, vlast_ref, base_ref, *, tiles_per_batch):
    tm = x_ref.shape[0]
    mm = win_ref.dtype
    dot = _dot if mm == BF16 else _dot_f32
    x = x_ref[...]
    h = _rms(x, mixg_ref[...]).astype(mm)
    z = _gelu_tanh(dot(h, win_ref[...]))
    u = z[:, :GMLP_WIDTH]
    vn = _rms(z[:, GMLP_WIDTH:], vng_ref[...])
    vb = vn.astype(mm)
    gated = []
    for c in range(tm // GMLP_CHUNK):
        rows = slice(c * GMLP_CHUNK, (c + 1) * GMLP_CHUNK)
        s = jnp.concatenate(
            [dot(ws_ref[g], vb[rows, g * GMLP_GROUP_DIM:(g + 1) * GMLP_GROUP_DIM])
             for g in range(GMLP_GROUPS)], axis=1) + bexp_ref[...]
        gated.append((u[rows] * s).astype(mm))
    y = dot(jnp.concatenate(gated, axis=0), wout_ref[...])
    x1 = x + y
    x1_ref[...] = x1

    @pl.when(pl.program_id(0) % tiles_per_batch == tiles_per_batch - 1)
    def _():
        vlast_ref[0] = vn[tm - GMLP_CHUNK:]

    _route_epilogue(x1, ffng_ref[...], wrhi_ref[...], wrlo_ref[...], base_ref,
                    hn3_ref, route_ref, route_t_ref, counts_ref)


def _gmlp_layer(x, n_batch, mix_g, w_in, vn_g, ws_eff, b_exp, w_out, ffn_g, wr_hi, wr_lo):
    t = x.shape[0]
    tm = min(MIX_TILE, t)
    tiles_per_batch = t // n_batch // tm
    gw = GMLP_WIDTH
    kern = functools.partial(_gmlp_kernel, tiles_per_batch=tiles_per_batch)
    return pl.pallas_call(
        kern,
        grid=(t // tm,),
        in_specs=[pl.BlockSpec((tm, D_MODEL), lambda i: (i, 0)),
                  _const_spec((1, D_MODEL)), _const_spec((D_MODEL, 2 * gw)), _const_spec((1, gw)),
                  _const_spec((GMLP_GROUPS, GMLP_CHUNK, GMLP_CHUNK)), _const_spec((GMLP_CHUNK, gw)),
                  _const_spec((gw, D_MODEL))] + _route_in_specs(),
        out_specs=(pl.BlockSpec((tm, D_MODEL), lambda i: (i, 0)),) + _route_out_specs(tm)
                  + (pl.BlockSpec((1, GMLP_CHUNK, gw), lambda i: (i // tiles_per_batch, 0, 0)),),
        out_shape=(jax.ShapeDtypeStruct((t, D_MODEL), F32),) + _route_out_shapes(t)
                  + (jax.ShapeDtypeStruct((n_batch, GMLP_CHUNK, gw), F32),),
        scratch_shapes=[pltpu.VMEM((1, ROUTE_LANES), F32)],
        compiler_params=_params(),
        name="gmlp_mixer",
    )(x, mix_g, w_in, vn_g, ws_eff, b_exp, w_out, ffn_g, wr_hi, wr_lo)


def _qk_norm_rope(t, group_ones, gain, cos, sin_lo, sin_hi):
    ms = _dot((t * t).astype(BF16), group_ones) * (1.0 / DIFF_HEAD_DIM)
    tn = t * lax.rsqrt(ms + EPS) * gain
    heads = []
    for h in range(DIFF_HEADS):
        th = tn[:, h * LANES:(h + 1) * LANES]
        heads.append(th * cos + pltpu.roll(th, LANES - ROT_DIM // 2, 1) * sin_lo
                     + pltpu.roll(th, ROT_DIM // 2, 1) * sin_hi)
    return jnp.concatenate(heads, axis=1)


def _qkv_kernel(x_ref, mixg_ref, win_ref, ones_ref, qg_ref, kg_ref, cos_ref, slo_ref, shi_ref, *outs, prompt):
    tm = x_ref.shape[0]
    h = _rms(x_ref[...], mixg_ref[...]).astype(BF16)
    qkv = _dot(h, win_ref[...])
    rope = (cos_ref[...], slo_ref[...], shi_ref[...])
    q_scale = DIFF_HEAD_DIM ** -0.5 * (math.log2(math.e) if prompt else 1.0)
    q = _qk_norm_rope(qkv[:, :D_MODEL], ones_ref[...], qg_ref[...], *rope) * q_scale
    k = _qk_norm_rope(qkv[:, D_MODEL:2 * D_MODEL], ones_ref[...], kg_ref[...], *rope)
    v = qkv[:, 2 * D_MODEL:]
    if not prompt:
        q_ref, kout_ref, vout_ref = outs
        q_ref[...] = q
        kout_ref[...] = k
        vout_ref[...] = v
        return
    kout_ref, vout_ref, kb_ref, vt_ref, qt_ref = outs
    kout_ref[...] = k
    vout_ref[...] = v
    for hd in range(DIFF_HEADS):
        lanes = slice(hd * LANES, (hd + 1) * LANES)
        kb_ref[0, hd] = k[:, lanes].astype(BF16)
        for c in range(tm // ATT_TILE):
            rows = slice(c * ATT_TILE, (c + 1) * ATT_TILE)
            vt_ref[0, hd, c] = v[rows, lanes].T.astype(BF16)
            qt_ref[0, hd, c] = q[rows, lanes].T.astype(BF16)


def _qkv_layer(x, n_batch, mix_g, w_in, group_ones, q_g, k_g, cos, sin_lo, sin_hi, prompt):
    t = x.shape[0]
    tm = min(MIX_TILE, t)
    seq = t // n_batch
    tpb = seq // tm if prompt else 1
    n_tab = cos.shape[0] // tm
    tab_spec = pl.BlockSpec((tm, LANES), lambda i: (i % n_tab, 0))
    row_spec = pl.BlockSpec((tm, D_MODEL), lambda i: (i, 0))
    in_specs = [row_spec, _const_spec((1, D_MODEL)), _const_spec((D_MODEL, 3 * D_MODEL)),
                _const_spec((D_MODEL, D_MODEL)), _const_spec((1, D_MODEL)), _const_spec((1, D_MODEL)),
                tab_spec, tab_spec, tab_spec]
    row_shape = jax.ShapeDtypeStruct((t, D_MODEL), F32)
    if prompt:
        nq = seq // ATT_TILE
        cpt = tm // ATT_TILE
        out_specs = (row_spec, row_spec,
                     pl.BlockSpec((1, DIFF_HEADS, tm, LANES), lambda i: (i // tpb, 0, i % tpb, 0)),
                     pl.BlockSpec((1, DIFF_HEADS, cpt, LANES, ATT_TILE), lambda i: (i // tpb, 0, i % tpb, 0, 0)),
                     pl.BlockSpec((1, DIFF_HEADS, cpt, LANES, ATT_TILE), lambda i: (i // tpb, 0, i % tpb, 0, 0)))
        out_shape = (row_shape, row_shape,
                     jax.ShapeDtypeStruct((n_batch, DIFF_HEADS, seq, LANES), BF16),
                     jax.ShapeDtypeStruct((n_batch, DIFF_HEADS, nq, LANES, ATT_TILE), BF16),
                     jax.ShapeDtypeStruct((n_batch, DIFF_HEADS, nq, LANES, ATT_TILE), BF16))
    else:
        out_specs = (row_spec, row_spec, row_spec)
        out_shape = (row_shape, row_shape, row_shape)
    return pl.pallas_call(
        functools.partial(_qkv_kernel, prompt=prompt),
        grid=(t // tm,), in_specs=in_specs, out_specs=out_specs, out_shape=out_shape,
        compiler_params=_params(), name="qkv_prompt" if prompt else "qkv_sample",
    )(x, mix_g, w_in, group_ones, q_g, k_g, cos, sin_lo, sin_hi)


def _flash_kernel(lam_ref, qt_ref, k_ref, vt_ref, sg_ref, o_ref, s_scr, p_scr, acc_scr, *, lam_init):
    tq = ATT_TILE
    n_heads = qt_ref.shape[1]
    i = pl.program_id(2)
    frow = lax.broadcasted_iota(jnp.int32, (LANES, tq), 0)
    qbd = []
    for g in range(n_heads):
        qt = qt_ref[0, g, 0]
        zero = jnp.zeros_like(qt)
        qbd.append(jnp.concatenate([jnp.where(frow < DIFF_HEAD_DIM, qt, zero),
                                    jnp.where(frow >= DIFF_HEAD_DIM, qt, zero)], axis=1))

    def scores(g, j):
        return _dot(k_ref[0, g, pl.ds(pl.multiple_of(j * tq, tq), tq), :], qbd[g])

    def softmax(s, m, l):
        m_new = jnp.maximum(m, jnp.max(s, axis=0, keepdims=True))
        p = jnp.exp2(s - m_new)
        alpha = jnp.exp2(m - m_new)
        return m_new, alpha * l + jnp.sum(p, axis=0, keepdims=True), alpha, p.astype(BF16)

    for g in range(n_heads):
        s_scr[g, 0] = scores(g, 0)
        p_scr[g, 1] = jnp.zeros(p_scr.shape[2:], p_scr.dtype)
        acc_scr[g] = jnp.zeros(acc_scr.shape[1:], acc_scr.dtype)

    def stage(g, j, cur, oth, m, l):
        pv = _dot(vt_ref[0, g, jnp.maximum(j - 1, 0)], p_scr[g, oth])
        m, l, alpha, p = softmax(s_scr[g, cur], m, l)
        p_scr[g, cur] = p
        acc_scr[g] = alpha * (acc_scr[g] + pv)
        s_scr[g, oth] = scores(g, j + 1)
        return m, l

    def stage_pair(jj, carry):
        out = []
        for g in range(n_heads):
            m, l = stage(g, 2 * jj, 0, 1, *carry[g])
            out.append(stage(g, 2 * jj + 1, 1, 0, m, l))
        return tuple(out)

    init = ((jnp.full((1, 2 * tq), NEG_BIG, F32), jnp.zeros((1, 2 * tq), F32)),) * n_heads
    carry = lax.fori_loop(0, i // 2, stage_pair, init)

    r = 2 * (i // 2)
    odd = i > r
    kchunk = lax.broadcasted_iota(jnp.int32, (tq, 2 * tq), 0) // CHUNK
    qchunk = (lax.broadcasted_iota(jnp.int32, (tq, 2 * tq), 1) % tq) // CHUNK
    causal = kchunk <= qchunk
    lam = lam_ref[0]
    for g in range(n_heads):
        m, l = carry[g]
        pv = _dot(vt_ref[0, g, jnp.maximum(r - 1, 0)], p_scr[g, 1])
        s_r = s_scr[g, 0]
        m, l, alpha, p_r = softmax(jnp.where(causal, s_r, jnp.where(odd, s_r, NEG_BIG)), m, l)
        acc = alpha * (acc_scr[g] + pv)
        s_i = scores(g, i)
        pv = _dot(vt_ref[0, g, r], p_r)
        m, l, alpha, p_i = softmax(jnp.where(causal, jnp.where(odd, s_i, NEG_BIG), NEG_BIG), m, l)
        acc = alpha * (acc + pv) + _dot(vt_ref[0, g, i], p_i)
        o = acc[:, :tq] / l[:, :tq] - lam * (acc[:, tq:] / l[:, tq:])
        on = o * lax.rsqrt(jnp.mean(o * o, axis=0, keepdims=True) + EPS) * sg_ref[...] * (1.0 - lam_init)
        o_ref[0, :, g * LANES:(g + 1) * LANES] = on.T.astype(o_ref.dtype)


def _flash_attention(lam, qt, kb, vt, sub_g_col, lam_init):
    n_batch, _, nq, _, tq = qt.shape
    seq = nq * tq
    hg = FLASH_HEADS_PER_STEP
    return pl.pallas_call(
        functools.partial(_flash_kernel, lam_init=lam_init),
        grid=(n_batch, DIFF_HEADS // hg, nq),
        in_specs=[pl.BlockSpec(memory_space=pltpu.SMEM),
                  pl.BlockSpec((1, hg, 1, LANES, tq), lambda b, h, i: (b, h, i, 0, 0)),
                  pl.BlockSpec((1, hg, seq, LANES), lambda b, h, i: (b, h, 0, 0)),
                  pl.BlockSpec((1, hg, nq, LANES, tq), lambda b, h, i: (b, h, 0, 0, 0)),
                  pl.BlockSpec((LANES, 1), lambda b, h, i: (0, 0))],
        out_specs=pl.BlockSpec((1, tq, hg * LANES), lambda b, h, i: (b, i, h)),
        out_shape=jax.ShapeDtypeStruct((n_batch, seq, D_MODEL), BF16),
        scratch_shapes=[pltpu.VMEM((hg, 2, tq, 2 * tq), F32),
                        pltpu.VMEM((hg, 2, tq, 2 * tq), BF16),
                        pltpu.VMEM((hg, LANES, 2 * tq), F32)],
        compiler_params=_params(3), name="diff_flash",
    )(lam, qt, kb, vt, sub_g_col)


def _sample_attn_kernel(lam_ref, q_ref, kn_ref, vn_ref, ck_ref, cv_ref, sg_ref, o_ref, *, lam_init, past):
    q = q_ref[0]
    kn = kn_ref[0]
    vn = vn_ref[0]
    lam = lam_ref[0]
    lane = lax.broadcasted_iota(jnp.int32, (q.shape[0], LANES), 1)
    contract_last = (((1,), (1,)), ((), ()))
    outs = []
    for h in range(DIFF_HEADS):
        lanes = slice(h * LANES, (h + 1) * LANES)
        kc = ck_ref[0, pl.ds(h, past, stride=DIFF_HEADS), :].astype(BF16)
        vc = cv_ref[0, pl.ds(h, past, stride=DIFF_HEADS), :].astype(BF16)
        qh = q[:, lanes]
        knh = kn[:, lanes].astype(BF16)
        vnh = vn[:, lanes].astype(BF16)
        sub = []
        for c in range(2):
            keep = (lane < DIFF_HEAD_DIM) if c == 0 else (lane >= DIFF_HEAD_DIM)
            qc = jnp.where(keep, qh, 0.0).astype(BF16)
            s_old = lax.dot_general(qc, kc, contract_last, preferred_element_type=F32)
            s_new = lax.dot_general(qc, knh, contract_last, preferred_element_type=F32)
            m = jnp.maximum(jnp.max(s_old, axis=1, keepdims=True), jnp.max(s_new, axis=1, keepdims=True))
            p_old = jnp.exp(s_old - m)
            p_new = jnp.exp(s_new - m)
            l = jnp.sum(p_old, axis=1, keepdims=True) + jnp.sum(p_new, axis=1, keepdims=True)
            sub.append((_dot(p_old.astype(BF16), vc) + _dot(p_new.astype(BF16), vnh)) / l)
        o = sub[0] - lam * sub[1]
        outs.append(_rms(o, sg_ref[...]) * (1.0 - lam_init))
    o_ref[0] = jnp.concatenate(outs, axis=1).astype(o_ref.dtype)


def _sample_attention(lam, q, kn, vn, cache_k, cache_v, sub_g_row, lam_init):
    nb, rows, _ = q.shape
    past = cache_k.shape[1] // DIFF_HEADS
    new_spec = pl.BlockSpec((1, rows, D_MODEL), lambda b: (b, 0, 0))
    cache_spec = pl.BlockSpec((1, past * DIFF_HEADS, LANES), lambda b: (b, 0, 0))
    return pl.pallas_call(
        functools.partial(_sample_attn_kernel, lam_init=lam_init, past=past),
        grid=(nb,),
        in_specs=[pl.BlockSpec(memory_space=pltpu.SMEM), new_spec, new_spec, new_spec, cache_spec, cache_spec,
                  _const_spec((1, LANES))],
        out_specs=new_spec, out_shape=jax.ShapeDtypeStruct((nb, rows, D_MODEL), BF16),
        compiler_params=_params(), name="sample_attn",
    )(lam, q, kn, vn, cache_k, cache_v, sub_g_row)


def _attn_out_kernel(a_ref, x_ref, wout_ref, ffng_ref, wrhi_ref, wrlo_ref,
                     x1_ref, hn3_ref, route_ref, route_t_ref, counts_ref, base_ref):
    x1 = x_ref[...] + _dot(a_ref[...], wout_ref[...])
    x1_ref[...] = x1
    _route_epilogue(x1, ffng_ref[...], wrhi_ref[...], wrlo_ref[...], base_ref,
                    hn3_ref, route_ref, route_t_ref, counts_ref)


def _attn_out_layer(a, x, w_out, ffn_g, wr_hi, wr_lo):
    t = x.shape[0]
    tm = min(OUT_PROJ_TILE, t)
    row_spec = pl.BlockSpec((tm, D_MODEL), lambda i: (i, 0))
    return pl.pallas_call(
        _attn_out_kernel,
        grid=(t // tm,),
        in_specs=[row_spec, row_spec, _const_spec((D_MODEL, D_MODEL))] + _route_in_specs(),
        out_specs=(row_spec,) + _route_out_specs(tm),
        out_shape=(jax.ShapeDtypeStruct((t, D_MODEL), F32),) + _route_out_shapes(t),
        scratch_shapes=[pltpu.VMEM((1, ROUTE_LANES), F32)],
        compiler_params=_params(), name="attn_out",
    )(a, x, w_out, ffn_g, wr_hi, wr_lo)


def _load_tile_indices(dest_hbm, idx_smem, idx_sem):
    n = idx_smem.shape[0]
    start = pl.multiple_of(pl.program_id(0) * n, n)
    idx_copy = pltpu.make_async_copy(dest_hbm.at[pl.ds(start, n)], idx_smem, idx_sem)
    idx_copy.start()
    idx_copy.wait()


def _index_tile_len(n_indices):
    return -(-n_indices // INDEX_SLICE_WORDS) * INDEX_SLICE_WORDS


def _per_step_dest(dest, tokens_per_step):
    steps = dest.shape[1] // tokens_per_step
    return dest.reshape(2, steps, tokens_per_step).transpose(1, 0, 2).reshape(steps, 2 * tokens_per_step)


def _tiled_indices(per_step):
    n = per_step.shape[1]
    return jnp.pad(per_step, ((0, 0), (0, _index_tile_len(n) - n))).reshape(-1)


def _dispatch_kernel(idx_hbm, hn_ref, xs3_hbm, idx_smem, zero_buf, idx_sem, row_sem, *, td, n_fill):
    _load_tile_indices(idx_hbm, idx_smem, idx_sem)
    zero_buf[...] = jnp.zeros_like(zero_buf)

    def scatter(t, carry):
        src = hn_ref.at[pl.ds(pl.multiple_of(t * ROW_TILES, ROW_TILES), ROW_TILES)]
        pltpu.make_async_copy(src, xs3_hbm.at[idx_smem[t]], row_sem).start(priority=0)
        pltpu.make_async_copy(src, xs3_hbm.at[idx_smem[td + t]], row_sem).start(priority=1)
        return carry

    def fill(p, carry):
        pltpu.make_async_copy(zero_buf, xs3_hbm.at[idx_smem[2 * td + p]], row_sem).start()
        return carry

    lax.fori_loop(0, td, scatter, 0)
    lax.fori_loop(0, n_fill, fill, 0)
    done = xs3_hbm.at[pl.ds(0, 2 * td + n_fill)]
    pltpu.make_async_copy(done, done, row_sem).wait()


def _dispatch(dest, pad_rows, hn2, n_rows):
    t = hn2.shape[0] // ROW_TILES
    td = min(ROW_DMA_TILE, t)
    steps = t // td
    n_fill = pad_rows.shape[0] // steps
    table = jnp.concatenate([_per_step_dest(dest, td), pad_rows.reshape(steps, n_fill)], axis=1)
    return pl.pallas_call(
        functools.partial(_dispatch_kernel, td=td, n_fill=n_fill),
        grid=(steps,),
        in_specs=[pl.BlockSpec(memory_space=pl.ANY),
                  pl.BlockSpec((td * ROW_TILES, LANES), lambda i: (i, 0))],
        out_specs=pl.BlockSpec(memory_space=pl.ANY),
        out_shape=jax.ShapeDtypeStruct((n_rows, ROW_TILES, LANES), F32),
        scratch_shapes=[pltpu.SMEM((_index_tile_len(2 * td + n_fill),), jnp.int32),
                        pltpu.VMEM((ROW_TILES, LANES), F32),
                        pltpu.SemaphoreType.DMA, pltpu.SemaphoreType.DMA],
        compiler_params=_params(), name="moe_dispatch",
    )(_tiled_indices(table), hn2)


def _expert_kernel(bexp_ref, nvalid_ref, xs_ref, wg_ref, wu_ref, wd_ref, yb_ref, wg_bf, wu_bf, wd_bf):
    blk = xs_ref.shape[0] // ROW_TILES
    i = pl.program_id(0)
    nvalid = nvalid_ref[i]

    @pl.when(jnp.logical_or(i == 0, bexp_ref[i] != bexp_ref[jnp.maximum(i - 1, 0)]))
    def _():
        wg_bf[...] = wg_ref[0, 0].astype(BF16)
        wu_bf[...] = wu_ref[0, 0].astype(BF16)
        wd_bf[...] = wd_ref[0, 0].astype(BF16)

    @pl.when(nvalid > 0)
    def _():
        x = jnp.concatenate([xs_ref[pl.ds(s, blk, stride=ROW_TILES), :] for s in range(ROW_TILES)], axis=1)
        row = lax.broadcasted_iota(jnp.int32, (blk, 1), 0)
        xb = jnp.where(row < nvalid, x, 0.0).astype(BF16)
        hg = _dot(xb, wg_bf[...])
        hu = _dot(xb, wu_bf[...])
        act = (hg * (1.0 / (1.0 + jnp.exp(-hg))) * hu).astype(BF16)
        y = _dot(act, wd_bf[...])
        for s in range(ROW_TILES):
            yb_ref[pl.ds(s, blk, stride=ROW_TILES), :] = y[:, s * LANES:(s + 1) * LANES]

    @pl.when(nvalid <= 0)
    def _():
        yb_ref[...] = jnp.zeros_like(yb_ref)


def _experts(block_expert, block_nvalid, xs2, layer, wg, wu, wd):
    n_blocks = block_expert.shape[0]
    blk = xs2.shape[0] // ROW_TILES // n_blocks
    rows_spec = pl.BlockSpec((blk * ROW_TILES, LANES), lambda i, be, nv: (i, 0))
    up_spec = pl.BlockSpec((1, 1, D_MODEL, EXPERT_HIDDEN), lambda i, be, nv: (layer, be[i], 0, 0))
    down_spec = pl.BlockSpec((1, 1, EXPERT_HIDDEN, D_MODEL), lambda i, be, nv: (layer, be[i], 0, 0))
    return pl.pallas_call(
        _expert_kernel,
        grid_spec=pltpu.PrefetchScalarGridSpec(
            num_scalar_prefetch=2, grid=(n_blocks,),
            in_specs=[rows_spec, up_spec, up_spec, down_spec],
            out_specs=rows_spec,
            scratch_shapes=[pltpu.VMEM((D_MODEL, EXPERT_HIDDEN), BF16), pltpu.VMEM((D_MODEL, EXPERT_HIDDEN), BF16),
                            pltpu.VMEM((EXPERT_HIDDEN, D_MODEL), BF16)]),
        out_shape=jax.ShapeDtypeStruct(xs2.shape, F32),
        compiler_params=_params(), name="moe_experts",
    )(block_expert, block_nvalid, xs2, wg, wu, wd)


def _combine_kernel(dest_hbm, x1_ref, route_ref, yb3_hbm, out_ref, idx_smem, buf0, buf1, idx_sem, row_sem):
    tc = x1_ref.shape[0]
    _load_tile_indices(dest_hbm, idx_smem, idx_sem)

    def body(t, carry):
        dst = pl.ds(pl.multiple_of(t * ROW_TILES, ROW_TILES), ROW_TILES)
        pltpu.make_async_copy(yb3_hbm.at[idx_smem[t]], buf0.at[dst], row_sem).start(priority=0)
        pltpu.make_async_copy(yb3_hbm.at[idx_smem[tc + t]], buf1.at[dst], row_sem).start(priority=1)
        return carry

    lax.fori_loop(0, tc, body, 0)
    pltpu.make_async_copy(buf0, buf0, row_sem).wait()
    pltpu.make_async_copy(buf1, buf1, row_sem).wait()
    g1 = route_ref[:, 2:3]
    g2 = route_ref[:, 3:4]
    for s in range(ROW_TILES):
        lanes = slice(s * LANES, (s + 1) * LANES)
        y1 = buf0[pl.ds(s, tc, stride=ROW_TILES), :]
        y2 = buf1[pl.ds(s, tc, stride=ROW_TILES), :]
        out_ref[:, lanes] = x1_ref[:, lanes] + (g1 * y1 + g2 * y2)


def _combine(dest, x1, route, yb3):
    t = x1.shape[0]
    tc = min(ROW_DMA_TILE, t)
    dest_flat = _tiled_indices(_per_step_dest(dest, tc))
    return pl.pallas_call(
        _combine_kernel,
        grid=(t // tc,),
        in_specs=[pl.BlockSpec(memory_space=pl.ANY),
                  pl.BlockSpec((tc, D_MODEL), lambda i: (i, 0)),
                  pl.BlockSpec((tc, ROUTE_COLS), lambda i: (i, 0)),
                  pl.BlockSpec(memory_space=pl.ANY)],
        out_specs=pl.BlockSpec((tc, D_MODEL), lambda i: (i, 0)),
        out_shape=jax.ShapeDtypeStruct((t, D_MODEL), F32),
        scratch_shapes=[pltpu.SMEM((_index_tile_len(2 * tc),), jnp.int32),
                        pltpu.VMEM((tc * ROW_TILES, LANES), F32), pltpu.VMEM((tc * ROW_TILES, LANES), F32),
                        pltpu.SemaphoreType.DMA, pltpu.SemaphoreType.DMA],
        compiler_params=_params(), name="moe_combine",
    )(dest_flat, x1, route, yb3)


def _count_le(sorted_ends, values):
    ends = sorted_ends.reshape((-1,) + (1,) * values.ndim)
    return jnp.sum((ends <= values[None]).astype(jnp.int32), axis=0)


def _lookup(table, idx):
    keys = jnp.arange(table.shape[0], dtype=jnp.int32).reshape((-1,) + (1,) * idx.ndim)
    return jnp.sum(jnp.where(idx[None] == keys, table.reshape(keys.shape), 0), axis=0)


def _moe_block_rows(t):
    mean_rows_per_expert = 2 * t // N_EXPERTS
    return min(MOE_BLOCK_ROWS, max(BF16_TILE_ROWS, mean_rows_per_expert))


def _moe(x1, hn3, route, route_t, counts, layer, wg, wu, wd):
    t = x1.shape[0]
    blk = _moe_block_rows(t)
    expert = route_t[0:2].astype(jnp.int32)
    rank = route_t[4:6].astype(jnp.int32)
    cnt = counts[0, :N_EXPERTS].astype(jnp.int32)
    padded = (cnt + blk - 1) // blk * blk
    pad_end = jnp.cumsum(padded)
    pad_start = pad_end - padded
    dest = _lookup(pad_start, expert) + rank
    n_blocks = -(-2 * t // blk) + N_EXPERTS
    blk_start = jnp.arange(n_blocks, dtype=jnp.int32) * blk
    block_expert = jnp.minimum(_count_le(pad_end, blk_start), N_EXPERTS - 1)
    block_nvalid = jnp.clip(_lookup(pad_start + cnt, block_expert) - blk_start, 0, blk).astype(jnp.int32)
    n_rows = n_blocks * blk
    gap_start = jnp.concatenate([pad_start + cnt, pad_end[-1:]])
    gap_len = jnp.concatenate([padded - cnt, n_rows - pad_end[-1:]])
    gap_end = jnp.cumsum(gap_len)
    p = jnp.arange(n_rows - 2 * t, dtype=jnp.int32)
    pad_rows = (p + _lookup(gap_start - gap_end + gap_len, _count_le(gap_end, p))).astype(jnp.int32)
    xs3 = _dispatch(dest, pad_rows, hn3, n_rows)
    yb2 = _experts(block_expert, block_nvalid, xs3.reshape(n_rows * ROW_TILES, LANES), layer, wg, wu, wd)
    return _combine(dest, x1, route, yb2.reshape(n_rows, ROW_TILES, LANES))


def _router_weights(w_group, w_router):
    w = jnp.concatenate([w_router, w_group,
                         jnp.zeros((D_MODEL, ROUTE_LANES - N_EXPERTS - MOE_GROUPS), F32)], axis=1)
    hi = lax.bitcast_convert_type(lax.bitcast_convert_type(w, jnp.uint32) & jnp.uint32(0xFFFF0000), F32)
    return hi.astype(BF16), (w - hi).astype(BF16)


def _rope_tables(pos):
    half = ROT_DIM // 2
    inv_freq = jnp.power(ROPE_THETA, -jnp.arange(half, dtype=F32) * (2.0 / ROT_DIM))
    ang = pos.astype(F32)[:, None] * inv_freq[None, :]
    cos, sin = jnp.cos(ang), jnp.sin(ang)
    n = pos.shape[0]
    ones = jnp.ones((n, DIFF_HEAD_DIM - ROT_DIM), F32)
    zeros = jnp.zeros((n, DIFF_HEAD_DIM - ROT_DIM), F32)
    zh = jnp.zeros((n, half), F32)
    sub_cos = jnp.concatenate([cos, cos, ones], axis=1)
    sub_lo = jnp.concatenate([-sin, zh, zeros], axis=1)
    sub_hi = jnp.concatenate([zh, sin, zeros], axis=1)
    return tuple(jnp.concatenate([a, a], axis=1) for a in (sub_cos, sub_lo, sub_hi))


def _spatial_weights(w_s, b_s, lc):
    pos = jnp.arange(lc)
    mask = (pos[None, :] // CHUNK) <= (pos[:, None] // CHUNK)
    ws = jnp.where(mask[None], w_s[:, :lc, :lc], 0.0)
    reps = GMLP_CHUNK // lc
    eye = jnp.eye(reps, dtype=F32)
    ws_eff = jnp.einsum("ab,gij->gaibj", eye, ws).reshape(GMLP_GROUPS, GMLP_CHUNK, GMLP_CHUNK)
    b_rows = jnp.tile(b_s[:, :lc], (1, reps))
    b_exp = jnp.repeat(b_rows.T, GMLP_GROUP_DIM, axis=1)
    return ws_eff, b_exp


def kernel(x_prompt, x_sample, cache_attn_k, cache_attn_v, mix_norm, ffn_norm, gmlp_w_in, gmlp_v_norm, gmlp_w_s, gmlp_b_s, gmlp_w_out, attn_w_in, attn_q_norm, attn_k_norm, attn_lam_q1, attn_lam_k1, attn_lam_q2, attn_lam_k2, attn_sub_norm, attn_w_out, moe_w_group, moe_w_router, moe_w_gate, moe_w_up, moe_w_down):
    nb_p, seq, _ = x_prompt.shape
    nb_s, dec, _ = x_sample.shape
    past = cache_attn_k.shape[2]
    assert cache_attn_k.shape[0] == DEPTH // 2 == 1 and mix_norm.shape[0] == DEPTH
    xp = x_prompt.reshape(nb_p * seq, D_MODEL)
    xs = x_sample.reshape(nb_s * dec, D_MODEL)
    row = lambda a: a.reshape(1, -1)

    router0 = _router_weights(moe_w_group[0], moe_w_router[0])
    gm_tail = (row(ffn_norm[0]),) + router0
    experts0 = (0, moe_w_gate, moe_w_up, moe_w_down)
    ws_p, b_p = _spatial_weights(gmlp_w_s[0], gmlp_b_s[0], GMLP_CHUNK)
    ws_s, b_s = _spatial_weights(gmlp_w_s[0], gmlp_b_s[0], dec)
    *mixed_p, gv_p = _gmlp_layer(
        xp, nb_p, row(mix_norm[0]), gmlp_w_in[0].astype(BF16), row(gmlp_v_norm[0]), ws_p.astype(BF16), b_p,
        gmlp_w_out[0].astype(BF16), *gm_tail)
    *mixed_s, gv_s = _gmlp_layer(
        xs, 1, row(mix_norm[0]), gmlp_w_in[0], row(gmlp_v_norm[0]), ws_s, b_s, gmlp_w_out[0], *gm_tail)
    xp = _moe(*mixed_p, *experts0)
    xs = _moe(*mixed_s, *experts0)

    lam_init = 0.8 - 0.6 * math.exp(-0.3 * 1)
    lam = (jnp.exp(jnp.sum(attn_lam_q1[0] * attn_lam_k1[0])) - jnp.exp(jnp.sum(attn_lam_q2[0] * attn_lam_k2[0]))
           + lam_init).reshape(1).astype(F32)
    grp = jnp.arange(D_MODEL) // DIFF_HEAD_DIM
    group_ones = (grp[:, None] == grp[None, :]).astype(BF16)
    qk = (row(mix_norm[1]), attn_w_in[0].astype(BF16), group_ones,
          row(jnp.tile(attn_q_norm[0], 2 * DIFF_HEADS)), row(jnp.tile(attn_k_norm[0], 2 * DIFF_HEADS)))
    router1 = _router_weights(moe_w_group[1], moe_w_router[1])
    at_tail = (attn_w_out[0].astype(BF16), row(ffn_norm[1])) + router1
    experts1 = (1, moe_w_gate, moe_w_up, moe_w_down)

    kp, vp, kb, vt, qt = _qkv_layer(xp, nb_p, *qk, *_rope_tables(jnp.arange(seq)), prompt=True)
    ap = _flash_attention(lam, qt, kb, vt, attn_sub_norm[0].reshape(LANES, 1), lam_init)
    xp = _moe(*_attn_out_layer(ap.reshape(nb_p * seq, D_MODEL), xp, *at_tail), *experts1)

    pos_s = jnp.tile(past + jnp.arange(dec), nb_s)
    qs, ks, vs = _qkv_layer(xs, nb_s, *qk, *_rope_tables(pos_s), prompt=False)
    shp = (nb_s, dec, D_MODEL)
    a_s = _sample_attention(lam, qs.reshape(shp), ks.reshape(shp), vs.reshape(shp),
                            cache_attn_k.reshape(nb_s, past * DIFF_HEADS, LANES),
                            cache_attn_v.reshape(nb_s, past * DIFF_HEADS, LANES),
                            row(attn_sub_norm[0]), lam_init)
    xs = _moe(*_attn_out_layer(a_s.reshape(nb_s * dec, D_MODEL), xs, *at_tail), *experts1)

    hv = (DIFF_HEADS, DIFF_VALUE_DIM)
    return (xp.reshape(nb_p, seq, D_MODEL), xs.reshape(nb_s, dec, D_MODEL),
            gv_p[None], gv_s.reshape(1, nb_s, dec, GMLP_WIDTH),
            kp.reshape(1, nb_p, seq, *hv), vp.reshape(1, nb_p, seq, *hv),
            ks.reshape(1, nb_s, dec, *hv), vs.reshape(1, nb_s, dec, *hv))
```

```python
import functools
import math

import jax
import jax.numpy as jnp
from jax import lax
from jax.experimental import pallas as pl
from jax.experimental.pallas import tpu as pltpu

D_MODEL = 1024
DEPTH = 2
CHUNK = 64
GMLP_CHUNK = 128
GMLP_WIDTH = 2 * D_MODEL
GMLP_GROUPS = 8
GMLP_GROUP_DIM = GMLP_WIDTH // GMLP_GROUPS
DIFF_HEADS = 8
DIFF_HEAD_DIM = D_MODEL // (2 * DIFF_HEADS)
DIFF_VALUE_DIM = 2 * DIFF_HEAD_DIM
ROT_DIM = DIFF_HEAD_DIM // 4
ROPE_THETA = 500000.0
MOE_GROUPS = 4
MOE_EXPERTS_PER_GROUP = 8
N_EXPERTS = MOE_GROUPS * MOE_EXPERTS_PER_GROUP
EXPERT_HIDDEN = D_MODEL // 2
EPS = 1e-6

LANES = 128
SUBLANES = 8
BF16_TILE_ROWS = 2 * SUBLANES
ROW_TILES = D_MODEL // LANES
VMEM_LIMIT_BYTES = 56 * 1024 * 1024

ROUTE_LANES = LANES
GROUP_LANE0 = N_EXPERTS
ROUTE_COLS = 8
NEG_BIG = -1e30

MIX_TILE = 512
OUT_PROJ_TILE = 256
ATT_TILE = 256
FLASH_HEADS_PER_STEP = 4
MOE_BLOCK_ROWS = 256
ROW_DMA_TILE = 512
INDEX_SLICE_WORDS = 1024

F32 = jnp.float32
BF16 = jnp.bfloat16


def _params(n_axes=1):
    return pltpu.CompilerParams(dimension_semantics=("arbitrary",) * n_axes,
                                vmem_limit_bytes=VMEM_LIMIT_BYTES)


def _rms(x, g):
    return x * lax.rsqrt(jnp.mean(x * x, axis=-1, keepdims=True) + EPS) * g


def _dot(a, b):
    return jnp.dot(a, b, preferred_element_type=F32)


def _dot_f32(a, b):
    return jnp.dot(a, b, preferred_element_type=F32, precision=lax.Precision.HIGHEST)


def _const_spec(shape):
    return pl.BlockSpec(shape, lambda *_: (0,) * len(shape), pipeline_mode=pl.Buffered(1))


def _route_epilogue(x1, ffn_g, wr_hi, wr_lo, base_ref, hn3_ref, route_ref, route_t_ref, counts_ref):
    tm = x1.shape[0]

    @pl.when(pl.program_id(0) == 0)
    def _():
        base_ref[...] = jnp.zeros_like(base_ref)

    hn = _rms(x1, ffn_g)
    for s in range(ROW_TILES):
        hn3_ref[pl.ds(s, tm, stride=ROW_TILES), :] = hn[:, s * LANES:(s + 1) * LANES]

    h_hi = hn.astype(BF16)
    h_lo = (hn - h_hi.astype(F32)).astype(BF16)
    both = _dot(h_hi, jnp.concatenate([wr_hi, wr_lo], axis=1))
    logit = both[:, :ROUTE_LANES] + both[:, ROUTE_LANES:] + _dot(h_lo, wr_hi)

    lane = lax.broadcasted_iota(jnp.int32, (tm, ROUTE_LANES), 1)
    far = jnp.int32(4 * ROUTE_LANES)
    lg = jnp.where(lane >= GROUP_LANE0, jnp.where(lane < GROUP_LANE0 + MOE_GROUPS, logit, NEG_BIG), NEG_BIG)
    mg = jnp.max(lg, axis=1, keepdims=True)
    g_lane = jnp.min(jnp.where(lg == mg, lane, far), axis=1, keepdims=True)
    g_sel = g_lane - GROUP_LANE0
    p_g = 1.0 / jnp.sum(jnp.exp(lg - mg), axis=1, keepdims=True)

    lo_lane = g_sel * MOE_EXPERTS_PER_GROUP
    le = jnp.where(lane >= lo_lane, jnp.where(lane < lo_lane + MOE_EXPERTS_PER_GROUP, logit, NEG_BIG), NEG_BIG)
    m1 = jnp.max(le, axis=1, keepdims=True)
    j1 = jnp.min(jnp.where(le == m1, lane, far), axis=1, keepdims=True)
    le2 = jnp.where(lane == j1, NEG_BIG, le)
    m2 = jnp.max(le2, axis=1, keepdims=True)
    j2 = jnp.min(jnp.where(le2 == m2, lane, far), axis=1, keepdims=True)
    r = jnp.exp(m2 - m1)
    gate1 = p_g / (1.0 + r)
    gate2 = p_g * r / (1.0 + r)

    hit1 = lane == j1
    hit2 = lane == j2
    onehot = jnp.where(hit1, 1.0, jnp.where(hit2, 1.0, 0.0))
    row = lax.broadcasted_iota(jnp.int32, (tm, tm), 0)
    col = lax.broadcasted_iota(jnp.int32, (tm, tm), 1)
    earlier = jnp.where(row > col, 1.0, 0.0).astype(BF16)
    prefix = _dot(earlier, onehot.astype(BF16)) + base_ref[...]
    rank1 = jnp.sum(jnp.where(hit1, prefix, 0.0), axis=1, keepdims=True)
    rank2 = jnp.sum(jnp.where(hit2, prefix, 0.0), axis=1, keepdims=True)
    base_new = base_ref[...] + jnp.sum(onehot, axis=0, keepdims=True)
    base_ref[...] = base_new
    counts_ref[...] = base_new

    rec = jnp.where(lane == 0, j1.astype(F32),
          jnp.where(lane == 1, j2.astype(F32),
          jnp.where(lane == 2, gate1,
          jnp.where(lane == 3, gate2,
          jnp.where(lane == 4, rank1,
          jnp.where(lane == 5, rank2, 0.0))))))
    route_ref[...] = rec[:, :ROUTE_COLS]
    route_t_ref[...] = rec.T[:ROUTE_COLS]


def _route_out_shapes(t):
    return (jax.ShapeDtypeStruct((t * ROW_TILES, LANES), F32),
            jax.ShapeDtypeStruct((t, ROUTE_COLS), F32),
            jax.ShapeDtypeStruct((ROUTE_COLS, t), F32),
            jax.ShapeDtypeStruct((1, ROUTE_LANES), F32))


def _route_out_specs(tm):
    return (pl.BlockSpec((tm * ROW_TILES, LANES), lambda i: (i, 0)),
            pl.BlockSpec((tm, ROUTE_COLS), lambda i: (i, 0)),
            pl.BlockSpec((ROUTE_COLS, tm), lambda i: (0, i)),
            pl.BlockSpec((1, ROUTE_LANES), lambda i: (0, 0)))


def _route_in_specs():
    return [_const_spec((1, D_MODEL)), _const_spec((D_MODEL, ROUTE_LANES)), _const_spec((D_MODEL, ROUTE_LANES))]


def _gelu_tanh(x):
    cdf = 0.5 * (1.0 + jnp.tanh(math.sqrt(2.0 / math.pi) * (x + 0.044715 * (x * x * x))))
    return x * cdf


def _gmlp_kernel(x_ref, mixg_ref, win_ref, vng_ref, ws_ref, bexp_ref, wout_ref,
                 ffng_ref, wrhi_ref, wrlo_ref,
                 x1_ref, hn3_ref, route_ref, route_t_ref, counts_ref, vlast_ref, base_ref, *, tiles_per_batch):
    tm = x_ref.shape[0]
    mm = win_ref.dtype
    dot = _dot if mm == BF16 else _dot_f32
    x = x_ref[...]
    h = _rms(x, mixg_ref[...]).astype(mm)
    z = _gelu_tanh(dot(h, win_ref[...]))
    u = z[:, :GMLP_WIDTH]
    vn = _rms(z[:, GMLP_WIDTH:], vng_ref[...])
    vb = vn.astype(mm)
    gated = []
    for c in range(tm // GMLP_CHUNK):
        rows = slice(c * GMLP_CHUNK, (c + 1) * GMLP_CHUNK)
        s = jnp.concatenate(
            [dot(ws_ref[g], vb[rows, g * GMLP_GROUP_DIM:(g + 1) * GMLP_GROUP_DIM])
             for g in range(GMLP_GROUPS)], axis=1) + bexp_ref[...]
        gated.append((u[rows] * s).astype(mm))
    y = dot(jnp.concatenate(gated, axis=0), wout_ref[...])
    x1 = x + y
    x1_ref[...] = x1

    @pl.when(pl.program_id(0) % tiles_per_batch == tiles_per_batch - 1)
    def _():
        vlast_ref[0] = vn[tm - GMLP_CHUNK:]

    _route_epilogue(x1, ffng_ref[...], wrhi_ref[...], wrlo_ref[...], base_ref,
                    hn3_ref, route_ref, route_t_ref, counts_ref)


def _gmlp_layer(x, n_batch, mix_g, w_in, vn_g, ws_eff, b_exp, w_out, ffn_g, wr_hi, wr_lo):
    t = x.shape[0]
    tm = min(MIX_TILE, t)
    tiles_per_batch = t // n_batch // tm
    gw = GMLP_WIDTH
    kern = functools.partial(_gmlp_kernel, tiles_per_batch=tiles_per_batch)
    return pl.pallas_call(
        kern,
        grid=(t // tm,),
        in_specs=[pl.BlockSpec((tm, D_MODEL), lambda i: (i, 0)),
                  _const_spec((1, D_MODEL)), _const_spec((D_MODEL, 2 * gw)), _const_spec((1, gw)),
                  _const_spec((GMLP_GROUPS, GMLP_CHUNK, GMLP_CHUNK)), _const_spec((GMLP_CHUNK, gw)),
                  _const_spec((gw, D_MODEL))] + _route_in_specs(),
        out_specs=(pl.BlockSpec((tm, D_MODEL), lambda i: (i, 0)),) + _route_out_specs(tm)
                  + (pl.BlockSpec((1, GMLP_CHUNK, gw), lambda i: (i // tiles_per_batch, 0, 0)),),
        out_shape=(jax.ShapeDtypeStruct((t, D_MODEL), F32),) + _route_out_shapes(t)
                  + (jax.ShapeDtypeStruct((n_batch, GMLP_CHUNK, gw), F32),),
        scratch_shapes=[pltpu.VMEM((1, ROUTE_LANES), F32)],
        compiler_params=_params(),
        name="gmlp_mixer",
    )(x, mix_g, w_in, vn_g, ws_eff, b_exp, w_out, ffn_g, wr_hi, wr_lo)


def _qk_norm_rope(t, group_ones, gain, cos, sin_lo, sin_hi):
    ms = _dot((t * t).astype(BF16), group_ones) * (1.0 / DIFF_HEAD_DIM)
    tn = t * lax.rsqrt(ms + EPS) * gain
    heads = []
    for h in range(DIFF_HEADS):
        th = tn[:, h * LANES:(h + 1) * LANES]
        heads.append(th * cos + pltpu.roll(th, LANES - ROT_DIM // 2, 1) * sin_lo
                     + pltpu.roll(th, ROT_DIM // 2, 1) * sin_hi)
    return jnp.concatenate(heads, axis=1)


def _qkv_kernel(x_ref, *refs, prompt):
    _qkv_body(x_ref[...], *refs, prompt=prompt)


def _combine_qkv_kernel(idx_hbm, x1_ref, route_ref, yb3_hbm, *refs, n_steps):
    params, x_ref, qkv_outs, scratch = refs[:8], refs[8], refs[9:-4], refs[-4:]
    x = _combine_prefetched(idx_hbm, x1_ref, route_ref, yb3_hbm, *scratch, n_steps=n_steps)
    x_ref[...] = x
    _qkv_body(x, *params, *qkv_outs, prompt=True)


def _qkv_body(x, mixg_ref, win_ref, ones_ref, qg_ref, kg_ref, cos_ref, slo_ref, shi_ref, *outs, prompt):
    tm = x.shape[0]
    h = _rms(x, mixg_ref[...]).astype(BF16)
    qkv = _dot(h, win_ref[...])
    rope = (cos_ref[...], slo_ref[...], shi_ref[...])
    q_scale = DIFF_HEAD_DIM ** -0.5 * (math.log2(math.e) if prompt else 1.0)
    q = _qk_norm_rope(qkv[:, :D_MODEL], ones_ref[...], qg_ref[...], *rope) * q_scale
    k = _qk_norm_rope(qkv[:, D_MODEL:2 * D_MODEL], ones_ref[...], kg_ref[...], *rope)
    v = qkv[:, 2 * D_MODEL:]
    if not prompt:
        q_ref, kout_ref, vout_ref = outs
        q_ref[...] = q
        kout_ref[...] = k
        vout_ref[...] = v
        return
    kout_ref, vout_ref, kb_ref, vt_ref, qt_ref = outs
    kout_ref[...] = k
    vout_ref[...] = v
    for hd in range(DIFF_HEADS):
        lanes = slice(hd * LANES, (hd + 1) * LANES)
        kb_ref[0, hd] = k[:, lanes].astype(BF16)
        for c in range(tm // ATT_TILE):
            rows = slice(c * ATT_TILE, (c + 1) * ATT_TILE)
            vt_ref[0, hd, c] = v[rows, lanes].T.astype(BF16)
            qt_ref[0, hd, c] = q[rows, lanes].T.astype(BF16)


def _qkv_layer(x, n_batch, mix_g, w_in, group_ones, q_g, k_g, cos, sin_lo, sin_hi, prompt, pending_combine=None):
    t = x.shape[0]
    tm = min(MIX_TILE, t)
    seq = t // n_batch
    tpb = seq // tm if prompt else 1
    n_tab = cos.shape[0] // tm
    tab_spec = pl.BlockSpec((tm, LANES), lambda i: (i % n_tab, 0))
    row_spec = pl.BlockSpec((tm, D_MODEL), lambda i: (i, 0))
    in_specs = [row_spec, _const_spec((1, D_MODEL)), _const_spec((D_MODEL, 3 * D_MODEL)),
                _const_spec((D_MODEL, D_MODEL)), _const_spec((1, D_MODEL)), _const_spec((1, D_MODEL)),
                tab_spec, tab_spec, tab_spec]
    row_shape = jax.ShapeDtypeStruct((t, D_MODEL), F32)
    if prompt:
        nq = seq // ATT_TILE
        cpt = tm // ATT_TILE
        out_specs = (row_spec, row_spec,
                     pl.BlockSpec((1, DIFF_HEADS, tm, LANES), lambda i: (i // tpb, 0, i % tpb, 0)),
                     pl.BlockSpec((1, DIFF_HEADS, cpt, LANES, ATT_TILE), lambda i: (i // tpb, 0, i % tpb, 0, 0)),
                     pl.BlockSpec((1, DIFF_HEADS, cpt, LANES, ATT_TILE), lambda i: (i // tpb, 0, i % tpb, 0, 0)))
        out_shape = (row_shape, row_shape,
                     jax.ShapeDtypeStruct((n_batch, DIFF_HEADS, seq, LANES), BF16),
                     jax.ShapeDtypeStruct((n_batch, DIFF_HEADS, nq, LANES, ATT_TILE), BF16),
                     jax.ShapeDtypeStruct((n_batch, DIFF_HEADS, nq, LANES, ATT_TILE), BF16))
    else:
        out_specs = (row_spec, row_spec, row_spec)
        out_shape = (row_shape, row_shape, row_shape)
    params = (mix_g, w_in, group_ones, q_g, k_g, cos, sin_lo, sin_hi)
    if pending_combine is None:
        return pl.pallas_call(
            functools.partial(_qkv_kernel, prompt=prompt),
            grid=(t // tm,), in_specs=in_specs, out_specs=out_specs, out_shape=out_shape,
            compiler_params=_params(), name="qkv_prompt" if prompt else "qkv_sample",
        )(x, *params)
    dest, route, yb3 = pending_combine
    any_spec = pl.BlockSpec(memory_space=pl.ANY)
    return pl.pallas_call(
        functools.partial(_combine_qkv_kernel, n_steps=t // tm),
        grid=(t // tm,),
        in_specs=[any_spec, row_spec, pl.BlockSpec((tm, ROUTE_COLS), lambda i: (i, 0)), any_spec] + in_specs[1:],
        out_specs=(row_spec,) + out_specs, out_shape=(row_shape,) + out_shape,
        scratch_shapes=_combine_scratch(tm),
        compiler_params=_params(), name="combine_qkv_prompt",
    )(_tiled_indices(_per_step_dest(dest, tm)), x, route, yb3, *params)


def _flash_kernel(lam_ref, qt_ref, k_ref, vt_ref, sg_ref, o_ref, s_scr, p_scr, acc_scr, *, lam_init):
    tq = ATT_TILE
    n_heads = qt_ref.shape[1]
    i = pl.program_id(2)
    frow = lax.broadcasted_iota(jnp.int32, (LANES, tq), 0)
    qbd = []
    for g in range(n_heads):
        qt = qt_ref[0, g, 0]
        zero = jnp.zeros_like(qt)
        qbd.append(jnp.concatenate([jnp.where(frow < DIFF_HEAD_DIM, qt, zero),
                                    jnp.where(frow >= DIFF_HEAD_DIM, qt, zero)], axis=1))

    def scores(g, j):
        return _dot(k_ref[0, g, pl.ds(pl.multiple_of(j * tq, tq), tq), :], qbd[g])

    def softmax(s, m, l):
        m_new = jnp.maximum(m, jnp.max(s, axis=0, keepdims=True))
        p = jnp.exp2(s - m_new)
        alpha = jnp.exp2(m - m_new)
        return m_new, alpha * l + jnp.sum(p, axis=0, keepdims=True), alpha, p.astype(BF16)

    for g in range(n_heads):
        s_scr[g, 0] = scores(g, 0)
        p_scr[g, 1] = jnp.zeros(p_scr.shape[2:], p_scr.dtype)
        acc_scr[g] = jnp.zeros(acc_scr.shape[1:], acc_scr.dtype)

    def stage(g, j, cur, oth, m, l):
        pv = _dot(vt_ref[0, g, jnp.maximum(j - 1, 0)], p_scr[g, oth])
        m, l, alpha, p = softmax(s_scr[g, cur], m, l)
        p_scr[g, cur] = p
        acc_scr[g] = alpha * (acc_scr[g] + pv)
        s_scr[g, oth] = scores(g, j + 1)
        return m, l

    def stage_pair(jj, carry):
        out = []
        for g in range(n_heads):
            m, l = stage(g, 2 * jj, 0, 1, *carry[g])
            out.append(stage(g, 2 * jj + 1, 1, 0, m, l))
        return tuple(out)

    init = ((jnp.full((1, 2 * tq), NEG_BIG, F32), jnp.zeros((1, 2 * tq), F32)),) * n_heads
    carry = lax.fori_loop(0, i // 2, stage_pair, init)

    r = 2 * (i // 2)
    odd = i > r
    kchunk = lax.broadcasted_iota(jnp.int32, (tq, 2 * tq), 0) // CHUNK
    qchunk = (lax.broadcasted_iota(jnp.int32, (tq, 2 * tq), 1) % tq) // CHUNK
    causal = kchunk <= qchunk
    lam = lam_ref[0]
    for g in range(n_heads):
        m, l = carry[g]
        pv = _dot(vt_ref[0, g, jnp.maximum(r - 1, 0)], p_scr[g, 1])
        s_r = s_scr[g, 0]
        m, l, alpha, p_r = softmax(jnp.where(causal, s_r, jnp.where(odd, s_r, NEG_BIG)), m, l)
        acc = alpha * (acc_scr[g] + pv)
        s_i = scores(g, i)
        pv = _dot(vt_ref[0, g, r], p_r)
        m, l, alpha, p_i = softmax(jnp.where(causal, jnp.where(odd, s_i, NEG_BIG), NEG_BIG), m, l)
        acc = alpha * (acc + pv) + _dot(vt_ref[0, g, i], p_i)
        o = acc[:, :tq] / l[:, :tq] - lam * (acc[:, tq:] / l[:, tq:])
        on = o * lax.rsqrt(jnp.mean(o * o, axis=0, keepdims=True) + EPS) * sg_ref[...] * (1.0 - lam_init)
        o_ref[0, :, g * LANES:(g + 1) * LANES] = on.T.astype(o_ref.dtype)


def _flash_attention(lam, qt, kb, vt, sub_g_col, lam_init):
    n_batch, _, nq, _, tq = qt.shape
    seq = nq * tq
    hg = FLASH_HEADS_PER_STEP
    return pl.pallas_call(
        functools.partial(_flash_kernel, lam_init=lam_init),
        grid=(n_batch, DIFF_HEADS // hg, nq),
        in_specs=[pl.BlockSpec(memory_space=pltpu.SMEM),
                  pl.BlockSpec((1, hg, 1, LANES, tq), lambda b, h, i: (b, h, i, 0, 0)),
                  pl.BlockSpec((1, hg, seq, LANES), lambda b, h, i: (b, h, 0, 0)),
                  pl.BlockSpec((1, hg, nq, LANES, tq), lambda b, h, i: (b, h, 0, 0, 0)),
                  pl.BlockSpec((LANES, 1), lambda b, h, i: (0, 0))],
        out_specs=pl.BlockSpec((1, tq, hg * LANES), lambda b, h, i: (b, i, h)),
        out_shape=jax.ShapeDtypeStruct((n_batch, seq, D_MODEL), BF16),
        scratch_shapes=[pltpu.VMEM((hg, 2, tq, 2 * tq), F32),
                        pltpu.VMEM((hg, 2, tq, 2 * tq), BF16),
                        pltpu.VMEM((hg, LANES, 2 * tq), F32)],
        compiler_params=_params(3), name="diff_flash",
    )(lam, qt, kb, vt, sub_g_col)


def _sample_attn_kernel(lam_ref, q_ref, kn_ref, vn_ref, ck_ref, cv_ref, sg_ref, o_ref, *, lam_init, past):
    q = q_ref[0]
    kn = kn_ref[0]
    vn = vn_ref[0]
    lam = lam_ref[0]
    lane = lax.broadcasted_iota(jnp.int32, (q.shape[0], LANES), 1)
    contract_last = (((1,), (1,)), ((), ()))
    outs = []
    for h in range(DIFF_HEADS):
        lanes = slice(h * LANES, (h + 1) * LANES)
        kc = ck_ref[0, pl.ds(h, past, stride=DIFF_HEADS), :].astype(BF16)
        vc = cv_ref[0, pl.ds(h, past, stride=DIFF_HEADS), :].astype(BF16)
        qh = q[:, lanes]
        knh = kn[:, lanes].astype(BF16)
        vnh = vn[:, lanes].astype(BF16)
        sub = []
        for c in range(2):
            keep = (lane < DIFF_HEAD_DIM) if c == 0 else (lane >= DIFF_HEAD_DIM)
            qc = jnp.where(keep, qh, 0.0).astype(BF16)
            s_old = lax.dot_general(qc, kc, contract_last, preferred_element_type=F32)
            s_new = lax.dot_general(qc, knh, contract_last, preferred_element_type=F32)
            m = jnp.maximum(jnp.max(s_old, axis=1, keepdims=True), jnp.max(s_new, axis=1, keepdims=True))
            p_old = jnp.exp(s_old - m)
            p_new = jnp.exp(s_new - m)
            l = jnp.sum(p_old, axis=1, keepdims=True) + jnp.sum(p_new, axis=1, keepdims=True)
            sub.append((_dot(p_old.astype(BF16), vc) + _dot(p_new.astype(BF16), vnh)) / l)
        o = sub[0] - lam * sub[1]
        outs.append(_rms(o, sg_ref[...]) * (1.0 - lam_init))
    o_ref[0] = jnp.concatenate(outs, axis=1).astype(o_ref.dtype)


def _sample_attention(lam, q, kn, vn, cache_k, cache_v, sub_g_row, lam_init):
    nb, rows, _ = q.shape
    past = cache_k.shape[1] // DIFF_HEADS
    new_spec = pl.BlockSpec((1, rows, D_MODEL), lambda b: (b, 0, 0))
    cache_spec = pl.BlockSpec((1, past * DIFF_HEADS, LANES), lambda b: (b, 0, 0))
    return pl.pallas_call(
        functools.partial(_sample_attn_kernel, lam_init=lam_init, past=past),
        grid=(nb,),
        in_specs=[pl.BlockSpec(memory_space=pltpu.SMEM), new_spec, new_spec, new_spec, cache_spec, cache_spec,
                  _const_spec((1, LANES))],
        out_specs=new_spec, out_shape=jax.ShapeDtypeStruct((nb, rows, D_MODEL), BF16),
        compiler_params=_params(), name="sample_attn",
    )(lam, q, kn, vn, cache_k, cache_v, sub_g_row)


def _attn_out_kernel(a_ref, x_ref, wout_ref, ffng_ref, wrhi_ref, wrlo_ref,
                     x1_ref, hn3_ref, route_ref, route_t_ref, counts_ref, base_ref):
    x1 = x_ref[...] + _dot(a_ref[...], wout_ref[...])
    x1_ref[...] = x1
    _route_epilogue(x1, ffng_ref[...], wrhi_ref[...], wrlo_ref[...], base_ref,
                    hn3_ref, route_ref, route_t_ref, counts_ref)


def _attn_out_layer(a, x, w_out, ffn_g, wr_hi, wr_lo):
    t = x.shape[0]
    tm = min(OUT_PROJ_TILE, t)
    row_spec = pl.BlockSpec((tm, D_MODEL), lambda i: (i, 0))
    return pl.pallas_call(
        _attn_out_kernel,
        grid=(t // tm,),
        in_specs=[row_spec, row_spec, _const_spec((D_MODEL, D_MODEL))] + _route_in_specs(),
        out_specs=(row_spec,) + _route_out_specs(tm),
        out_shape=(jax.ShapeDtypeStruct((t, D_MODEL), F32),) + _route_out_shapes(t),
        scratch_shapes=[pltpu.VMEM((1, ROUTE_LANES), F32)],
        compiler_params=_params(), name="attn_out",
    )(a, x, w_out, ffn_g, wr_hi, wr_lo)


def _load_tile_indices(dest_hbm, idx_smem, idx_sem):
    n = idx_smem.shape[0]
    start = pl.multiple_of(pl.program_id(0) * n, n)
    idx_copy = pltpu.make_async_copy(dest_hbm.at[pl.ds(start, n)], idx_smem, idx_sem)
    idx_copy.start()
    idx_copy.wait()


def _index_tile_len(n_indices):
    return -(-n_indices // INDEX_SLICE_WORDS) * INDEX_SLICE_WORDS


def _per_step_dest(dest, tokens_per_step):
    steps = dest.shape[1] // tokens_per_step
    return dest.reshape(2, steps, tokens_per_step).transpose(1, 0, 2).reshape(steps, 2 * tokens_per_step)


def _tiled_indices(per_step):
    n = per_step.shape[1]
    return jnp.pad(per_step, ((0, 0), (0, _index_tile_len(n) - n))).reshape(-1)


def _dispatch_kernel(idx_hbm, *refs, plan):
    n_streams = len(plan)
    hn_refs = refs[:n_streams]
    xs3_hbm, idx_smem, zero_buf, idx_sem, row_sem = refs[n_streams:]
    i = pl.program_id(0)
    zero_buf[...] = jnp.zeros_like(zero_buf)

    for hn_ref, (first, steps, td, n_fill, offset, length) in zip(hn_refs, plan):
        @pl.when(jnp.logical_and(i >= first, i < first + steps))
        def _(hn_ref=hn_ref, first=first, td=td, n_fill=n_fill, offset=offset, length=length):
            start = pl.multiple_of(offset + (i - first) * length, INDEX_SLICE_WORDS)
            idx_copy = pltpu.make_async_copy(idx_hbm.at[pl.ds(start, length)], idx_smem.at[pl.ds(0, length)], idx_sem)
            idx_copy.start()
            idx_copy.wait()

            def scatter(t, carry):
                src = hn_ref.at[pl.ds(pl.multiple_of(t * ROW_TILES, ROW_TILES), ROW_TILES)]
                pltpu.make_async_copy(src, xs3_hbm.at[idx_smem[t]], row_sem).start(priority=0)
                pltpu.make_async_copy(src, xs3_hbm.at[idx_smem[td + t]], row_sem).start(priority=1)
                return carry

            def fill(p, carry):
                pltpu.make_async_copy(zero_buf, xs3_hbm.at[idx_smem[2 * td + p]], row_sem).start()
                return carry

            lax.fori_loop(0, td, scatter, 0)
            lax.fori_loop(0, n_fill, fill, 0)
            done = xs3_hbm.at[pl.ds(0, 2 * td + n_fill)]
            pltpu.make_async_copy(done, done, row_sem).wait()


def _dispatch(dests, pad_rows, hn2s, n_rows):
    plan, tables, in_specs = [], [], [pl.BlockSpec(memory_space=pl.ANY)]
    first = offset = 0
    for k, (dest, hn2) in enumerate(zip(dests, hn2s)):
        t = hn2.shape[0] // ROW_TILES
        td = min(ROW_DMA_TILE, t)
        steps = t // td
        fill = pad_rows if k == len(dests) - 1 else pad_rows[:0]
        n_fill = fill.shape[0] // steps
        table = _tiled_indices(jnp.concatenate([_per_step_dest(dest, td), fill.reshape(steps, n_fill)], axis=1))
        length = table.shape[0] // steps
        plan.append((first, steps, td, n_fill, offset, length))
        tables.append(table)
        in_specs.append(pl.BlockSpec((td * ROW_TILES, LANES),
                                     lambda i, first=first, steps=steps: (jnp.clip(i - first, 0, steps - 1), 0)))
        first += steps
        offset += table.shape[0]
    return pl.pallas_call(
        functools.partial(_dispatch_kernel, plan=tuple(plan)),
        grid=(first,),
        in_specs=in_specs,
        out_specs=pl.BlockSpec(memory_space=pl.ANY),
        out_shape=jax.ShapeDtypeStruct((n_rows, ROW_TILES, LANES), F32),
        scratch_shapes=[pltpu.SMEM((max(p[5] for p in plan),), jnp.int32),
                        pltpu.VMEM((ROW_TILES, LANES), F32),
                        pltpu.SemaphoreType.DMA, pltpu.SemaphoreType.DMA],
        compiler_params=_params(), name="moe_dispatch",
    )(jnp.concatenate(tables), *hn2s)


def _expert_kernel(bexp_ref, nvalid_ref, xs_ref, wg_ref, wu_ref, wd_ref, yb_ref, wg_bf, wu_bf, wd_bf):
    blk = xs_ref.shape[0] // ROW_TILES
    i = pl.program_id(0)
    nvalid = nvalid_ref[i]

    @pl.when(jnp.logical_or(i == 0, bexp_ref[i] != bexp_ref[jnp.maximum(i - 1, 0)]))
    def _():
        wg_bf[...] = wg_ref[0, 0].astype(BF16)
        wu_bf[...] = wu_ref[0, 0].astype(BF16)
        wd_bf[...] = wd_ref[0, 0].astype(BF16)

    @pl.when(nvalid > 0)
    def _():
        x = jnp.concatenate([xs_ref[pl.ds(s, blk, stride=ROW_TILES), :] for s in range(ROW_TILES)], axis=1)
        row = lax.broadcasted_iota(jnp.int32, (blk, 1), 0)
        xb = jnp.where(row < nvalid, x, 0.0).astype(BF16)
        hg = _dot(xb, wg_bf[...])
        hu = _dot(xb, wu_bf[...])
        act = (hg * (1.0 / (1.0 + jnp.exp(-hg))) * hu).astype(BF16)
        y = _dot(act, wd_bf[...])
        for s in range(ROW_TILES):
            yb_ref[pl.ds(s, blk, stride=ROW_TILES), :] = y[:, s * LANES:(s + 1) * LANES]

    @pl.when(nvalid <= 0)
    def _():
        yb_ref[...] = jnp.zeros_like(yb_ref)


def _experts(block_expert, block_nvalid, xs2, layer, wg, wu, wd):
    n_blocks = block_expert.shape[0]
    blk = xs2.shape[0] // ROW_TILES // n_blocks
    rows_spec = pl.BlockSpec((blk * ROW_TILES, LANES), lambda i, be, nv: (i, 0))
    up_spec = pl.BlockSpec((1, 1, D_MODEL, EXPERT_HIDDEN), lambda i, be, nv: (layer, be[i], 0, 0))
    down_spec = pl.BlockSpec((1, 1, EXPERT_HIDDEN, D_MODEL), lambda i, be, nv: (layer, be[i], 0, 0))
    return pl.pallas_call(
        _expert_kernel,
        grid_spec=pltpu.PrefetchScalarGridSpec(
            num_scalar_prefetch=2, grid=(n_blocks,),
            in_specs=[rows_spec, up_spec, up_spec, down_spec],
            out_specs=rows_spec,
            scratch_shapes=[pltpu.VMEM((D_MODEL, EXPERT_HIDDEN), BF16), pltpu.VMEM((D_MODEL, EXPERT_HIDDEN), BF16),
                            pltpu.VMEM((EXPERT_HIDDEN, D_MODEL), BF16)]),
        out_shape=jax.ShapeDtypeStruct(xs2.shape, F32),
        compiler_params=_params(), name="moe_experts",
    )(block_expert, block_nvalid, xs2, wg, wu, wd)


def _combine_kernel(dest_hbm, x1_ref, route_ref, yb3_hbm, out_ref, idx_smem, buf0, buf1, idx_sem, row_sem):
    tc = x1_ref.shape[0]
    _load_tile_indices(dest_hbm, idx_smem, idx_sem)

    def body(t, carry):
        dst = pl.ds(pl.multiple_of(t * ROW_TILES, ROW_TILES), ROW_TILES)
        pltpu.make_async_copy(yb3_hbm.at[idx_smem[t]], buf0.at[dst], row_sem).start(priority=0)
        pltpu.make_async_copy(yb3_hbm.at[idx_smem[tc + t]], buf1.at[dst], row_sem).start(priority=1)
        return carry

    lax.fori_loop(0, tc, body, 0)
    pltpu.make_async_copy(buf0, buf0, row_sem).wait()
    pltpu.make_async_copy(buf1, buf1, row_sem).wait()
    g1 = route_ref[:, 2:3]
    g2 = route_ref[:, 3:4]
    for s in range(ROW_TILES):
        lanes = slice(s * LANES, (s + 1) * LANES)
        y1 = buf0[pl.ds(s, tc, stride=ROW_TILES), :]
        y2 = buf1[pl.ds(s, tc, stride=ROW_TILES), :]
        out_ref[:, lanes] = x1_ref[:, lanes] + (g1 * y1 + g2 * y2)


def _combine_prefetched(idx_hbm, x1_ref, route_ref, yb3_hbm, idx_smem, bufs, idx_sem, row_sems, *, n_steps):
    tm = x1_ref.shape[0]
    n_idx = idx_smem.shape[0] // 2
    i = pl.program_id(0)

    def idx_copy(step):
        slot = lax.rem(step, 2)
        return pltpu.make_async_copy(idx_hbm.at[pl.ds(pl.multiple_of(step * n_idx, n_idx), n_idx)],
                                     idx_smem.at[pl.ds(pl.multiple_of(slot * n_idx, n_idx), n_idx)], idx_sem)

    def start_rows(step):
        slot = lax.rem(step, 2)
        base = slot * n_idx

        def body(t, carry):
            dst = pl.ds(pl.multiple_of(t * ROW_TILES, ROW_TILES), ROW_TILES)
            pltpu.make_async_copy(yb3_hbm.at[idx_smem[base + t]], bufs.at[slot, 0, dst],
                                  row_sems.at[slot]).start(priority=0)
            pltpu.make_async_copy(yb3_hbm.at[idx_smem[base + tm + t]], bufs.at[slot, 1, dst],
                                  row_sems.at[slot]).start(priority=1)
            return carry

        lax.fori_loop(0, tm, body, 0)

    @pl.when(i == 0)
    def _():
        first = idx_copy(0)
        first.start()
        first.wait()
        start_rows(0)
        if n_steps > 1:
            idx_copy(1).start()

    @pl.when(i + 1 < n_steps)
    def _():
        idx_copy(i + 1).wait()
        start_rows(i + 1)

        @pl.when(i + 2 < n_steps)
        def _():
            idx_copy(i + 2).start()

    slot = lax.rem(i, 2)
    mine = bufs.at[slot]
    pltpu.make_async_copy(mine, mine, row_sems.at[slot]).wait()
    g1 = route_ref[:, 2:3]
    g2 = route_ref[:, 3:4]
    cols = []
    for s in range(ROW_TILES):
        lanes = slice(s * LANES, (s + 1) * LANES)
        y1 = bufs[slot, 0, pl.ds(s, tm, stride=ROW_TILES), :]
        y2 = bufs[slot, 1, pl.ds(s, tm, stride=ROW_TILES), :]
        cols.append(x1_ref[:, lanes] + (g1 * y1 + g2 * y2))
    return jnp.concatenate(cols, axis=1)


def _combine_scratch(tm):
    return [pltpu.SMEM((2 * _index_tile_len(2 * tm),), jnp.int32),
            pltpu.VMEM((2, 2, tm * ROW_TILES, LANES), F32),
            pltpu.SemaphoreType.DMA, pltpu.SemaphoreType.DMA((2,))]


def _combine(dest, x1, route, yb3):
    t = x1.shape[0]
    tc = min(ROW_DMA_TILE, t)
    dest_flat = _tiled_indices(_per_step_dest(dest, tc))
    return pl.pallas_call(
        _combine_kernel,
        grid=(t // tc,),
        in_specs=[pl.BlockSpec(memory_space=pl.ANY),
                  pl.BlockSpec((tc, D_MODEL), lambda i: (i, 0)),
                  pl.BlockSpec((tc, ROUTE_COLS), lambda i: (i, 0)),
                  pl.BlockSpec(memory_space=pl.ANY)],
        out_specs=pl.BlockSpec((tc, D_MODEL), lambda i: (i, 0)),
        out_shape=jax.ShapeDtypeStruct((t, D_MODEL), F32),
        scratch_shapes=[pltpu.SMEM((_index_tile_len(2 * tc),), jnp.int32),
                        pltpu.VMEM((tc * ROW_TILES, LANES), F32), pltpu.VMEM((tc * ROW_TILES, LANES), F32),
                        pltpu.SemaphoreType.DMA, pltpu.SemaphoreType.DMA],
        compiler_params=_params(), name="moe_combine",
    )(dest_flat, x1, route, yb3)


def _count_le(sorted_ends, values):
    ends = sorted_ends.reshape((-1,) + (1,) * values.ndim)
    return jnp.sum((ends <= values[None]).astype(jnp.int32), axis=0)


def _lookup(table, idx):
    keys = jnp.arange(table.shape[0], dtype=jnp.int32).reshape((-1,) + (1,) * idx.ndim)
    return jnp.sum(jnp.where(idx[None] == keys, table.reshape(keys.shape), 0), axis=0)


def _moe_block_rows(t):
    mean_rows_per_expert = max(2 * t // N_EXPERTS, 1)
    return min(MOE_BLOCK_ROWS, max(BF16_TILE_ROWS, 1 << (mean_rows_per_expert.bit_length() - 1)))


def _moe_experts(streams, layer, wg, wu, wd):
    t = sum(s[0].shape[0] for s in streams)
    blk = _moe_block_rows(t)
    cnts = [s[4][0, :N_EXPERTS].astype(jnp.int32) for s in streams]
    cnt = sum(cnts)
    padded = (cnt + blk - 1) // blk * blk
    pad_end = jnp.cumsum(padded)
    pad_start = pad_end - padded
    dests, seen = [], jnp.zeros_like(cnt)
    for s, c in zip(streams, cnts):
        expert = s[3][0:2].astype(jnp.int32)
        rank = s[3][4:6].astype(jnp.int32)
        dests.append(_lookup(pad_start + seen, expert) + rank)
        seen = seen + c
    n_blocks = -(-2 * t // blk) + N_EXPERTS
    blk_start = jnp.arange(n_blocks, dtype=jnp.int32) * blk
    block_expert = jnp.minimum(_count_le(pad_end, blk_start), N_EXPERTS - 1)
    block_nvalid = jnp.clip(_lookup(pad_start + cnt, block_expert) - blk_start, 0, blk).astype(jnp.int32)
    n_rows = n_blocks * blk
    gap_start = jnp.concatenate([pad_start + cnt, pad_end[-1:]])
    gap_len = jnp.concatenate([padded - cnt, n_rows - pad_end[-1:]])
    gap_end = jnp.cumsum(gap_len)
    p = jnp.arange(n_rows - 2 * t, dtype=jnp.int32)
    pad_rows = (p + _lookup(gap_start - gap_end + gap_len, _count_le(gap_end, p))).astype(jnp.int32)
    xs3 = _dispatch(dests, pad_rows, [s[1] for s in streams], n_rows)
    yb2 = _experts(block_expert, block_nvalid, xs3.reshape(n_rows * ROW_TILES, LANES), layer, wg, wu, wd)
    return yb2.reshape(n_rows, ROW_TILES, LANES), dests


def _router_weights(w_group, w_router):
    w = jnp.concatenate([w_router, w_group,
                         jnp.zeros((D_MODEL, ROUTE_LANES - N_EXPERTS - MOE_GROUPS), F32)], axis=1)
    hi = lax.bitcast_convert_type(lax.bitcast_convert_type(w, jnp.uint32) & jnp.uint32(0xFFFF0000), F32)
    return hi.astype(BF16), (w - hi).astype(BF16)


def _rope_tables(pos):
    half = ROT_DIM // 2
    inv_freq = jnp.power(ROPE_THETA, -jnp.arange(half, dtype=F32) * (2.0 / ROT_DIM))
    ang = pos.astype(F32)[:, None] * inv_freq[None, :]
    cos, sin = jnp.cos(ang), jnp.sin(ang)
    n = pos.shape[0]
    ones = jnp.ones((n, DIFF_HEAD_DIM - ROT_DIM), F32)
    zeros = jnp.zeros((n, DIFF_HEAD_DIM - ROT_DIM), F32)
    zh = jnp.zeros((n, half), F32)
    sub_cos = jnp.concatenate([cos, cos, ones], axis=1)
    sub_lo = jnp.concatenate([-sin, zh, zeros], axis=1)
    sub_hi = jnp.concatenate([zh, sin, zeros], axis=1)
    return tuple(jnp.concatenate([a, a], axis=1) for a in (sub_cos, sub_lo, sub_hi))


def _spatial_weights(w_s, b_s, lc):
    pos = jnp.arange(lc)
    mask = (pos[None, :] // CHUNK) <= (pos[:, None] // CHUNK)
    ws = jnp.where(mask[None], w_s[:, :lc, :lc], 0.0)
    reps = GMLP_CHUNK // lc
    eye = jnp.eye(reps, dtype=F32)
    ws_eff = jnp.einsum("ab,gij->gaibj", eye, ws).reshape(GMLP_GROUPS, GMLP_CHUNK, GMLP_CHUNK)
    b_rows = jnp.tile(b_s[:, :lc], (1, reps))
    b_exp = jnp.repeat(b_rows.T, GMLP_GROUP_DIM, axis=1)
    return ws_eff, b_exp


def kernel(x_prompt, x_sample, cache_attn_k, cache_attn_v, mix_norm, ffn_norm, gmlp_w_in, gmlp_v_norm, gmlp_w_s, gmlp_b_s, gmlp_w_out, attn_w_in, attn_q_norm, attn_k_norm, attn_lam_q1, attn_lam_k1, attn_lam_q2, attn_lam_k2, attn_sub_norm, attn_w_out, moe_w_group, moe_w_router, moe_w_gate, moe_w_up, moe_w_down):
    nb_p, seq, _ = x_prompt.shape
    nb_s, dec, _ = x_sample.shape
    past = cache_attn_k.shape[2]
    assert cache_attn_k.shape[0] == DEPTH // 2 == 1 and mix_norm.shape[0] == DEPTH
    xp = x_prompt.reshape(nb_p * seq, D_MODEL)
    xs = x_sample.reshape(nb_s * dec, D_MODEL)
    row = lambda a: a.reshape(1, -1)

    router0 = _router_weights(moe_w_group[0], moe_w_router[0])
    gm_tail = (row(ffn_norm[0]),) + router0
    experts0 = (0, moe_w_gate, moe_w_up, moe_w_down)
    ws_p, b_p = _spatial_weights(gmlp_w_s[0], gmlp_b_s[0], GMLP_CHUNK)
    ws_s, b_s = _spatial_weights(gmlp_w_s[0], gmlp_b_s[0], dec)
    *mixed_p, gv_p = _gmlp_layer(
        xp, nb_p, row(mix_norm[0]), gmlp_w_in[0].astype(BF16), row(gmlp_v_norm[0]), ws_p.astype(BF16), b_p,
        gmlp_w_out[0].astype(BF16), *gm_tail)
    *mixed_s, gv_s = _gmlp_layer(
        xs, 1, row(mix_norm[0]), gmlp_w_in[0], row(gmlp_v_norm[0]), ws_s, b_s, gmlp_w_out[0], *gm_tail)
    yb3, (dest_p, dest_s) = _moe_experts([mixed_p, mixed_s], *experts0)
    xs = _combine(dest_s, mixed_s[0], mixed_s[2], yb3)
    prompt_combine0 = (dest_p, mixed_p[2], yb3)

    lam_init = 0.8 - 0.6 * math.exp(-0.3 * 1)
    lam = (jnp.exp(jnp.sum(attn_lam_q1[0] * attn_lam_k1[0])) - jnp.exp(jnp.sum(attn_lam_q2[0] * attn_lam_k2[0]))
           + lam_init).reshape(1).astype(F32)
    grp = jnp.arange(D_MODEL) // DIFF_HEAD_DIM
    group_ones = (grp[:, None] == grp[None, :]).astype(BF16)
    qk = (row(mix_norm[1]), attn_w_in[0].astype(BF16), group_ones,
          row(jnp.tile(attn_q_norm[0], 2 * DIFF_HEADS)), row(jnp.tile(attn_k_norm[0], 2 * DIFF_HEADS)))
    router1 = _router_weights(moe_w_group[1], moe_w_router[1])
    at_tail = (attn_w_out[0].astype(BF16), row(ffn_norm[1])) + router1
    experts1 = (1, moe_w_gate, moe_w_up, moe_w_down)

    xp, kp, vp, kb, vt, qt = _qkv_layer(mixed_p[0], nb_p, *qk, *_rope_tables(jnp.arange(seq)), prompt=True,
                                        pending_combine=prompt_combine0)
    ap = _flash_attention(lam, qt, kb, vt, attn_sub_norm[0].reshape(LANES, 1), lam_init)
    mixed_p = _attn_out_layer(ap.reshape(nb_p * seq, D_MODEL), xp, *at_tail)

    pos_s = jnp.tile(past + jnp.arange(dec), nb_s)
    qs, ks, vs = _qkv_layer(xs, nb_s, *qk, *_rope_tables(pos_s), prompt=False)
    shp = (nb_s, dec, D_MODEL)
    a_s = _sample_attention(lam, qs.reshape(shp), ks.reshape(shp), vs.reshape(shp),
                            cache_attn_k.reshape(nb_s, past * DIFF_HEADS, LANES),
                            cache_attn_v.reshape(nb_s, past * DIFF_HEADS, LANES),
                            row(attn_sub_norm[0]), lam_init)
    mixed_s = _attn_out_layer(a_s.reshape(nb_s * dec, D_MODEL), xs, *at_tail)
    yb3, (dest_p, dest_s) = _moe_experts([mixed_p, mixed_s], *experts1)
    xp = _combine(dest_p, mixed_p[0], mixed_p[2], yb3)
    xs = _combine(dest_s, mixed_s[0], mixed_s[2], yb3)

    hv = (DIFF_HEADS, DIFF_VALUE_DIM)
    return (xp.reshape(nb_p, seq, D_MODEL), xs.reshape(nb_s, dec, D_MODEL),
            gv_p[None], gv_s.reshape(1, nb_s, dec, GMLP_WIDTH),
            kp.reshape(1, nb_p, seq, *hv), vp.reshape(1, nb_p, seq, *hv),
            ks.reshape(1, nb_s, dec, *hv), vs.reshape(1, nb_s, dec, *hv))
```

```python
import functools
import math

import jax
import jax.numpy as jnp
from jax import lax
from jax.experimental import pallas as pl
from jax.experimental.pallas import tpu as pltpu

D_MODEL = 1024
DEPTH = 2
CHUNK = 64
GMLP_CHUNK = 128
GMLP_WIDTH = 2 * D_MODEL
GMLP_GROUPS = 8
GMLP_GROUP_DIM = GMLP_WIDTH // GMLP_GROUPS
DIFF_HEADS = 8
DIFF_HEAD_DIM = D_MODEL // (2 * DIFF_HEADS)
DIFF_VALUE_DIM = 2 * DIFF_HEAD_DIM
ROT_DIM = DIFF_HEAD_DIM // 4
ROPE_THETA = 500000.0
MOE_GROUPS = 4
MOE_EXPERTS_PER_GROUP = 8
N_EXPERTS = MOE_GROUPS * MOE_EXPERTS_PER_GROUP
EXPERT_HIDDEN = D_MODEL // 2
EPS = 1e-6

LANES = 128
SUBLANES = 8
BF16_TILE_ROWS = 2 * SUBLANES
ROW_TILES = D_MODEL // LANES
VMEM_LIMIT_BYTES = 56 * 1024 * 1024

ROUTE_LANES = LANES
GROUP_LANE0 = N_EXPERTS
ROUTE_COLS = 8
NEG_BIG = -1e30

MIX_TILE = 512
OUT_PROJ_TILE = 256
ATT_K_TILE = 256
ATT_Q_TILE = 2 * ATT_K_TILE
FLASH_HEADS_PER_STEP = 4
MOE_BLOCK_ROWS = 256
ROW_DMA_TILE = 512
INDEX_SLICE_WORDS = 1024

F32 = jnp.float32
BF16 = jnp.bfloat16


def _params(n_axes=1):
    return pltpu.CompilerParams(dimension_semantics=("arbitrary",) * n_axes,
                                vmem_limit_bytes=VMEM_LIMIT_BYTES)


def _rms(x, g):
    return x * lax.rsqrt(jnp.mean(x * x, axis=-1, keepdims=True) + EPS) * g


def _dot(a, b):
    return jnp.dot(a, b, preferred_element_type=F32)


def _dot_f32(a, b):
    return jnp.dot(a, b, preferred_element_type=F32, precision=lax.Precision.HIGHEST)


def _const_spec(shape):
    return pl.BlockSpec(shape, lambda *_: (0,) * len(shape), pipeline_mode=pl.Buffered(1))


def _route_epilogue(x1, ffn_g, wr_hi, wr_hi_lo, base_ref, hn3_ref, route_ref, route_t_ref, counts_ref):
    tm = x1.shape[0]

    @pl.when(pl.program_id(0) == 0)
    def _():
        base_ref[...] = jnp.zeros_like(base_ref)

    hn = _rms(x1, ffn_g)
    for s in range(ROW_TILES):
        hn3_ref[pl.ds(s, tm, stride=ROW_TILES), :] = hn[:, s * LANES:(s + 1) * LANES]

    h_hi = hn.astype(BF16)
    h_lo = (hn - h_hi.astype(F32)).astype(BF16)
    both = _dot(h_hi, wr_hi_lo)
    logit = both[:, :ROUTE_LANES] + both[:, ROUTE_LANES:] + _dot(h_lo, wr_hi)

    lane = lax.broadcasted_iota(jnp.int32, (tm, ROUTE_LANES), 1)
    far = jnp.int32(4 * ROUTE_LANES)
    lg = jnp.where(lane >= GROUP_LANE0, jnp.where(lane < GROUP_LANE0 + MOE_GROUPS, logit, NEG_BIG), NEG_BIG)
    mg = jnp.max(lg, axis=1, keepdims=True)
    g_lane = jnp.min(jnp.where(lg == mg, lane, far), axis=1, keepdims=True)
    g_sel = g_lane - GROUP_LANE0
    p_g = 1.0 / jnp.sum(jnp.exp(lg - mg), axis=1, keepdims=True)

    lo_lane = g_sel * MOE_EXPERTS_PER_GROUP
    le = jnp.where(lane >= lo_lane, jnp.where(lane < lo_lane + MOE_EXPERTS_PER_GROUP, logit, NEG_BIG), NEG_BIG)
    m1 = jnp.max(le, axis=1, keepdims=True)
    j1 = jnp.min(jnp.where(le == m1, lane, far), axis=1, keepdims=True)
    le2 = jnp.where(lane == j1, NEG_BIG, le)
    m2 = jnp.max(le2, axis=1, keepdims=True)
    j2 = jnp.min(jnp.where(le2 == m2, lane, far), axis=1, keepdims=True)
    r = jnp.exp(m2 - m1)
    gate1 = p_g / (1.0 + r)
    gate2 = p_g * r / (1.0 + r)

    hit1 = lane == j1
    hit2 = lane == j2
    onehot = jnp.where(hit1, 1.0, jnp.where(hit2, 1.0, 0.0))
    row = lax.broadcasted_iota(jnp.int32, (tm, tm), 0)
    col = lax.broadcasted_iota(jnp.int32, (tm, tm), 1)
    earlier = jnp.where(row > col, 1.0, 0.0).astype(BF16)
    prefix = _dot(earlier, onehot.astype(BF16)) + base_ref[...]
    rank1 = jnp.sum(jnp.where(hit1, prefix, 0.0), axis=1, keepdims=True)
    rank2 = jnp.sum(jnp.where(hit2, prefix, 0.0), axis=1, keepdims=True)
    base_new = base_ref[...] + jnp.sum(onehot, axis=0, keepdims=True)
    base_ref[...] = base_new
    counts_ref[...] = base_new

    rec = jnp.where(lane == 0, j1.astype(F32),
          jnp.where(lane == 1, j2.astype(F32),
          jnp.where(lane == 2, gate1,
          jnp.where(lane == 3, gate2,
          jnp.where(lane == 4, rank1,
          jnp.where(lane == 5, rank2, 0.0))))))
    route_ref[...] = rec[:, :ROUTE_COLS]
    route_t_ref[...] = rec.T[:ROUTE_COLS]


def _route_out_shapes(t):
    return (jax.ShapeDtypeStruct((t * ROW_TILES, LANES), F32),
            jax.ShapeDtypeStruct((t, ROUTE_COLS), F32),
            jax.ShapeDtypeStruct((ROUTE_COLS, t), F32),
            jax.ShapeDtypeStruct((1, ROUTE_LANES), F32))


def _route_out_specs(tm):
    return (pl.BlockSpec((tm * ROW_TILES, LANES), lambda i: (i, 0)),
            pl.BlockSpec((tm, ROUTE_COLS), lambda i: (i, 0)),
            pl.BlockSpec((ROUTE_COLS, tm), lambda i: (0, i)),
            pl.BlockSpec((1, ROUTE_LANES), lambda i: (0, 0)))


def _route_in_specs():
    return [_const_spec((1, D_MODEL)), _const_spec((D_MODEL, ROUTE_LANES)), _const_spec((D_MODEL, 2 * ROUTE_LANES))]


def _gelu_tanh(x):
    cdf = 0.5 * (1.0 + jnp.tanh(math.sqrt(2.0 / math.pi) * (x + 0.044715 * (x * x * x))))
    return x * cdf


def _gmlp_kernel(x_ref, mixg_ref, win_ref, vng_ref, ws_ref, bexp_ref, wout_ref,
                 ffng_ref, wrhi_ref, wrlo_ref,
                 x1_ref, hn3_ref, route_ref, route_t_ref, counts_ref, vlast_ref, base_ref, *, tiles_per_batch):
    tm = x_ref.shape[0]
    mm = win_ref.dtype
    dot = _dot if mm == BF16 else _dot_f32
    x = x_ref[...]
    h = _rms(x, mixg_ref[...]).astype(mm)
    z = _gelu_tanh(dot(h, win_ref[...]))
    u = z[:, :GMLP_WIDTH]
    vn = _rms(z[:, GMLP_WIDTH:], vng_ref[...])
    vb = vn.astype(mm)
    gated = []
    for c in range(tm // GMLP_CHUNK):
        rows = slice(c * GMLP_CHUNK, (c + 1) * GMLP_CHUNK)
        s = jnp.concatenate(
            [dot(ws_ref[g], vb[rows, g * GMLP_GROUP_DIM:(g + 1) * GMLP_GROUP_DIM])
             for g in range(GMLP_GROUPS)], axis=1) + bexp_ref[...]
        gated.append((u[rows] * s).astype(mm))
    y = dot(jnp.concatenate(gated, axis=0), wout_ref[...])
    x1 = x + y
    x1_ref[...] = x1

    @pl.when(pl.program_id(0) % tiles_per_batch == tiles_per_batch - 1)
    def _():
        vlast_ref[0] = vn[tm - GMLP_CHUNK:]

    _route_epilogue(x1, ffng_ref[...], wrhi_ref[...], wrlo_ref[...], base_ref,
                    hn3_ref, route_ref, route_t_ref, counts_ref)


def _gmlp_layer(x, n_batch, mix_g, w_in, vn_g, ws_eff, b_exp, w_out, ffn_g, wr_hi, wr_lo):
    t = x.shape[0]
    tm = min(MIX_TILE, t)
    tiles_per_batch = t // n_batch // tm
    gw = GMLP_WIDTH
    kern = functools.partial(_gmlp_kernel, tiles_per_batch=tiles_per_batch)
    return pl.pallas_call(
        kern,
        grid=(t // tm,),
        in_specs=[pl.BlockSpec((tm, D_MODEL), lambda i: (i, 0)),
                  _const_spec((1, D_MODEL)), _const_spec((D_MODEL, 2 * gw)), _const_spec((1, gw)),
                  _const_spec((GMLP_GROUPS, GMLP_CHUNK, GMLP_CHUNK)), _const_spec((GMLP_CHUNK, gw)),
                  _const_spec((gw, D_MODEL))] + _route_in_specs(),
        out_specs=(pl.BlockSpec((tm, D_MODEL), lambda i: (i, 0)),) + _route_out_specs(tm)
                  + (pl.BlockSpec((1, GMLP_CHUNK, gw), lambda i: (i // tiles_per_batch, 0, 0)),),
        out_shape=(jax.ShapeDtypeStruct((t, D_MODEL), F32),) + _route_out_shapes(t)
                  + (jax.ShapeDtypeStruct((n_batch, GMLP_CHUNK, gw), F32),),
        scratch_shapes=[pltpu.VMEM((1, ROUTE_LANES), F32)],
        compiler_params=_params(),
        name="gmlp_mixer",
    )(x, mix_g, w_in, vn_g, ws_eff, b_exp, w_out, ffn_g, wr_hi, wr_lo)


def _qk_norm_rope(t, group_ones, gain, cos, sin_lo, sin_hi):
    ms = _dot((t * t).astype(BF16), group_ones) * (1.0 / DIFF_HEAD_DIM)
    tn = t * lax.rsqrt(ms + EPS) * gain
    heads = []
    for h in range(DIFF_HEADS):
        th = tn[:, h * LANES:(h + 1) * LANES]
        heads.append(th * cos + pltpu.roll(th, LANES - ROT_DIM // 2, 1) * sin_lo
                     + pltpu.roll(th, ROT_DIM // 2, 1) * sin_hi)
    return jnp.concatenate(heads, axis=1)


def _qkv_kernel(x_ref, mixg_ref, win_ref, ones_ref, qg_ref, kg_ref, cos_ref, slo_ref, shi_ref, *outs, prompt):
    tm = x_ref.shape[0]
    h = _rms(x_ref[...], mixg_ref[...]).astype(BF16)
    qkv = _dot(h, win_ref[...])
    rope = (cos_ref[...], slo_ref[...], shi_ref[...])
    q_scale = DIFF_HEAD_DIM ** -0.5 * (math.log2(math.e) if prompt else 1.0)
    q = _qk_norm_rope(qkv[:, :D_MODEL], ones_ref[...], qg_ref[...], *rope) * q_scale
    k = _qk_norm_rope(qkv[:, D_MODEL:2 * D_MODEL], ones_ref[...], kg_ref[...], *rope)
    v = qkv[:, 2 * D_MODEL:]
    if not prompt:
        q_ref, kout_ref, vout_ref = outs
        q_ref[...] = q
        kout_ref[...] = k
        vout_ref[...] = v
        return
    kout_ref, vout_ref, kb_ref, vt_ref, qt_ref = outs
    kout_ref[...] = k
    vout_ref[...] = v
    for hd in range(DIFF_HEADS):
        lanes = slice(hd * LANES, (hd + 1) * LANES)
        kb_ref[0, hd] = k[:, lanes].astype(BF16)
        for c in range(tm // ATT_K_TILE):
            vt_ref[0, hd, c] = v[c * ATT_K_TILE:(c + 1) * ATT_K_TILE, lanes].T.astype(BF16)
        for c in range(tm // ATT_Q_TILE):
            qt_ref[0, hd, c] = q[c * ATT_Q_TILE:(c + 1) * ATT_Q_TILE, lanes].T.astype(BF16)


def _qkv_layer(x, n_batch, mix_g, w_in, group_ones, q_g, k_g, cos, sin_lo, sin_hi, prompt):
    t = x.shape[0]
    tm = min(MIX_TILE, t)
    seq = t // n_batch
    tpb = seq // tm if prompt else 1
    n_tab = cos.shape[0] // tm
    tab_spec = pl.BlockSpec((tm, LANES), lambda i: (i % n_tab, 0))
    row_spec = pl.BlockSpec((tm, D_MODEL), lambda i: (i, 0))
    in_specs = [row_spec, _const_spec((1, D_MODEL)), _const_spec((D_MODEL, 3 * D_MODEL)),
                _const_spec((D_MODEL, D_MODEL)), _const_spec((1, D_MODEL)), _const_spec((1, D_MODEL)),
                tab_spec, tab_spec, tab_spec]
    row_shape = jax.ShapeDtypeStruct((t, D_MODEL), F32)
    if prompt:
        tq, tk = ATT_Q_TILE, ATT_K_TILE
        tiled = lambda tile: pl.BlockSpec((1, DIFF_HEADS, tm // tile, LANES, tile),
                                          lambda i: (i // tpb, 0, i % tpb, 0, 0))
        out_specs = (row_spec, row_spec,
                     pl.BlockSpec((1, DIFF_HEADS, tm, LANES), lambda i: (i // tpb, 0, i % tpb, 0)),
                     tiled(tk), tiled(tq))
        out_shape = (row_shape, row_shape,
                     jax.ShapeDtypeStruct((n_batch, DIFF_HEADS, seq, LANES), BF16),
                     jax.ShapeDtypeStruct((n_batch, DIFF_HEADS, seq // tk, LANES, tk), BF16),
                     jax.ShapeDtypeStruct((n_batch, DIFF_HEADS, seq // tq, LANES, tq), BF16))
    else:
        out_specs = (row_spec, row_spec, row_spec)
        out_shape = (row_shape, row_shape, row_shape)
    return pl.pallas_call(
        functools.partial(_qkv_kernel, prompt=prompt),
        grid=(t // tm,), in_specs=in_specs, out_specs=out_specs, out_shape=out_shape,
        compiler_params=_params(), name="qkv_prompt" if prompt else "qkv_sample",
    )(x, mix_g, w_in, group_ones, q_g, k_g, cos, sin_lo, sin_hi)


def _flash_kernel(lam_ref, qt_ref, k_ref, vt_ref, sg_ref, o_ref, s_scr, p_scr, acc_scr, *, lam_init):
    tq, tk = ATT_Q_TILE, ATT_K_TILE
    n_heads = qt_ref.shape[1]
    i = pl.program_id(2)
    frow = lax.broadcasted_iota(jnp.int32, (LANES, tq), 0)
    qbd = []
    for g in range(n_heads):
        qt = qt_ref[0, g, 0]
        zero = jnp.zeros_like(qt)
        qbd.append(jnp.concatenate([jnp.where(frow < DIFF_HEAD_DIM, qt, zero),
                                    jnp.where(frow >= DIFF_HEAD_DIM, qt, zero)], axis=1))

    def scores(g, j):
        return _dot(k_ref[0, g, pl.ds(pl.multiple_of(j * tk, tk), tk), :], qbd[g])

    def softmax(s, m, l):
        m_new = jnp.maximum(m, jnp.max(s, axis=0, keepdims=True))
        p = jnp.exp2(s - m_new)
        alpha = jnp.exp2(m - m_new)
        return m_new, alpha * l + jnp.sum(p, axis=0, keepdims=True), alpha, p.astype(BF16)

    for g in range(n_heads):
        s_scr[g, 0] = scores(g, 0)
        p_scr[g, 1] = jnp.zeros(p_scr.shape[2:], p_scr.dtype)
        acc_scr[g] = jnp.zeros(acc_scr.shape[1:], acc_scr.dtype)

    def stage(g, j, cur, oth, m, l):
        pv = _dot(vt_ref[0, g, jnp.maximum(j - 1, 0)], p_scr[g, oth])
        m, l, alpha, p = softmax(s_scr[g, cur], m, l)
        p_scr[g, cur] = p
        acc_scr[g] = alpha * (acc_scr[g] + pv)
        s_scr[g, oth] = scores(g, j + 1)
        return m, l

    def stage_pair(jj, carry):
        out = []
        for g in range(n_heads):
            m, l = stage(g, 2 * jj, 0, 1, *carry[g])
            out.append(stage(g, 2 * jj + 1, 1, 0, m, l))
        return tuple(out)

    init = ((jnp.full((1, 2 * tq), NEG_BIG, F32), jnp.zeros((1, 2 * tq), F32)),) * n_heads
    carry = lax.fori_loop(0, i, stage_pair, init)

    r = 2 * i
    kchunk = lax.broadcasted_iota(jnp.int32, (tk, 2 * tq), 0) // CHUNK
    qchunk = (lax.broadcasted_iota(jnp.int32, (tk, 2 * tq), 1) % tq) // CHUNK
    lam = lam_ref[0]
    for g in range(n_heads):
        m, l = carry[g]
        pv = _dot(vt_ref[0, g, jnp.maximum(r - 1, 0)], p_scr[g, 1])
        m, l, alpha, p_r = softmax(jnp.where(kchunk <= qchunk, s_scr[g, 0], NEG_BIG), m, l)
        acc = alpha * (acc_scr[g] + pv)
        s_last = scores(g, r + 1)
        pv = _dot(vt_ref[0, g, r], p_r)
        m, l, alpha, p_last = softmax(jnp.where(kchunk + tk // CHUNK <= qchunk, s_last, NEG_BIG), m, l)
        acc = alpha * (acc + pv) + _dot(vt_ref[0, g, r + 1], p_last)
        o = acc[:, :tq] / l[:, :tq] - lam * (acc[:, tq:] / l[:, tq:])
        on = o * lax.rsqrt(jnp.mean(o * o, axis=0, keepdims=True) + EPS) * sg_ref[...] * (1.0 - lam_init)
        o_ref[0, :, g * LANES:(g + 1) * LANES] = on.T.astype(o_ref.dtype)


def _flash_attention(lam, qt, kb, vt, sub_g_col, lam_init):
    n_batch, _, nq, _, tq = qt.shape
    nk, tk = vt.shape[2], vt.shape[4]
    seq = nq * tq
    assert tq == 2 * tk
    hg = FLASH_HEADS_PER_STEP
    return pl.pallas_call(
        functools.partial(_flash_kernel, lam_init=lam_init),
        grid=(n_batch, DIFF_HEADS // hg, nq),
        in_specs=[pl.BlockSpec(memory_space=pltpu.SMEM),
                  pl.BlockSpec((1, hg, 1, LANES, tq), lambda b, h, i: (b, h, i, 0, 0)),
                  pl.BlockSpec((1, hg, seq, LANES), lambda b, h, i: (b, h, 0, 0)),
                  pl.BlockSpec((1, hg, nk, LANES, tk), lambda b, h, i: (b, h, 0, 0, 0)),
                  pl.BlockSpec((LANES, 1), lambda b, h, i: (0, 0))],
        out_specs=pl.BlockSpec((1, tq, hg * LANES), lambda b, h, i: (b, i, h)),
        out_shape=jax.ShapeDtypeStruct((n_batch, seq, D_MODEL), BF16),
        scratch_shapes=[pltpu.VMEM((hg, 2, tk, 2 * tq), F32),
                        pltpu.VMEM((hg, 2, tk, 2 * tq), BF16),
                        pltpu.VMEM((hg, LANES, 2 * tq), F32)],
        compiler_params=_params(3), name="diff_flash",
    )(lam, qt, kb, vt, sub_g_col)


def _sample_attn_kernel(lam_ref, q_ref, kn_ref, vn_ref, ck_ref, cv_ref, sg_ref, o_ref, *, lam_init, past):
    q = q_ref[0]
    kn = kn_ref[0]
    vn = vn_ref[0]
    lam = lam_ref[0]
    lane = lax.broadcasted_iota(jnp.int32, (q.shape[0], LANES), 1)
    contract_last = (((1,), (1,)), ((), ()))
    outs = []
    for h in range(DIFF_HEADS):
        lanes = slice(h * LANES, (h + 1) * LANES)
        kc = ck_ref[0, pl.ds(h, past, stride=DIFF_HEADS), :].astype(BF16)
        vc = cv_ref[0, pl.ds(h, past, stride=DIFF_HEADS), :].astype(BF16)
        qh = q[:, lanes]
        knh = kn[:, lanes].astype(BF16)
        vnh = vn[:, lanes].astype(BF16)
        sub = []
        for c in range(2):
            keep = (lane < DIFF_HEAD_DIM) if c == 0 else (lane >= DIFF_HEAD_DIM)
            qc = jnp.where(keep, qh, 0.0).astype(BF16)
            s_old = lax.dot_general(qc, kc, contract_last, preferred_element_type=F32)
            s_new = lax.dot_general(qc, knh, contract_last, preferred_element_type=F32)
            m = jnp.maximum(jnp.max(s_old, axis=1, keepdims=True), jnp.max(s_new, axis=1, keepdims=True))
            p_old = jnp.exp(s_old - m)
            p_new = jnp.exp(s_new - m)
            l = jnp.sum(p_old, axis=1, keepdims=True) + jnp.sum(p_new, axis=1, keepdims=True)
            sub.append((_dot(p_old.astype(BF16), vc) + _dot(p_new.astype(BF16), vnh)) / l)
        o = sub[0] - lam * sub[1]
        outs.append(_rms(o, sg_ref[...]) * (1.0 - lam_init))
    o_ref[0] = jnp.concatenate(outs, axis=1).astype(o_ref.dtype)


def _sample_attention(lam, q, kn, vn, cache_k, cache_v, sub_g_row, lam_init):
    nb, rows, _ = q.shape
    past = cache_k.shape[1] // DIFF_HEADS
    new_spec = pl.BlockSpec((1, rows, D_MODEL), lambda b: (b, 0, 0))
    cache_spec = pl.BlockSpec((1, past * DIFF_HEADS, LANES), lambda b: (b, 0, 0))
    return pl.pallas_call(
        functools.partial(_sample_attn_kernel, lam_init=lam_init, past=past),
        grid=(nb,),
        in_specs=[pl.BlockSpec(memory_space=pltpu.SMEM), new_spec, new_spec, new_spec, cache_spec, cache_spec,
                  _const_spec((1, LANES))],
        out_specs=new_spec, out_shape=jax.ShapeDtypeStruct((nb, rows, D_MODEL), BF16),
        compiler_params=_params(), name="sample_attn",
    )(lam, q, kn, vn, cache_k, cache_v, sub_g_row)


def _attn_out_kernel(a_ref, x_ref, wout_ref, ffng_ref, wrhi_ref, wrlo_ref,
                     x1_ref, hn3_ref, route_ref, route_t_ref, counts_ref, base_ref):
    x1 = x_ref[...] + _dot(a_ref[...], wout_ref[...])
    x1_ref[...] = x1
    _route_epilogue(x1, ffng_ref[...], wrhi_ref[...], wrlo_ref[...], base_ref,
                    hn3_ref, route_ref, route_t_ref, counts_ref)


def _attn_out_layer(a, x, w_out, ffn_g, wr_hi, wr_lo):
    t = x.shape[0]
    tm = min(OUT_PROJ_TILE, t)
    row_spec = pl.BlockSpec((tm, D_MODEL), lambda i: (i, 0))
    return pl.pallas_call(
        _attn_out_kernel,
        grid=(t // tm,),
        in_specs=[row_spec, row_spec, _const_spec((D_MODEL, D_MODEL))] + _route_in_specs(),
        out_specs=(row_spec,) + _route_out_specs(tm),
        out_shape=(jax.ShapeDtypeStruct((t, D_MODEL), F32),) + _route_out_shapes(t),
        scratch_shapes=[pltpu.VMEM((1, ROUTE_LANES), F32)],
        compiler_params=_params(), name="attn_out",
    )(a, x, w_out, ffn_g, wr_hi, wr_lo)


def _load_tile_indices(dest_hbm, idx_smem, idx_sem):
    n = idx_smem.shape[0]
    start = pl.multiple_of(pl.program_id(0) * n, n)
    idx_copy = pltpu.make_async_copy(dest_hbm.at[pl.ds(start, n)], idx_smem, idx_sem)
    idx_copy.start()
    idx_copy.wait()


def _index_tile_len(n_indices):
    return -(-n_indices // INDEX_SLICE_WORDS) * INDEX_SLICE_WORDS


def _per_step_dest(dest, tokens_per_step):
    steps = dest.shape[1] // tokens_per_step
    return dest.reshape(2, steps, tokens_per_step).transpose(1, 0, 2).reshape(steps, 2 * tokens_per_step)


def _tiled_indices(per_step):
    n = per_step.shape[1]
    return jnp.pad(per_step, ((0, 0), (0, _index_tile_len(n) - n))).reshape(-1)


def _dispatch_kernel(idx_hbm, *refs, plan):
    n_streams = len(plan)
    hn_refs = refs[:n_streams]
    xs3_hbm, idx_smem, zero_buf, idx_sem, row_sem = refs[n_streams:]
    i = pl.program_id(0)
    zero_buf[...] = jnp.zeros_like(zero_buf)

    for hn_ref, (first, steps, td, n_fill, offset, length) in zip(hn_refs, plan):
        @pl.when(jnp.logical_and(i >= first, i < first + steps))
        def _(hn_ref=hn_ref, first=first, td=td, n_fill=n_fill, offset=offset, length=length):
            start = pl.multiple_of(offset + (i - first) * length, INDEX_SLICE_WORDS)
            idx_copy = pltpu.make_async_copy(idx_hbm.at[pl.ds(start, length)], idx_smem.at[pl.ds(0, length)], idx_sem)
            idx_copy.start()
            idx_copy.wait()

            def scatter(t, carry):
                src = hn_ref.at[pl.ds(pl.multiple_of(t * ROW_TILES, ROW_TILES), ROW_TILES)]
                pltpu.make_async_copy(src, xs3_hbm.at[idx_smem[t]], row_sem).start(priority=0)
                pltpu.make_async_copy(src, xs3_hbm.at[idx_smem[td + t]], row_sem).start(priority=1)
                return carry

            def fill(p, carry):
                pltpu.make_async_copy(zero_buf, xs3_hbm.at[idx_smem[2 * td + p]], row_sem).start()
                return carry

            lax.fori_loop(0, td, scatter, 0)
            lax.fori_loop(0, n_fill, fill, 0)
            done = xs3_hbm.at[pl.ds(0, 2 * td + n_fill)]
            pltpu.make_async_copy(done, done, row_sem).wait()


def _dispatch(dests, pad_rows, hn2s, n_rows):
    plan, tables, in_specs = [], [], [pl.BlockSpec(memory_space=pl.ANY)]
    first = offset = 0
    for k, (dest, hn2) in enumerate(zip(dests, hn2s)):
        t = hn2.shape[0] // ROW_TILES
        td = min(ROW_DMA_TILE, t)
        steps = t // td
        fill = pad_rows if k == len(dests) - 1 else pad_rows[:0]
        n_fill = fill.shape[0] // steps
        table = _tiled_indices(jnp.concatenate([_per_step_dest(dest, td), fill.reshape(steps, n_fill)], axis=1))
        length = table.shape[0] // steps
        plan.append((first, steps, td, n_fill, offset, length))
        tables.append(table)
        in_specs.append(pl.BlockSpec((td * ROW_TILES, LANES),
                                     lambda i, first=first, steps=steps: (jnp.clip(i - first, 0, steps - 1), 0)))
        first += steps
        offset += table.shape[0]
    return pl.pallas_call(
        functools.partial(_dispatch_kernel, plan=tuple(plan)),
        grid=(first,),
        in_specs=in_specs,
        out_specs=pl.BlockSpec(memory_space=pl.ANY),
        out_shape=jax.ShapeDtypeStruct((n_rows, ROW_TILES, LANES), F32),
        scratch_shapes=[pltpu.SMEM((max(p[5] for p in plan),), jnp.int32),
                        pltpu.VMEM((ROW_TILES, LANES), F32),
                        pltpu.SemaphoreType.DMA, pltpu.SemaphoreType.DMA],
        compiler_params=_params(), name="moe_dispatch",
    )(jnp.concatenate(tables), *hn2s)


def _expert_kernel(bexp_ref, nvalid_ref, xs_ref, wg_ref, wu_ref, wd_ref, yb_ref, wg_bf, wu_bf, wd_bf):
    blk = xs_ref.shape[0] // ROW_TILES
    i = pl.program_id(0)
    nvalid = nvalid_ref[i]

    @pl.when(jnp.logical_or(i == 0, bexp_ref[i] != bexp_ref[jnp.maximum(i - 1, 0)]))
    def _():
        wg_bf[...] = wg_ref[0, 0].astype(BF16)
        wu_bf[...] = wu_ref[0, 0].astype(BF16)
        wd_bf[...] = wd_ref[0, 0].astype(BF16)

    @pl.when(nvalid > 0)
    def _():
        x = jnp.concatenate([xs_ref[pl.ds(s, blk, stride=ROW_TILES), :] for s in range(ROW_TILES)], axis=1)
        row = lax.broadcasted_iota(jnp.int32, (blk, 1), 0)
        xb = jnp.where(row < nvalid, x, 0.0).astype(BF16)
        hg = _dot(xb, wg_bf[...])
        hu = _dot(xb, wu_bf[...])
        act = (hg * (1.0 / (1.0 + jnp.exp(-hg))) * hu).astype(BF16)
        y = _dot(act, wd_bf[...])
        for s in range(ROW_TILES):
            yb_ref[pl.ds(s, blk, stride=ROW_TILES), :] = y[:, s * LANES:(s + 1) * LANES]

    @pl.when(nvalid <= 0)
    def _():
        yb_ref[...] = jnp.zeros_like(yb_ref)


def _experts(block_expert, block_nvalid, xs2, layer, wg, wu, wd):
    n_blocks = block_expert.shape[0]
    blk = xs2.shape[0] // ROW_TILES // n_blocks
    rows_spec = pl.BlockSpec((blk * ROW_TILES, LANES), lambda i, be, nv: (i, 0))
    up_spec = pl.BlockSpec((1, 1, D_MODEL, EXPERT_HIDDEN), lambda i, be, nv: (layer, be[i], 0, 0))
    down_spec = pl.BlockSpec((1, 1, EXPERT_HIDDEN, D_MODEL), lambda i, be, nv: (layer, be[i], 0, 0))
    return pl.pallas_call(
        _expert_kernel,
        grid_spec=pltpu.PrefetchScalarGridSpec(
            num_scalar_prefetch=2, grid=(n_blocks,),
            in_specs=[rows_spec, up_spec, up_spec, down_spec],
            out_specs=rows_spec,
            scratch_shapes=[pltpu.VMEM((D_MODEL, EXPERT_HIDDEN), BF16), pltpu.VMEM((D_MODEL, EXPERT_HIDDEN), BF16),
                            pltpu.VMEM((EXPERT_HIDDEN, D_MODEL), BF16)]),
        out_shape=jax.ShapeDtypeStruct(xs2.shape, F32),
        compiler_params=_params(), name="moe_experts",
    )(block_expert, block_nvalid, xs2, wg, wu, wd)


def _combine_kernel(dest_hbm, x1_ref, route_ref, yb3_hbm, out_ref, idx_smem, buf0, buf1, idx_sem, row_sem):
    tc = x1_ref.shape[0]
    _load_tile_indices(dest_hbm, idx_smem, idx_sem)

    def body(t, carry):
        dst = pl.ds(pl.multiple_of(t * ROW_TILES, ROW_TILES), ROW_TILES)
        pltpu.make_async_copy(yb3_hbm.at[idx_smem[t]], buf0.at[dst], row_sem).start(priority=0)
        pltpu.make_async_copy(yb3_hbm.at[idx_smem[tc + t]], buf1.at[dst], row_sem).start(priority=1)
        return carry

    lax.fori_loop(0, tc, body, 0)
    pltpu.make_async_copy(buf0, buf0, row_sem).wait()
    pltpu.make_async_copy(buf1, buf1, row_sem).wait()
    g1 = route_ref[:, 2:3]
    g2 = route_ref[:, 3:4]
    for s in range(ROW_TILES):
        lanes = slice(s * LANES, (s + 1) * LANES)
        y1 = buf0[pl.ds(s, tc, stride=ROW_TILES), :]
        y2 = buf1[pl.ds(s, tc, stride=ROW_TILES), :]
        out_ref[:, lanes] = x1_ref[:, lanes] + (g1 * y1 + g2 * y2)


def _combine(dest, x1, route, yb3):
    t = x1.shape[0]
    tc = min(ROW_DMA_TILE, t)
    dest_flat = _tiled_indices(_per_step_dest(dest, tc))
    return pl.pallas_call(
        _combine_kernel,
        grid=(t // tc,),
        in_specs=[pl.BlockSpec(memory_space=pl.ANY),
                  pl.BlockSpec((tc, D_MODEL), lambda i: (i, 0)),
                  pl.BlockSpec((tc, ROUTE_COLS), lambda i: (i, 0)),
                  pl.BlockSpec(memory_space=pl.ANY)],
        out_specs=pl.BlockSpec((tc, D_MODEL), lambda i: (i, 0)),
        out_shape=jax.ShapeDtypeStruct((t, D_MODEL), F32),
        scratch_shapes=[pltpu.SMEM((_index_tile_len(2 * tc),), jnp.int32),
                        pltpu.VMEM((tc * ROW_TILES, LANES), F32), pltpu.VMEM((tc * ROW_TILES, LANES), F32),
                        pltpu.SemaphoreType.DMA, pltpu.SemaphoreType.DMA],
        compiler_params=_params(), name="moe_combine",
    )(dest_flat, x1, route, yb3)


def _count_le(sorted_ends, values):
    ends = sorted_ends.reshape((-1,) + (1,) * values.ndim)
    return jnp.sum((ends <= values[None]).astype(jnp.int32), axis=0)


def _lookup(table, idx):
    keys = jnp.arange(table.shape[0], dtype=jnp.int32).reshape((-1,) + (1,) * idx.ndim)
    return jnp.sum(jnp.where(idx[None] == keys, table.reshape(keys.shape), 0), axis=0)


def _moe_block_rows(t):
    mean_rows_per_expert = max(2 * t // N_EXPERTS, 1)
    return min(MOE_BLOCK_ROWS, max(BF16_TILE_ROWS, 1 << (mean_rows_per_expert.bit_length() - 1)))


def _moe_experts(streams, layer, wg, wu, wd):
    t = sum(s[0].shape[0] for s in streams)
    blk = _moe_block_rows(t)
    cnts = [s[4][0, :N_EXPERTS].astype(jnp.int32) for s in streams]
    cnt = sum(cnts)
    padded = (cnt + blk - 1) // blk * blk
    pad_end = jnp.cumsum(padded)
    pad_start = pad_end - padded
    dests, seen = [], jnp.zeros_like(cnt)
    for s, c in zip(streams, cnts):
        expert = s[3][0:2].astype(jnp.int32)
        rank = s[3][4:6].astype(jnp.int32)
        dests.append(_lookup(pad_start + seen, expert) + rank)
        seen = seen + c
    n_blocks = -(-2 * t // blk) + N_EXPERTS
    blk_start = jnp.arange(n_blocks, dtype=jnp.int32) * blk
    block_expert = jnp.minimum(_count_le(pad_end, blk_start), N_EXPERTS - 1)
    block_nvalid = jnp.clip(_lookup(pad_start + cnt, block_expert) - blk_start, 0, blk).astype(jnp.int32)
    n_rows = n_blocks * blk
    gap_start = jnp.concatenate([pad_start + cnt, pad_end[-1:]])
    gap_len = jnp.concatenate([padded - cnt, n_rows - pad_end[-1:]])
    gap_end = jnp.cumsum(gap_len)
    p = jnp.arange(n_rows - 2 * t, dtype=jnp.int32)
    pad_rows = (p + _lookup(gap_start - gap_end + gap_len, _count_le(gap_end, p))).astype(jnp.int32)
    xs3 = _dispatch(dests, pad_rows, [s[1] for s in streams], n_rows)
    yb2 = _experts(block_expert, block_nvalid, xs3.reshape(n_rows * ROW_TILES, LANES), layer, wg, wu, wd)
    return yb2.reshape(n_rows, ROW_TILES, LANES), dests


def _router_weights(w_group, w_router):
    w = jnp.concatenate([w_router, w_group,
                         jnp.zeros((D_MODEL, ROUTE_LANES - N_EXPERTS - MOE_GROUPS), F32)], axis=1)
    hi = lax.bitcast_convert_type(lax.bitcast_convert_type(w, jnp.uint32) & jnp.uint32(0xFFFF0000), F32)
    hi, lo = hi.astype(BF16), (w - hi).astype(BF16)
    return hi, jnp.concatenate([hi, lo], axis=1)


def _rope_tables(pos):
    half = ROT_DIM // 2
    inv_freq = jnp.power(ROPE_THETA, -jnp.arange(half, dtype=F32) * (2.0 / ROT_DIM))
    ang = pos.astype(F32)[:, None] * inv_freq[None, :]
    cos, sin = jnp.cos(ang), jnp.sin(ang)
    n = pos.shape[0]
    ones = jnp.ones((n, DIFF_HEAD_DIM - ROT_DIM), F32)
    zeros = jnp.zeros((n, DIFF_HEAD_DIM - ROT_DIM), F32)
    zh = jnp.zeros((n, half), F32)
    sub_cos = jnp.concatenate([cos, cos, ones], axis=1)
    sub_lo = jnp.concatenate([-sin, zh, zeros], axis=1)
    sub_hi = jnp.concatenate([zh, sin, zeros], axis=1)
    return tuple(jnp.concatenate([a, a], axis=1) for a in (sub_cos, sub_lo, sub_hi))


def _spatial_weights(w_s, b_s, lc):
    pos = jnp.arange(lc)
    mask = (pos[None, :] // CHUNK) <= (pos[:, None] // CHUNK)
    ws = jnp.where(mask[None], w_s[:, :lc, :lc], 0.0)
    reps = GMLP_CHUNK // lc
    eye = jnp.eye(reps, dtype=F32)
    ws_eff = jnp.einsum("ab,gij->gaibj", eye, ws).reshape(GMLP_GROUPS, GMLP_CHUNK, GMLP_CHUNK)
    b_rows = jnp.tile(b_s[:, :lc], (1, reps))
    b_exp = jnp.repeat(b_rows.T, GMLP_GROUP_DIM, axis=1)
    return ws_eff, b_exp


def kernel(x_prompt, x_sample, cache_attn_k, cache_attn_v, mix_norm, ffn_norm, gmlp_w_in, gmlp_v_norm, gmlp_w_s, gmlp_b_s, gmlp_w_out, attn_w_in, attn_q_norm, attn_k_norm, attn_lam_q1, attn_lam_k1, attn_lam_q2, attn_lam_k2, attn_sub_norm, attn_w_out, moe_w_group, moe_w_router, moe_w_gate, moe_w_up, moe_w_down):
    nb_p, seq, _ = x_prompt.shape
    nb_s, dec, _ = x_sample.shape
    past = cache_attn_k.shape[2]
    assert cache_attn_k.shape[0] == DEPTH // 2 == 1 and mix_norm.shape[0] == DEPTH
    xp = x_prompt.reshape(nb_p * seq, D_MODEL)
    xs = x_sample.reshape(nb_s * dec, D_MODEL)
    row = lambda a: a.reshape(1, -1)

    router0 = _router_weights(moe_w_group[0], moe_w_router[0])
    gm_tail = (row(ffn_norm[0]),) + router0
    experts0 = (0, moe_w_gate, moe_w_up, moe_w_down)
    ws_p, b_p = _spatial_weights(gmlp_w_s[0], gmlp_b_s[0], GMLP_CHUNK)
    ws_s, b_s = _spatial_weights(gmlp_w_s[0], gmlp_b_s[0], dec)
    *mixed_p, gv_p = _gmlp_layer(
        xp, nb_p, row(mix_norm[0]), gmlp_w_in[0].astype(BF16), row(gmlp_v_norm[0]), ws_p.astype(BF16), b_p,
        gmlp_w_out[0].astype(BF16), *gm_tail)
    *mixed_s, gv_s = _gmlp_layer(
        xs, 1, row(mix_norm[0]), gmlp_w_in[0], row(gmlp_v_norm[0]), ws_s, b_s, gmlp_w_out[0], *gm_tail)
    yb3, (dest_p, dest_s) = _moe_experts([mixed_p, mixed_s], *experts0)
    xp = _combine(dest_p, mixed_p[0], mixed_p[2], yb3)
    xs = _combine(dest_s, mixed_s[0], mixed_s[2], yb3)

    lam_init = 0.8 - 0.6 * math.exp(-0.3 * 1)
    lam = (jnp.exp(jnp.sum(attn_lam_q1[0] * attn_lam_k1[0])) - jnp.exp(jnp.sum(attn_lam_q2[0] * attn_lam_k2[0]))
           + lam_init).reshape(1).astype(F32)
    grp = jnp.arange(D_MODEL) // DIFF_HEAD_DIM
    group_ones = (grp[:, None] == grp[None, :]).astype(BF16)
    qk = (row(mix_norm[1]), attn_w_in[0].astype(BF16), group_ones,
          row(jnp.tile(attn_q_norm[0], 2 * DIFF_HEADS)), row(jnp.tile(attn_k_norm[0], 2 * DIFF_HEADS)))
    router1 = _router_weights(moe_w_group[1], moe_w_router[1])
    at_tail = (attn_w_out[0].astype(BF16), row(ffn_norm[1])) + router1
    experts1 = (1, moe_w_gate, moe_w_up, moe_w_down)

    kp, vp, kb, vt, qt = _qkv_layer(xp, nb_p, *qk, *_rope_tables(jnp.arange(seq)), prompt=True)
    ap = _flash_attention(lam, qt, kb, vt, attn_sub_norm[0].reshape(LANES, 1), lam_init)
    mixed_p = _attn_out_layer(ap.reshape(nb_p * seq, D_MODEL), xp, *at_tail)

    pos_s = jnp.tile(past + jnp.arange(dec), nb_s)
    qs, ks, vs = _qkv_layer(xs, nb_s, *qk, *_rope_tables(pos_s), prompt=False)
    shp = (nb_s, dec, D_MODEL)
    a_s = _sample_attention(lam, qs.reshape(shp), ks.reshape(shp), vs.reshape(shp),
                            cache_attn_k.reshape(nb_s, past * DIFF_HEADS, LANES),
                            cache_attn_v.reshape(nb_s, past * DIFF_HEADS, LANES),
                            row(attn_sub_norm[0]), lam_init)
    mixed_s = _attn_out_layer(a_s.reshape(nb_s * dec, D_MODEL), xs, *at_tail)
    yb3, (dest_p, dest_s) = _moe_experts([mixed_p, mixed_s], *experts1)
    xp = _combine(dest_p, mixed_p[0], mixed_p[2], yb3)
    xs = _combine(dest_s, mixed_s[0], mixed_s[2], yb3)

    hv = (DIFF_HEADS, DIFF_VALUE_DIM)
    return (xp.reshape(nb_p, seq, D_MODEL), xs.reshape(nb_s, dec, D_MODEL),
            gv_p[None], gv_s.reshape(1, nb_s, dec, GMLP_WIDTH),
            kp.reshape(1, nb_p, seq, *hv), vp.reshape(1, nb_p, seq, *hv),
            ks.reshape(1, nb_s, dec, *hv), vs.reshape(1, nb_s, dec, *hv))
```

```python
import functools
import math

import jax
import jax.numpy as jnp
from jax import lax
from jax.experimental import pallas as pl
from jax.experimental.pallas import tpu as pltpu

D_MODEL = 1024
DEPTH = 2
CHUNK = 64
GMLP_CHUNK = 128
GMLP_WIDTH = 2 * D_MODEL
GMLP_GROUPS = 8
GMLP_GROUP_DIM = GMLP_WIDTH // GMLP_GROUPS
DIFF_HEADS = 8
DIFF_HEAD_DIM = D_MODEL // (2 * DIFF_HEADS)
DIFF_VALUE_DIM = 2 * DIFF_HEAD_DIM
ROT_DIM = DIFF_HEAD_DIM // 4
ROPE_THETA = 500000.0
MOE_GROUPS = 4
MOE_EXPERTS_PER_GROUP = 8
N_EXPERTS = MOE_GROUPS * MOE_EXPERTS_PER_GROUP
EXPERT_HIDDEN = D_MODEL // 2
EPS = 1e-6

LANES = 128
SUBLANES = 8
BF16_TILE_ROWS = 2 * SUBLANES
ROW_TILES = D_MODEL // LANES
VMEM_LIMIT_BYTES = 56 * 1024 * 1024

ROUTE_LANES = LANES
GROUP_LANE0 = N_EXPERTS
ROUTE_COLS = 8
NEG_BIG = -1e30

MIX_TILE = 512
OUT_PROJ_TILE = 256
ATT_K_TILE = 256
ATT_Q_TILE = 2 * ATT_K_TILE
VT_ROWS = LANES + BF16_TILE_ROWS
FLASH_HEADS_PER_STEP = 4
MOE_BLOCK_ROWS = 256
ROW_DMA_TILE = 512
INDEX_SLICE_WORDS = 1024

F32 = jnp.float32
BF16 = jnp.bfloat16


def _params(n_axes=1):
    return pltpu.CompilerParams(dimension_semantics=("arbitrary",) * n_axes,
                                vmem_limit_bytes=VMEM_LIMIT_BYTES)


def _rms(x, g):
    return x * lax.rsqrt(jnp.mean(x * x, axis=-1, keepdims=True) + EPS) * g


def _dot(a, b):
    return jnp.dot(a, b, preferred_element_type=F32)


def _dot_f32(a, b):
    return jnp.dot(a, b, preferred_element_type=F32, precision=lax.Precision.HIGHEST)


def _const_spec(shape):
    return pl.BlockSpec(shape, lambda *_: (0,) * len(shape), pipeline_mode=pl.Buffered(1))


def _route_epilogue(x1, ffn_g, wr_hi, wr_hi_lo, base_ref, hn3_ref, route_ref, route_t_ref, counts_ref):
    tm = x1.shape[0]

    @pl.when(pl.program_id(0) == 0)
    def _():
        base_ref[...] = jnp.zeros_like(base_ref)

    hn = _rms(x1, ffn_g)
    for s in range(ROW_TILES):
        hn3_ref[pl.ds(s, tm, stride=ROW_TILES), :] = hn[:, s * LANES:(s + 1) * LANES]

    h_hi = hn.astype(BF16)
    h_lo = (hn - h_hi.astype(F32)).astype(BF16)
    both = _dot(h_hi, wr_hi_lo)
    logit = both[:, :ROUTE_LANES] + both[:, ROUTE_LANES:] + _dot(h_lo, wr_hi)

    lane = lax.broadcasted_iota(jnp.int32, (tm, ROUTE_LANES), 1)
    far = jnp.int32(4 * ROUTE_LANES)
    lg = jnp.where(lane >= GROUP_LANE0, jnp.where(lane < GROUP_LANE0 + MOE_GROUPS, logit, NEG_BIG), NEG_BIG)
    mg = jnp.max(lg, axis=1, keepdims=True)
    g_lane = jnp.min(jnp.where(lg == mg, lane, far), axis=1, keepdims=True)
    g_sel = g_lane - GROUP_LANE0
    p_g = 1.0 / jnp.sum(jnp.exp(lg - mg), axis=1, keepdims=True)

    lo_lane = g_sel * MOE_EXPERTS_PER_GROUP
    le = jnp.where(lane >= lo_lane, jnp.where(lane < lo_lane + MOE_EXPERTS_PER_GROUP, logit, NEG_BIG), NEG_BIG)
    m1 = jnp.max(le, axis=1, keepdims=True)
    j1 = jnp.min(jnp.where(le == m1, lane, far), axis=1, keepdims=True)
    le2 = jnp.where(lane == j1, NEG_BIG, le)
    m2 = jnp.max(le2, axis=1, keepdims=True)
    j2 = jnp.min(jnp.where(le2 == m2, lane, far), axis=1, keepdims=True)
    r = jnp.exp(m2 - m1)
    gate1 = p_g / (1.0 + r)
    gate2 = p_g * r / (1.0 + r)

    hit1 = lane == j1
    hit2 = lane == j2
    onehot = jnp.where(hit1, 1.0, jnp.where(hit2, 1.0, 0.0))
    row = lax.broadcasted_iota(jnp.int32, (tm, tm), 0)
    col = lax.broadcasted_iota(jnp.int32, (tm, tm), 1)
    earlier = jnp.where(row > col, 1.0, 0.0).astype(BF16)
    prefix = _dot(earlier, onehot.astype(BF16)) + base_ref[...]
    rank1 = jnp.sum(jnp.where(hit1, prefix, 0.0), axis=1, keepdims=True)
    rank2 = jnp.sum(jnp.where(hit2, prefix, 0.0), axis=1, keepdims=True)
    base_new = base_ref[...] + jnp.sum(onehot, axis=0, keepdims=True)
    base_ref[...] = base_new
    counts_ref[...] = base_new

    rec = jnp.where(lane == 0, j1.astype(F32),
          jnp.where(lane == 1, j2.astype(F32),
          jnp.where(lane == 2, gate1,
          jnp.where(lane == 3, gate2,
          jnp.where(lane == 4, rank1,
          jnp.where(lane == 5, rank2, 0.0))))))
    route_ref[...] = rec[:, :ROUTE_COLS]
    route_t_ref[...] = rec.T[:ROUTE_COLS]


def _route_out_shapes(t):
    return (jax.ShapeDtypeStruct((t * ROW_TILES, LANES), F32),
            jax.ShapeDtypeStruct((t, ROUTE_COLS), F32),
            jax.ShapeDtypeStruct((ROUTE_COLS, t), F32),
            jax.ShapeDtypeStruct((1, ROUTE_LANES), F32))


def _route_out_specs(tm):
    return (pl.BlockSpec((tm * ROW_TILES, LANES), lambda i: (i, 0)),
            pl.BlockSpec((tm, ROUTE_COLS), lambda i: (i, 0)),
            pl.BlockSpec((ROUTE_COLS, tm), lambda i: (0, i)),
            pl.BlockSpec((1, ROUTE_LANES), lambda i: (0, 0)))


def _route_in_specs():
    return [_const_spec((1, D_MODEL)), _const_spec((D_MODEL, ROUTE_LANES)), _const_spec((D_MODEL, 2 * ROUTE_LANES))]


def _gelu_tanh(x):
    cdf = 0.5 * (1.0 + jnp.tanh(math.sqrt(2.0 / math.pi) * (x + 0.044715 * (x * x * x))))
    return x * cdf


def _gmlp_kernel(x_ref, mixg_ref, win_ref, vng_ref, ws_ref, bexp_ref, wout_ref,
                 ffng_ref, wrhi_ref, wrlo_ref,
                 x1_ref, hn3_ref, route_ref, route_t_ref, counts_ref, vlast_ref, base_ref, *, tiles_per_batch):
    tm = x_ref.shape[0]
    mm = win_ref.dtype
    dot = _dot if mm == BF16 else _dot_f32
    x = x_ref[...]
    h = _rms(x, mixg_ref[...]).astype(mm)
    z = _gelu_tanh(dot(h, win_ref[...]))
    u = z[:, :GMLP_WIDTH]
    vn = _rms(z[:, GMLP_WIDTH:], vng_ref[...])
    vb = vn.astype(mm)
    gated = []
    for c in range(tm // GMLP_CHUNK):
        rows = slice(c * GMLP_CHUNK, (c + 1) * GMLP_CHUNK)
        s = jnp.concatenate(
            [dot(ws_ref[g], vb[rows, g * GMLP_GROUP_DIM:(g + 1) * GMLP_GROUP_DIM])
             for g in range(GMLP_GROUPS)], axis=1) + bexp_ref[...]
        gated.append((u[rows] * s).astype(mm))
    y = dot(jnp.concatenate(gated, axis=0), wout_ref[...])
    x1 = x + y
    x1_ref[...] = x1

    @pl.when(pl.program_id(0) % tiles_per_batch == tiles_per_batch - 1)
    def _():
        vlast_ref[0] = vn[tm - GMLP_CHUNK:]

    _route_epilogue(x1, ffng_ref[...], wrhi_ref[...], wrlo_ref[...], base_ref,
                    hn3_ref, route_ref, route_t_ref, counts_ref)


def _gmlp_layer(x, n_batch, mix_g, w_in, vn_g, ws_eff, b_exp, w_out, ffn_g, wr_hi, wr_lo):
    t = x.shape[0]
    tm = min(MIX_TILE, t)
    tiles_per_batch = t // n_batch // tm
    gw = GMLP_WIDTH
    kern = functools.partial(_gmlp_kernel, tiles_per_batch=tiles_per_batch)
    return pl.pallas_call(
        kern,
        grid=(t // tm,),
        in_specs=[pl.BlockSpec((tm, D_MODEL), lambda i: (i, 0)),
                  _const_spec((1, D_MODEL)), _const_spec((D_MODEL, 2 * gw)), _const_spec((1, gw)),
                  _const_spec((GMLP_GROUPS, GMLP_CHUNK, GMLP_CHUNK)), _const_spec((GMLP_CHUNK, gw)),
                  _const_spec((gw, D_MODEL))] + _route_in_specs(),
        out_specs=(pl.BlockSpec((tm, D_MODEL), lambda i: (i, 0)),) + _route_out_specs(tm)
                  + (pl.BlockSpec((1, GMLP_CHUNK, gw), lambda i: (i // tiles_per_batch, 0, 0)),),
        out_shape=(jax.ShapeDtypeStruct((t, D_MODEL), F32),) + _route_out_shapes(t)
                  + (jax.ShapeDtypeStruct((n_batch, GMLP_CHUNK, gw), F32),),
        scratch_shapes=[pltpu.VMEM((1, ROUTE_LANES), F32)],
        compiler_params=_params(),
        name="gmlp_mixer",
    )(x, mix_g, w_in, vn_g, ws_eff, b_exp, w_out, ffn_g, wr_hi, wr_lo)


def _qk_norm_rope(t, group_ones, gain, cos, sin_lo, sin_hi):
    ms = _dot((t * t).astype(BF16), group_ones) * (1.0 / DIFF_HEAD_DIM)
    tn = t * lax.rsqrt(ms + EPS) * gain
    heads = []
    for h in range(DIFF_HEADS):
        th = tn[:, h * LANES:(h + 1) * LANES]
        heads.append(th * cos + pltpu.roll(th, LANES - ROT_DIM // 2, 1) * sin_lo
                     + pltpu.roll(th, ROT_DIM // 2, 1) * sin_hi)
    return jnp.concatenate(heads, axis=1)


def _qkv_kernel(x_ref, mixg_ref, win_ref, ones_ref, qg_ref, kg_ref, cos_ref, slo_ref, shi_ref, *outs, prompt):
    tm = x_ref.shape[0]
    h = _rms(x_ref[...], mixg_ref[...]).astype(BF16)
    qkv = _dot(h, win_ref[...])
    rope = (cos_ref[...], slo_ref[...], shi_ref[...])
    q_scale = DIFF_HEAD_DIM ** -0.5 * (math.log2(math.e) if prompt else 1.0)
    q = _qk_norm_rope(qkv[:, :D_MODEL], ones_ref[...], qg_ref[...], *rope) * q_scale
    k = _qk_norm_rope(qkv[:, D_MODEL:2 * D_MODEL], ones_ref[...], kg_ref[...], *rope)
    v = qkv[:, 2 * D_MODEL:]
    if not prompt:
        q_ref, kout_ref, vout_ref = outs
        q_ref[...] = q
        kout_ref[...] = k
        vout_ref[...] = v
        return
    kout_ref, vout_ref, kb_ref, vt_ref, qt_ref = outs
    kout_ref[...] = k
    vout_ref[...] = v
    pad_row = lax.broadcasted_iota(jnp.int32, (VT_ROWS - LANES, ATT_K_TILE), 0)
    ones_row = jnp.where(pad_row == 0, 1.0, 0.0).astype(BF16)
    for hd in range(DIFF_HEADS):
        lanes = slice(hd * LANES, (hd + 1) * LANES)
        kb_ref[0, hd] = k[:, lanes].astype(BF16)
        for c in range(tm // ATT_K_TILE):
            vt_ref[0, hd, c, :LANES] = v[c * ATT_K_TILE:(c + 1) * ATT_K_TILE, lanes].T.astype(BF16)
            vt_ref[0, hd, c, LANES:] = ones_row
        for c in range(tm // ATT_Q_TILE):
            qt_ref[0, hd, c] = q[c * ATT_Q_TILE:(c + 1) * ATT_Q_TILE, lanes].T.astype(BF16)


def _qkv_layer(x, n_batch, mix_g, w_in, group_ones, q_g, k_g, cos, sin_lo, sin_hi, prompt):
    t = x.shape[0]
    tm = min(MIX_TILE, t)
    seq = t // n_batch
    tpb = seq // tm if prompt else 1
    n_tab = cos.shape[0] // tm
    tab_spec = pl.BlockSpec((tm, LANES), lambda i: (i % n_tab, 0))
    row_spec = pl.BlockSpec((tm, D_MODEL), lambda i: (i, 0))
    in_specs = [row_spec, _const_spec((1, D_MODEL)), _const_spec((D_MODEL, 3 * D_MODEL)),
                _const_spec((D_MODEL, D_MODEL)), _const_spec((1, D_MODEL)), _const_spec((1, D_MODEL)),
                tab_spec, tab_spec, tab_spec]
    row_shape = jax.ShapeDtypeStruct((t, D_MODEL), F32)
    if prompt:
        tq, tk = ATT_Q_TILE, ATT_K_TILE
        tiled = lambda rows, tile: pl.BlockSpec((1, DIFF_HEADS, tm // tile, rows, tile),
                                                lambda i: (i // tpb, 0, i % tpb, 0, 0))
        out_specs = (row_spec, row_spec,
                     pl.BlockSpec((1, DIFF_HEADS, tm, LANES), lambda i: (i // tpb, 0, i % tpb, 0)),
                     tiled(VT_ROWS, tk), tiled(LANES, tq))
        out_shape = (row_shape, row_shape,
                     jax.ShapeDtypeStruct((n_batch, DIFF_HEADS, seq, LANES), BF16),
                     jax.ShapeDtypeStruct((n_batch, DIFF_HEADS, seq // tk, VT_ROWS, tk), BF16),
                     jax.ShapeDtypeStruct((n_batch, DIFF_HEADS, seq // tq, LANES, tq), BF16))
    else:
        out_specs = (row_spec, row_spec, row_spec)
        out_shape = (row_shape, row_shape, row_shape)
    return pl.pallas_call(
        functools.partial(_qkv_kernel, prompt=prompt),
        grid=(t // tm,), in_specs=in_specs, out_specs=out_specs, out_shape=out_shape,
        compiler_params=_params(), name="qkv_prompt" if prompt else "qkv_sample",
    )(x, mix_g, w_in, group_ones, q_g, k_g, cos, sin_lo, sin_hi)


def _flash_kernel(lam_ref, qt_ref, k_ref, vt_ref, sg_ref, o_ref, s_scr, p_scr, acc_scr, *, lam_init):
    tq, tk = ATT_Q_TILE, ATT_K_TILE
    n_heads = qt_ref.shape[1]
    i = pl.program_id(2)
    frow = lax.broadcasted_iota(jnp.int32, (LANES, tq), 0)
    qbd = []
    for g in range(n_heads):
        qt = qt_ref[0, g, 0]
        zero = jnp.zeros_like(qt)
        qbd.append(jnp.concatenate([jnp.where(frow < DIFF_HEAD_DIM, qt, zero),
                                    jnp.where(frow >= DIFF_HEAD_DIM, qt, zero)], axis=1))

    def scores(g, j):
        return _dot(k_ref[0, g, pl.ds(pl.multiple_of(j * tk, tk), tk), :], qbd[g])

    def softmax(s, m):
        m_new = jnp.maximum(m, jnp.max(s, axis=0, keepdims=True))
        return m_new, jnp.exp2(m - m_new), jnp.exp2(s - m_new).astype(BF16)

    for g in range(n_heads):
        s_scr[g, 0] = scores(g, 0)
        p_scr[g, 1] = jnp.zeros(p_scr.shape[2:], p_scr.dtype)
        acc_scr[g] = jnp.zeros(acc_scr.shape[1:], acc_scr.dtype)

    def stage(g, j, cur, oth, m):
        pv = _dot(vt_ref[0, g, jnp.maximum(j - 1, 0)], p_scr[g, oth])
        m, alpha, p = softmax(s_scr[g, cur], m)
        p_scr[g, cur] = p
        acc_scr[g] = alpha * (acc_scr[g] + pv)
        s_scr[g, oth] = scores(g, j + 1)
        return m

    def stage_pair(jj, carry):
        return tuple(stage(g, 2 * jj + 1, 1, 0, stage(g, 2 * jj, 0, 1, carry[g])) for g in range(n_heads))

    init = (jnp.full((1, 2 * tq), NEG_BIG, F32),) * n_heads
    carry = lax.fori_loop(0, i, stage_pair, init)

    r = 2 * i
    kchunk = lax.broadcasted_iota(jnp.int32, (tk, 2 * tq), 0) // CHUNK
    qchunk = (lax.broadcasted_iota(jnp.int32, (tk, 2 * tq), 1) % tq) // CHUNK
    lam = lam_ref[0]
    for g in range(n_heads):
        m = carry[g]
        pv = _dot(vt_ref[0, g, jnp.maximum(r - 1, 0)], p_scr[g, 1])
        m, alpha, p_r = softmax(jnp.where(kchunk <= qchunk, s_scr[g, 0], NEG_BIG), m)
        acc = alpha * (acc_scr[g] + pv)
        s_last = scores(g, r + 1)
        pv = _dot(vt_ref[0, g, r], p_r)
        m, alpha, p_last = softmax(jnp.where(kchunk + tk // CHUNK <= qchunk, s_last, NEG_BIG), m)
        acc = alpha * (acc + pv) + _dot(vt_ref[0, g, r + 1], p_last)
        l = acc[LANES:LANES + 1]
        acc = acc[:LANES]
        o = acc[:, :tq] / l[:, :tq] - lam * (acc[:, tq:] / l[:, tq:])
        on = o * lax.rsqrt(jnp.mean(o * o, axis=0, keepdims=True) + EPS) * sg_ref[...] * (1.0 - lam_init)
        o_ref[0, :, g * LANES:(g + 1) * LANES] = on.T.astype(o_ref.dtype)


def _flash_attention(lam, qt, kb, vt, sub_g_col, lam_init):
    n_batch, _, nq, _, tq = qt.shape
    nk, tk = vt.shape[2], vt.shape[4]
    seq = nq * tq
    assert tq == 2 * tk
    hg = FLASH_HEADS_PER_STEP
    return pl.pallas_call(
        functools.partial(_flash_kernel, lam_init=lam_init),
        grid=(n_batch, DIFF_HEADS // hg, nq),
        in_specs=[pl.BlockSpec(memory_space=pltpu.SMEM),
                  pl.BlockSpec((1, hg, 1, LANES, tq), lambda b, h, i: (b, h, i, 0, 0)),
                  pl.BlockSpec((1, hg, seq, LANES), lambda b, h, i: (b, h, 0, 0)),
                  pl.BlockSpec((1, hg, nk, VT_ROWS, tk), lambda b, h, i: (b, h, 0, 0, 0)),
                  pl.BlockSpec((LANES, 1), lambda b, h, i: (0, 0))],
        out_specs=pl.BlockSpec((1, tq, hg * LANES), lambda b, h, i: (b, i, h)),
        out_shape=jax.ShapeDtypeStruct((n_batch, seq, D_MODEL), BF16),
        scratch_shapes=[pltpu.VMEM((hg, 2, tk, 2 * tq), F32),
                        pltpu.VMEM((hg, 2, tk, 2 * tq), BF16),
                        pltpu.VMEM((hg, VT_ROWS, 2 * tq), F32)],
        compiler_params=_params(3), name="diff_flash",
    )(lam, qt, kb, vt, sub_g_col)


def _sample_attn_kernel(lam_ref, q_ref, kn_ref, vn_ref, ck_ref, cv_ref, sg_ref, o_ref, *, lam_init, past):
    q = q_ref[0]
    kn = kn_ref[0]
    vn = vn_ref[0]
    lam = lam_ref[0]
    lane = lax.broadcasted_iota(jnp.int32, (q.shape[0], LANES), 1)
    contract_last = (((1,), (1,)), ((), ()))
    outs = []
    for h in range(DIFF_HEADS):
        lanes = slice(h * LANES, (h + 1) * LANES)
        kc = ck_ref[0, pl.ds(h, past, stride=DIFF_HEADS), :].astype(BF16)
        vc = cv_ref[0, pl.ds(h, past, stride=DIFF_HEADS), :].astype(BF16)
        qh = q[:, lanes]
        knh = kn[:, lanes].astype(BF16)
        vnh = vn[:, lanes].astype(BF16)
        sub = []
        for c in range(2):
            keep = (lane < DIFF_HEAD_DIM) if c == 0 else (lane >= DIFF_HEAD_DIM)
            qc = jnp.where(keep, qh, 0.0).astype(BF16)
            s_old = lax.dot_general(qc, kc, contract_last, preferred_element_type=F32)
            s_new = lax.dot_general(qc, knh, contract_last, preferred_element_type=F32)
            m = jnp.maximum(jnp.max(s_old, axis=1, keepdims=True), jnp.max(s_new, axis=1, keepdims=True))
            p_old = jnp.exp(s_old - m)
            p_new = jnp.exp(s_new - m)
            l = jnp.sum(p_old, axis=1, keepdims=True) + jnp.sum(p_new, axis=1, keepdims=True)
            sub.append((_dot(p_old.astype(BF16), vc) + _dot(p_new.astype(BF16), vnh)) / l)
        o = sub[0] - lam * sub[1]
        outs.append(_rms(o, sg_ref[...]) * (1.0 - lam_init))
    o_ref[0] = jnp.concatenate(outs, axis=1).astype(o_ref.dtype)


def _sample_attention(lam, q, kn, vn, cache_k, cache_v, sub_g_row, lam_init):
    nb, rows, _ = q.shape
    past = cache_k.shape[1] // DIFF_HEADS
    new_spec = pl.BlockSpec((1, rows, D_MODEL), lambda b: (b, 0, 0))
    cache_spec = pl.BlockSpec((1, past * DIFF_HEADS, LANES), lambda b: (b, 0, 0))
    return pl.pallas_call(
        functools.partial(_sample_attn_kernel, lam_init=lam_init, past=past),
        grid=(nb,),
        in_specs=[pl.BlockSpec(memory_space=pltpu.SMEM), new_spec, new_spec, new_spec, cache_spec, cache_spec,
                  _const_spec((1, LANES))],
        out_specs=new_spec, out_shape=jax.ShapeDtypeStruct((nb, rows, D_MODEL), BF16),
        compiler_params=_params(), name="sample_attn",
    )(lam, q, kn, vn, cache_k, cache_v, sub_g_row)


def _attn_out_kernel(a_ref, x_ref, wout_ref, ffng_ref, wrhi_ref, wrlo_ref,
                     x1_ref, hn3_ref, route_ref, route_t_ref, counts_ref, base_ref):
    x1 = x_ref[...] + _dot(a_ref[...], wout_ref[...])
    x1_ref[...] = x1
    _route_epilogue(x1, ffng_ref[...], wrhi_ref[...], wrlo_ref[...], base_ref,
                    hn3_ref, route_ref, route_t_ref, counts_ref)


def _attn_out_layer(a, x, w_out, ffn_g, wr_hi, wr_lo):
    t = x.shape[0]
    tm = min(OUT_PROJ_TILE, t)
    row_spec = pl.BlockSpec((tm, D_MODEL), lambda i: (i, 0))
    return pl.pallas_call(
        _attn_out_kernel,
        grid=(t // tm,),
        in_specs=[row_spec, row_spec, _const_spec((D_MODEL, D_MODEL))] + _route_in_specs(),
        out_specs=(row_spec,) + _route_out_specs(tm),
        out_shape=(jax.ShapeDtypeStruct((t, D_MODEL), F32),) + _route_out_shapes(t),
        scratch_shapes=[pltpu.VMEM((1, ROUTE_LANES), F32)],
        compiler_params=_params(), name="attn_out",
    )(a, x, w_out, ffn_g, wr_hi, wr_lo)


def _load_tile_indices(dest_hbm, idx_smem, idx_sem):
    n = idx_smem.shape[0]
    start = pl.multiple_of(pl.program_id(0) * n, n)
    idx_copy = pltpu.make_async_copy(dest_hbm.at[pl.ds(start, n)], idx_smem, idx_sem)
    idx_copy.start()
    idx_copy.wait()


def _index_tile_len(n_indices):
    return -(-n_indices // INDEX_SLICE_WORDS) * INDEX_SLICE_WORDS


def _per_step_dest(dest, tokens_per_step):
    steps = dest.shape[1] // tokens_per_step
    return dest.reshape(2, steps, tokens_per_step).transpose(1, 0, 2).reshape(steps, 2 * tokens_per_step)


def _tiled_indices(per_step):
    n = per_step.shape[1]
    return jnp.pad(per_step, ((0, 0), (0, _index_tile_len(n) - n))).reshape(-1)


def _dispatch_kernel(idx_hbm, *refs, plan):
    n_streams = len(plan)
    hn_refs = refs[:n_streams]
    xs3_hbm, idx_smem, zero_buf, idx_sem, row_sem = refs[n_streams:]
    i = pl.program_id(0)
    zero_buf[...] = jnp.zeros_like(zero_buf)

    for hn_ref, (first, steps, td, n_fill, offset, length) in zip(hn_refs, plan):
        @pl.when(jnp.logical_and(i >= first, i < first + steps))
        def _(hn_ref=hn_ref, first=first, td=td, n_fill=n_fill, offset=offset, length=length):
            start = pl.multiple_of(offset + (i - first) * length, INDEX_SLICE_WORDS)
            idx_copy = pltpu.make_async_copy(idx_hbm.at[pl.ds(start, length)], idx_smem.at[pl.ds(0, length)], idx_sem)
            idx_copy.start()
            idx_copy.wait()

            def scatter(t, carry):
                src = hn_ref.at[pl.ds(pl.multiple_of(t * ROW_TILES, ROW_TILES), ROW_TILES)]
                pltpu.make_async_copy(src, xs3_hbm.at[idx_smem[t]], row_sem).start(priority=0)
                pltpu.make_async_copy(src, xs3_hbm.at[idx_smem[td + t]], row_sem).start(priority=1)
                return carry

            def fill(p, carry):
                pltpu.make_async_copy(zero_buf, xs3_hbm.at[idx_smem[2 * td + p]], row_sem).start()
                return carry

            lax.fori_loop(0, td, scatter, 0)
            lax.fori_loop(0, n_fill, fill, 0)
            done = xs3_hbm.at[pl.ds(0, 2 * td + n_fill)]
            pltpu.make_async_copy(done, done, row_sem).wait()


def _dispatch(dests, pad_rows, hn2s, n_rows):
    plan, tables, in_specs = [], [], [pl.BlockSpec(memory_space=pl.ANY)]
    first = offset = 0
    for k, (dest, hn2) in enumerate(zip(dests, hn2s)):
        t = hn2.shape[0] // ROW_TILES
        td = min(ROW_DMA_TILE, t)
        steps = t // td
        fill = pad_rows if k == len(dests) - 1 else pad_rows[:0]
        n_fill = fill.shape[0] // steps
        table = _tiled_indices(jnp.concatenate([_per_step_dest(dest, td), fill.reshape(steps, n_fill)], axis=1))
        length = table.shape[0] // steps
        plan.append((first, steps, td, n_fill, offset, length))
        tables.append(table)
        in_specs.append(pl.BlockSpec((td * ROW_TILES, LANES),
                                     lambda i, first=first, steps=steps: (jnp.clip(i - first, 0, steps - 1), 0)))
        first += steps
        offset += table.shape[0]
    return pl.pallas_call(
        functools.partial(_dispatch_kernel, plan=tuple(plan)),
        grid=(first,),
        in_specs=in_specs,
        out_specs=pl.BlockSpec(memory_space=pl.ANY),
        out_shape=jax.ShapeDtypeStruct((n_rows, ROW_TILES, LANES), F32),
        scratch_shapes=[pltpu.SMEM((max(p[5] for p in plan),), jnp.int32),
                        pltpu.VMEM((ROW_TILES, LANES), F32),
                        pltpu.SemaphoreType.DMA, pltpu.SemaphoreType.DMA],
        compiler_params=_params(), name="moe_dispatch",
    )(jnp.concatenate(tables), *hn2s)


def _expert_kernel(bexp_ref, nvalid_ref, xs_ref, wg_ref, wu_ref, wd_ref, yb_ref, wg_bf, wu_bf, wd_bf):
    blk = xs_ref.shape[0] // ROW_TILES
    i = pl.program_id(0)
    nvalid = nvalid_ref[i]

    @pl.when(jnp.logical_or(i == 0, bexp_ref[i] != bexp_ref[jnp.maximum(i - 1, 0)]))
    def _():
        wg_bf[...] = wg_ref[0, 0].astype(BF16)
        wu_bf[...] = wu_ref[0, 0].astype(BF16)
        wd_bf[...] = wd_ref[0, 0].astype(BF16)

    @pl.when(nvalid > 0)
    def _():
        x = jnp.concatenate([xs_ref[pl.ds(s, blk, stride=ROW_TILES), :] for s in range(ROW_TILES)], axis=1)
        row = lax.broadcasted_iota(jnp.int32, (blk, 1), 0)
        xb = jnp.where(row < nvalid, x, 0.0).astype(BF16)
        hg = _dot(xb, wg_bf[...])
        hu = _dot(xb, wu_bf[...])
        act = (hg * (1.0 / (1.0 + jnp.exp(-hg))) * hu).astype(BF16)
        y = _dot(act, wd_bf[...])
        for s in range(ROW_TILES):
            yb_ref[pl.ds(s, blk, stride=ROW_TILES), :] = y[:, s * LANES:(s + 1) * LANES]

    @pl.when(nvalid <= 0)
    def _():
        yb_ref[...] = jnp.zeros_like(yb_ref)


def _experts(block_expert, block_nvalid, xs2, layer, wg, wu, wd):
    n_blocks = block_expert.shape[0]
    blk = xs2.shape[0] // ROW_TILES // n_blocks
    rows_spec = pl.BlockSpec((blk * ROW_TILES, LANES), lambda i, be, nv: (i, 0))
    up_spec = pl.BlockSpec((1, 1, D_MODEL, EXPERT_HIDDEN), lambda i, be, nv: (layer, be[i], 0, 0))
    down_spec = pl.BlockSpec((1, 1, EXPERT_HIDDEN, D_MODEL), lambda i, be, nv: (layer, be[i], 0, 0))
    return pl.pallas_call(
        _expert_kernel,
        grid_spec=pltpu.PrefetchScalarGridSpec(
            num_scalar_prefetch=2, grid=(n_blocks,),
            in_specs=[rows_spec, up_spec, up_spec, down_spec],
            out_specs=rows_spec,
            scratch_shapes=[pltpu.VMEM((D_MODEL, EXPERT_HIDDEN), BF16), pltpu.VMEM((D_MODEL, EXPERT_HIDDEN), BF16),
                            pltpu.VMEM((EXPERT_HIDDEN, D_MODEL), BF16)]),
        out_shape=jax.ShapeDtypeStruct(xs2.shape, F32),
        compiler_params=_params(), name="moe_experts",
    )(block_expert, block_nvalid, xs2, wg, wu, wd)


def _combine_kernel(dest_hbm, x1_ref, route_ref, yb3_hbm, out_ref, idx_smem, buf0, buf1, idx_sem, row_sem):
    tc = x1_ref.shape[0]
    _load_tile_indices(dest_hbm, idx_smem, idx_sem)

    def body(t, carry):
        dst = pl.ds(pl.multiple_of(t * ROW_TILES, ROW_TILES), ROW_TILES)
        pltpu.make_async_copy(yb3_hbm.at[idx_smem[t]], buf0.at[dst], row_sem).start(priority=0)
        pltpu.make_async_copy(yb3_hbm.at[idx_smem[tc + t]], buf1.at[dst], row_sem).start(priority=1)
        return carry

    lax.fori_loop(0, tc, body, 0)
    pltpu.make_async_copy(buf0, buf0, row_sem).wait()
    pltpu.make_async_copy(buf1, buf1, row_sem).wait()
    g1 = route_ref[:, 2:3]
    g2 = route_ref[:, 3:4]
    for s in range(ROW_TILES):
        lanes = slice(s * LANES, (s + 1) * LANES)
        y1 = buf0[pl.ds(s, tc, stride=ROW_TILES), :]
        y2 = buf1[pl.ds(s, tc, stride=ROW_TILES), :]
        out_ref[:, lanes] = x1_ref[:, lanes] + (g1 * y1 + g2 * y2)


def _combine(dest, x1, route, yb3):
    t = x1.shape[0]
    tc = min(ROW_DMA_TILE, t)
    dest_flat = _tiled_indices(_per_step_dest(dest, tc))
    return pl.pallas_call(
        _combine_kernel,
        grid=(t // tc,),
        in_specs=[pl.BlockSpec(memory_space=pl.ANY),
                  pl.BlockSpec((tc, D_MODEL), lambda i: (i, 0)),
                  pl.BlockSpec((tc, ROUTE_COLS), lambda i: (i, 0)),
                  pl.BlockSpec(memory_space=pl.ANY)],
        out_specs=pl.BlockSpec((tc, D_MODEL), lambda i: (i, 0)),
        out_shape=jax.ShapeDtypeStruct((t, D_MODEL), F32),
        scratch_shapes=[pltpu.SMEM((_index_tile_len(2 * tc),), jnp.int32),
                        pltpu.VMEM((tc * ROW_TILES, LANES), F32), pltpu.VMEM((tc * ROW_TILES, LANES), F32),
                        pltpu.SemaphoreType.DMA, pltpu.SemaphoreType.DMA],
        compiler_params=_params(), name="moe_combine",
    )(dest_flat, x1, route, yb3)


def _count_le(sorted_ends, values):
    ends = sorted_ends.reshape((-1,) + (1,) * values.ndim)
    return jnp.sum((ends <= values[None]).astype(jnp.int32), axis=0)


def _lookup(table, idx):
    keys = jnp.arange(table.shape[0], dtype=jnp.int32).reshape((-1,) + (1,) * idx.ndim)
    return jnp.sum(jnp.where(idx[None] == keys, table.reshape(keys.shape), 0), axis=0)


def _moe_block_rows(t):
    mean_rows_per_expert = max(2 * t // N_EXPERTS, 1)
    return min(MOE_BLOCK_ROWS, max(BF16_TILE_ROWS, 1 << (mean_rows_per_expert.bit_length() - 1)))


def _moe_experts(streams, layer, wg, wu, wd):
    t = sum(s[0].shape[0] for s in streams)
    blk = _moe_block_rows(t)
    cnts = [s[4][0, :N_EXPERTS].astype(jnp.int32) for s in streams]
    cnt = sum(cnts)
    padded = (cnt + blk - 1) // blk * blk
    pad_end = jnp.cumsum(padded)
    pad_start = pad_end - padded
    dests, seen = [], jnp.zeros_like(cnt)
    for s, c in zip(streams, cnts):
        expert = s[3][0:2].astype(jnp.int32)
        rank = s[3][4:6].astype(jnp.int32)
        dests.append(_lookup(pad_start + seen, expert) + rank)
        seen = seen + c
    n_blocks = -(-2 * t // blk) + N_EXPERTS
    blk_start = jnp.arange(n_blocks, dtype=jnp.int32) * blk
    block_expert = jnp.minimum(_count_le(pad_end, blk_start), N_EXPERTS - 1)
    block_nvalid = jnp.clip(_lookup(pad_start + cnt, block_expert) - blk_start, 0, blk).astype(jnp.int32)
    n_rows = n_blocks * blk
    gap_start = jnp.concatenate([pad_start + cnt, pad_end[-1:]])
    gap_len = jnp.concatenate([padded - cnt, n_rows - pad_end[-1:]])
    gap_end = jnp.cumsum(gap_len)
    p = jnp.arange(n_rows - 2 * t, dtype=jnp.int32)
    pad_rows = (p + _lookup(gap_start - gap_end + gap_len, _count_le(gap_end, p))).astype(jnp.int32)
    xs3 = _dispatch(dests, pad_rows, [s[1] for s in streams], n_rows)
    yb2 = _experts(block_expert, block_nvalid, xs3.reshape(n_rows * ROW_TILES, LANES), layer, wg, wu, wd)
    return yb2.reshape(n_rows, ROW_TILES, LANES), dests


def _router_weights(w_group, w_router):
    w = jnp.concatenate([w_router, w_group,
                         jnp.zeros((D_MODEL, ROUTE_LANES - N_EXPERTS - MOE_GROUPS), F32)], axis=1)
    hi = lax.bitcast_convert_type(lax.bitcast_convert_type(w, jnp.uint32) & jnp.uint32(0xFFFF0000), F32)
    hi, lo = hi.astype(BF16), (w - hi).astype(BF16)
    return hi, jnp.concatenate([hi, lo], axis=1)


def _rope_tables(pos):
    half = ROT_DIM // 2
    inv_freq = jnp.power(ROPE_THETA, -jnp.arange(half, dtype=F32) * (2.0 / ROT_DIM))
    ang = pos.astype(F32)[:, None] * inv_freq[None, :]
    cos, sin = jnp.cos(ang), jnp.sin(ang)
    n = pos.shape[0]
    ones = jnp.ones((n, DIFF_HEAD_DIM - ROT_DIM), F32)
    zeros = jnp.zeros((n, DIFF_HEAD_DIM - ROT_DIM), F32)
    zh = jnp.zeros((n, half), F32)
    sub_cos = jnp.concatenate([cos, cos, ones], axis=1)
    sub_lo = jnp.concatenate([-sin, zh, zeros], axis=1)
    sub_hi = jnp.concatenate([zh, sin, zeros], axis=1)
    return tuple(jnp.concatenate([a, a], axis=1) for a in (sub_cos, sub_lo, sub_hi))


def _spatial_weights(w_s, b_s, lc):
    pos = jnp.arange(lc)
    mask = (pos[None, :] // CHUNK) <= (pos[:, None] // CHUNK)
    ws = jnp.where(mask[None], w_s[:, :lc, :lc], 0.0)
    reps = GMLP_CHUNK // lc
    eye = jnp.eye(reps, dtype=F32)
    ws_eff = jnp.einsum("ab,gij->gaibj", eye, ws).reshape(GMLP_GROUPS, GMLP_CHUNK, GMLP_CHUNK)
    b_rows = jnp.tile(b_s[:, :lc], (1, reps))
    b_exp = jnp.repeat(b_rows.T, GMLP_GROUP_DIM, axis=1)
    return ws_eff, b_exp


def kernel(x_prompt, x_sample, cache_attn_k, cache_attn_v, mix_norm, ffn_norm, gmlp_w_in, gmlp_v_norm, gmlp_w_s, gmlp_b_s, gmlp_w_out, attn_w_in, attn_q_norm, attn_k_norm, attn_lam_q1, attn_lam_k1, attn_lam_q2, attn_lam_k2, attn_sub_norm, attn_w_out, moe_w_group, moe_w_router, moe_w_gate, moe_w_up, moe_w_down):
    nb_p, seq, _ = x_prompt.shape
    nb_s, dec, _ = x_sample.shape
    past = cache_attn_k.shape[2]
    assert cache_attn_k.shape[0] == DEPTH // 2 == 1 and mix_norm.shape[0] == DEPTH
    xp = x_prompt.reshape(nb_p * seq, D_MODEL)
    xs = x_sample.reshape(nb_s * dec, D_MODEL)
    row = lambda a: a.reshape(1, -1)

    router0 = _router_weights(moe_w_group[0], moe_w_router[0])
    gm_tail = (row(ffn_norm[0]),) + router0
    experts0 = (0, moe_w_gate, moe_w_up, moe_w_down)
    ws_p, b_p = _spatial_weights(gmlp_w_s[0], gmlp_b_s[0], GMLP_CHUNK)
    ws_s, b_s = _spatial_weights(gmlp_w_s[0], gmlp_b_s[0], dec)
    *mixed_p, gv_p = _gmlp_layer(
        xp, nb_p, row(mix_norm[0]), gmlp_w_in[0].astype(BF16), row(gmlp_v_norm[0]), ws_p.astype(BF16), b_p,
        gmlp_w_out[0].astype(BF16), *gm_tail)
    *mixed_s, gv_s = _gmlp_layer(
        xs, 1, row(mix_norm[0]), gmlp_w_in[0], row(gmlp_v_norm[0]), ws_s, b_s, gmlp_w_out[0], *gm_tail)
    yb3, (dest_p, dest_s) = _moe_experts([mixed_p, mixed_s], *experts0)
    xp = _combine(dest_p, mixed_p[0], mixed_p[2], yb3)
    xs = _combine(dest_s, mixed_s[0], mixed_s[2], yb3)

    lam_init = 0.8 - 0.6 * math.exp(-0.3 * 1)
    lam = (jnp.exp(jnp.sum(attn_lam_q1[0] * attn_lam_k1[0])) - jnp.exp(jnp.sum(attn_lam_q2[0] * attn_lam_k2[0]))
           + lam_init).reshape(1).astype(F32)
    grp = jnp.arange(D_MODEL) // DIFF_HEAD_DIM
    group_ones = (grp[:, None] == grp[None, :]).astype(BF16)
    qk = (row(mix_norm[1]), attn_w_in[0].astype(BF16), group_ones,
          row(jnp.tile(attn_q_norm[0], 2 * DIFF_HEADS)), row(jnp.tile(attn_k_norm[0], 2 * DIFF_HEADS)))
    router1 = _router_weights(moe_w_group[1], moe_w_router[1])
    at_tail = (attn_w_out[0].astype(BF16), row(ffn_norm[1])) + router1
    experts1 = (1, moe_w_gate, moe_w_up, moe_w_down)

    kp, vp, kb, vt, qt = _qkv_layer(xp, nb_p, *qk, *_rope_tables(jnp.arange(seq)), prompt=True)
    ap = _flash_attention(lam, qt, kb, vt, attn_sub_norm[0].reshape(LANES, 1), lam_init)
    mixed_p = _attn_out_layer(ap.reshape(nb_p * seq, D_MODEL), xp, *at_tail)

    pos_s = jnp.tile(past + jnp.arange(dec), nb_s)
    qs, ks, vs = _qkv_layer(xs, nb_s, *qk, *_rope_tables(pos_s), prompt=False)
    shp = (nb_s, dec, D_MODEL)
    a_s = _sample_attention(lam, qs.reshape(shp), ks.reshape(shp), vs.reshape(shp),
                            cache_attn_k.reshape(nb_s, past * DIFF_HEADS, LANES),
                            cache_attn_v.reshape(nb_s, past * DIFF_HEADS, LANES),
                            row(attn_sub_norm[0]), lam_init)
    mixed_s = _attn_out_layer(a_s.reshape(nb_s * dec, D_MODEL), xs, *at_tail)
    yb3, (dest_p, dest_s) = _moe_experts([mixed_p, mixed_s], *experts1)
    xp = _combine(dest_p, mixed_p[0], mixed_p[2], yb3)
    xs = _combine(dest_s, mixed_s[0], mixed_s[2], yb3)

    hv = (DIFF_HEADS, DIFF_VALUE_DIM)
    return (xp.reshape(nb_p, seq, D_MODEL), xs.reshape(nb_s, dec, D_MODEL),
            gv_p[None], gv_s.reshape(1, nb_s, dec, GMLP_WIDTH),
            kp.reshape(1, nb_p, seq, *hv), vp.reshape(1, nb_p, seq, *hv),
            ks.reshape(1, nb_s, dec, *hv), vs.reshape(1, nb_s, dec, *hv))
```

```python
import functools
import math

import jax
import jax.numpy as jnp
from jax import lax
from jax.experimental import pallas as pl
from jax.experimental.pallas import tpu as pltpu

D_MODEL = 1024
DEPTH = 2
CHUNK = 64
GMLP_CHUNK = 128
GMLP_WIDTH = 2 * D_MODEL
GMLP_GROUPS = 8
GMLP_GROUP_DIM = GMLP_WIDTH // GMLP_GROUPS
DIFF_HEADS = 8
DIFF_HEAD_DIM = D_MODEL // (2 * DIFF_HEADS)
DIFF_VALUE_DIM = 2 * DIFF_HEAD_DIM
ROT_DIM = DIFF_HEAD_DIM // 4
ROPE_THETA = 500000.0
MOE_GROUPS = 4
MOE_EXPERTS_PER_GROUP = 8
N_EXPERTS = MOE_GROUPS * MOE_EXPERTS_PER_GROUP
EXPERT_HIDDEN = D_MODEL // 2
EPS = 1e-6

LANES = 128
SUBLANES = 8
BF16_TILE_ROWS = 2 * SUBLANES
ROW_TILES = D_MODEL // LANES
VMEM_LIMIT_BYTES = 56 * 1024 * 1024

ROUTE_LANES = LANES
GROUP_LANE0 = N_EXPERTS
ROUTE_COLS = 8
NEG_BIG = -1e30

MIX_TILE = 512
OUT_PROJ_TILE = 256
ATT_K_TILE = 256
ATT_Q_TILE = 2 * ATT_K_TILE
VT_ROWS = LANES + BF16_TILE_ROWS
FLASH_HEADS_PER_STEP = 4
MOE_BLOCK_ROWS = 256
ROW_DMA_TILE = 512
INDEX_SLICE_WORDS = 1024

F32 = jnp.float32
BF16 = jnp.bfloat16


def _params(n_axes=1):
    return pltpu.CompilerParams(dimension_semantics=("arbitrary",) * n_axes,
                                vmem_limit_bytes=VMEM_LIMIT_BYTES)


def _rms(x, g):
    return x * lax.rsqrt(jnp.mean(x * x, axis=-1, keepdims=True) + EPS) * g


def _dot(a, b):
    return jnp.dot(a, b, preferred_element_type=F32)


def _dot_f32(a, b):
    return jnp.dot(a, b, preferred_element_type=F32, precision=lax.Precision.HIGHEST)


def _const_spec(shape):
    return pl.BlockSpec(shape, lambda *_: (0,) * len(shape), pipeline_mode=pl.Buffered(1))


def _route_epilogue(x1, ffn_g, wr_hi, wr_hi_lo, base_ref, hn3_ref, route_ref, route_t_ref, counts_ref):
    tm = x1.shape[0]

    @pl.when(pl.program_id(0) == 0)
    def _():
        base_ref[...] = jnp.zeros_like(base_ref)

    hn = _rms(x1, ffn_g)
    for s in range(ROW_TILES):
        hn3_ref[pl.ds(s, tm, stride=ROW_TILES), :] = hn[:, s * LANES:(s + 1) * LANES]

    h_hi = hn.astype(BF16)
    h_lo = (hn - h_hi.astype(F32)).astype(BF16)
    both = _dot(h_hi, wr_hi_lo)
    logit = both[:, :ROUTE_LANES] + both[:, ROUTE_LANES:] + _dot(h_lo, wr_hi)

    lane = lax.broadcasted_iota(jnp.int32, (tm, ROUTE_LANES), 1)
    far = jnp.int32(4 * ROUTE_LANES)
    lg = jnp.where(lane >= GROUP_LANE0, jnp.where(lane < GROUP_LANE0 + MOE_GROUPS, logit, NEG_BIG), NEG_BIG)
    mg = jnp.max(lg, axis=1, keepdims=True)
    g_lane = jnp.min(jnp.where(lg == mg, lane, far), axis=1, keepdims=True)
    g_sel = g_lane - GROUP_LANE0
    p_g = 1.0 / jnp.sum(jnp.exp(lg - mg), axis=1, keepdims=True)

    lo_lane = g_sel * MOE_EXPERTS_PER_GROUP
    le = jnp.where(lane >= lo_lane, jnp.where(lane < lo_lane + MOE_EXPERTS_PER_GROUP, logit, NEG_BIG), NEG_BIG)
    m1 = jnp.max(le, axis=1, keepdims=True)
    j1 = jnp.min(jnp.where(le == m1, lane, far), axis=1, keepdims=True)
    le2 = jnp.where(lane == j1, NEG_BIG, le)
    m2 = jnp.max(le2, axis=1, keepdims=True)
    j2 = jnp.min(jnp.where(le2 == m2, lane, far), axis=1, keepdims=True)
    r = jnp.exp(m2 - m1)
    gate1 = p_g / (1.0 + r)
    gate2 = p_g * r / (1.0 + r)

    hit1 = lane == j1
    hit2 = lane == j2
    onehot = jnp.where(hit1, 1.0, jnp.where(hit2, 1.0, 0.0))
    row = lax.broadcasted_iota(jnp.int32, (tm, tm), 0)
    col = lax.broadcasted_iota(jnp.int32, (tm, tm), 1)
    earlier = jnp.where(row > col, 1.0, 0.0).astype(BF16)
    prefix = _dot(earlier, onehot.astype(BF16)) + base_ref[...]
    rank1 = jnp.sum(jnp.where(hit1, prefix, 0.0), axis=1, keepdims=True)
    rank2 = jnp.sum(jnp.where(hit2, prefix, 0.0), axis=1, keepdims=True)
    base_new = base_ref[...] + jnp.sum(onehot, axis=0, keepdims=True)
    base_ref[...] = base_new
    counts_ref[...] = base_new

    rec = jnp.where(lane == 0, j1.astype(F32),
          jnp.where(lane == 1, j2.astype(F32),
          jnp.where(lane == 2, gate1,
          jnp.where(lane == 3, gate2,
          jnp.where(lane == 4, rank1,
          jnp.where(lane == 5, rank2, 0.0))))))
    route_ref[...] = rec[:, :ROUTE_COLS]
    route_t_ref[...] = rec.T[:ROUTE_COLS]


def _route_out_shapes(t):
    return (jax.ShapeDtypeStruct((t * ROW_TILES, LANES), F32),
            jax.ShapeDtypeStruct((t, ROUTE_COLS), F32),
            jax.ShapeDtypeStruct((ROUTE_COLS, t), F32),
            jax.ShapeDtypeStruct((1, ROUTE_LANES), F32))


def _route_out_specs(tm):
    return (pl.BlockSpec((tm * ROW_TILES, LANES), lambda i: (i, 0)),
            pl.BlockSpec((tm, ROUTE_COLS), lambda i: (i, 0)),
            pl.BlockSpec((ROUTE_COLS, tm), lambda i: (0, i)),
            pl.BlockSpec((1, ROUTE_LANES), lambda i: (0, 0)))


def _route_in_specs():
    return [_const_spec((1, D_MODEL)), _const_spec((D_MODEL, ROUTE_LANES)), _const_spec((D_MODEL, 2 * ROUTE_LANES))]


def _gelu_tanh(x):
    cdf = 0.5 * (1.0 + jnp.tanh(math.sqrt(2.0 / math.pi) * (x + 0.044715 * (x * x * x))))
    return x * cdf


def _gmlp_kernel(x_ref, mixg_ref, win_ref, vng_ref, ws_ref, bexp_ref, wout_ref,
                 ffng_ref, wrhi_ref, wrlo_ref,
                 x1_ref, hn3_ref, route_ref, route_t_ref, counts_ref, vlast_ref, base_ref, *, tiles_per_batch):
    tm = x_ref.shape[0]
    mm = win_ref.dtype
    dot = _dot if mm == BF16 else _dot_f32
    x = x_ref[...]
    h = _rms(x, mixg_ref[...]).astype(mm)
    z = _gelu_tanh(dot(h, win_ref[...]))
    u = z[:, :GMLP_WIDTH]
    vn = _rms(z[:, GMLP_WIDTH:], vng_ref[...])
    vb = vn.astype(mm)
    gated = []
    for c in range(tm // GMLP_CHUNK):
        rows = slice(c * GMLP_CHUNK, (c + 1) * GMLP_CHUNK)
        s = jnp.concatenate(
            [dot(ws_ref[g], vb[rows, g * GMLP_GROUP_DIM:(g + 1) * GMLP_GROUP_DIM])
             for g in range(GMLP_GROUPS)], axis=1) + bexp_ref[...]
        gated.append((u[rows] * s).astype(mm))
    y = dot(jnp.concatenate(gated, axis=0), wout_ref[...])
    x1 = x + y
    x1_ref[...] = x1

    @pl.when(pl.program_id(0) % tiles_per_batch == tiles_per_batch - 1)
    def _():
        vlast_ref[0] = vn[tm - GMLP_CHUNK:]

    _route_epilogue(x1, ffng_ref[...], wrhi_ref[...], wrlo_ref[...], base_ref,
                    hn3_ref, route_ref, route_t_ref, counts_ref)


def _gmlp_layer(x, n_batch, mix_g, w_in, vn_g, ws_eff, b_exp, w_out, ffn_g, wr_hi, wr_lo):
    t = x.shape[0]
    tm = min(MIX_TILE, t)
    tiles_per_batch = t // n_batch // tm
    gw = GMLP_WIDTH
    kern = functools.partial(_gmlp_kernel, tiles_per_batch=tiles_per_batch)
    return pl.pallas_call(
        kern,
        grid=(t // tm,),
        in_specs=[pl.BlockSpec((tm, D_MODEL), lambda i: (i, 0)),
                  _const_spec((1, D_MODEL)), _const_spec((D_MODEL, 2 * gw)), _const_spec((1, gw)),
                  _const_spec((GMLP_GROUPS, GMLP_CHUNK, GMLP_CHUNK)), _const_spec((GMLP_CHUNK, gw)),
                  _const_spec((gw, D_MODEL))] + _route_in_specs(),
        out_specs=(pl.BlockSpec((tm, D_MODEL), lambda i: (i, 0)),) + _route_out_specs(tm)
                  + (pl.BlockSpec((1, GMLP_CHUNK, gw), lambda i: (i // tiles_per_batch, 0, 0)),),
        out_shape=(jax.ShapeDtypeStruct((t, D_MODEL), F32),) + _route_out_shapes(t)
                  + (jax.ShapeDtypeStruct((n_batch, GMLP_CHUNK, gw), F32),),
        scratch_shapes=[pltpu.VMEM((1, ROUTE_LANES), F32)],
        compiler_params=_params(),
        name="gmlp_mixer",
    )(x, mix_g, w_in, vn_g, ws_eff, b_exp, w_out, ffn_g, wr_hi, wr_lo)


def _qk_norm_rope(t, group_ones, gain, cos, sin_lo, sin_hi):
    ms = _dot((t * t).astype(BF16), group_ones) * (1.0 / DIFF_HEAD_DIM)
    tn = t * lax.rsqrt(ms + EPS) * gain
    heads = []
    for h in range(DIFF_HEADS):
        th = tn[:, h * LANES:(h + 1) * LANES]
        heads.append(th * cos + pltpu.roll(th, LANES - ROT_DIM // 2, 1) * sin_lo
                     + pltpu.roll(th, ROT_DIM // 2, 1) * sin_hi)
    return jnp.concatenate(heads, axis=1)


def _qkv_kernel(x_ref, mixg_ref, win_ref, ones_ref, qg_ref, kg_ref, cos_ref, slo_ref, shi_ref, *outs, prompt):
    tm = x_ref.shape[0]
    h = _rms(x_ref[...], mixg_ref[...]).astype(BF16)
    qkv = _dot(h, win_ref[...])
    rope = (cos_ref[...], slo_ref[...], shi_ref[...])
    q_scale = DIFF_HEAD_DIM ** -0.5 * (math.log2(math.e) if prompt else 1.0)
    q = _qk_norm_rope(qkv[:, :D_MODEL], ones_ref[...], qg_ref[...], *rope) * q_scale
    k = _qk_norm_rope(qkv[:, D_MODEL:2 * D_MODEL], ones_ref[...], kg_ref[...], *rope)
    v = qkv[:, 2 * D_MODEL:]
    if not prompt:
        q_ref, kout_ref, vout_ref = outs
        q_ref[...] = q
        kout_ref[...] = k
        vout_ref[...] = v
        return
    kout_ref, vout_ref, kb_ref, vt_ref, qt_ref = outs
    kout_ref[...] = k
    vout_ref[...] = v
    pad_row = lax.broadcasted_iota(jnp.int32, (VT_ROWS - LANES, ATT_K_TILE), 0)
    ones_row = jnp.where(pad_row == 0, 1.0, 0.0).astype(BF16)
    for hd in range(DIFF_HEADS):
        lanes = slice(hd * LANES, (hd + 1) * LANES)
        kb_ref[0, hd] = k[:, lanes].astype(BF16)
        for c in range(tm // ATT_K_TILE):
            vt_ref[0, hd, c, :LANES] = v[c * ATT_K_TILE:(c + 1) * ATT_K_TILE, lanes].T.astype(BF16)
            vt_ref[0, hd, c, LANES:] = ones_row
        for c in range(tm // ATT_Q_TILE):
            qt_ref[0, hd, c] = q[c * ATT_Q_TILE:(c + 1) * ATT_Q_TILE, lanes].T.astype(BF16)


def _qkv_layer(x, n_batch, mix_g, w_in, group_ones, q_g, k_g, cos, sin_lo, sin_hi, prompt):
    t = x.shape[0]
    tm = min(MIX_TILE, t)
    seq = t // n_batch
    tpb = seq // tm if prompt else 1
    n_tab = cos.shape[0] // tm
    tab_spec = pl.BlockSpec((tm, LANES), lambda i: (i % n_tab, 0))
    row_spec = pl.BlockSpec((tm, D_MODEL), lambda i: (i, 0))
    in_specs = [row_spec, _const_spec((1, D_MODEL)), _const_spec((D_MODEL, 3 * D_MODEL)),
                _const_spec((D_MODEL, D_MODEL)), _const_spec((1, D_MODEL)), _const_spec((1, D_MODEL)),
                tab_spec, tab_spec, tab_spec]
    row_shape = jax.ShapeDtypeStruct((t, D_MODEL), F32)
    if prompt:
        tq, tk = ATT_Q_TILE, ATT_K_TILE
        tiled = lambda rows, tile: pl.BlockSpec((1, DIFF_HEADS, tm // tile, rows, tile),
                                                lambda i: (i // tpb, 0, i % tpb, 0, 0))
        out_specs = (row_spec, row_spec,
                     pl.BlockSpec((1, DIFF_HEADS, tm, LANES), lambda i: (i // tpb, 0, i % tpb, 0)),
                     tiled(VT_ROWS, tk), tiled(LANES, tq))
        out_shape = (row_shape, row_shape,
                     jax.ShapeDtypeStruct((n_batch, DIFF_HEADS, seq, LANES), BF16),
                     jax.ShapeDtypeStruct((n_batch, DIFF_HEADS, seq // tk, VT_ROWS, tk), BF16),
                     jax.ShapeDtypeStruct((n_batch, DIFF_HEADS, seq // tq, LANES, tq), BF16))
    else:
        out_specs = (row_spec, row_spec, row_spec)
        out_shape = (row_shape, row_shape, row_shape)
    return pl.pallas_call(
        functools.partial(_qkv_kernel, prompt=prompt),
        grid=(t // tm,), in_specs=in_specs, out_specs=out_specs, out_shape=out_shape,
        compiler_params=_params(), name="qkv_prompt" if prompt else "qkv_sample",
    )(x, mix_g, w_in, group_ones, q_g, k_g, cos, sin_lo, sin_hi)


def _flash_kernel(lam_ref, qt_ref, k_ref, vt_ref, sg_ref, o_ref, s_scr, p_scr, acc_scr, *, lam_init):
    tq, tk = ATT_Q_TILE, ATT_K_TILE
    n_heads = qt_ref.shape[1]
    i = pl.program_id(2)
    frow = lax.broadcasted_iota(jnp.int32, (LANES, tq), 0)
    qbd = []
    for g in range(n_heads):
        qt = qt_ref[0, g, 0]
        zero = jnp.zeros_like(qt)
        qbd.append(jnp.concatenate([jnp.where(frow < DIFF_HEAD_DIM, qt, zero),
                                    jnp.where(frow >= DIFF_HEAD_DIM, qt, zero)], axis=1))

    def scores(g, j):
        return _dot(k_ref[0, g, pl.ds(pl.multiple_of(j * tk, tk), tk), :], qbd[g])

    def softmax(s, m):
        m_new = jnp.maximum(m, jnp.max(s, axis=0, keepdims=True))
        return m_new, jnp.exp2(m - m_new), jnp.exp2(s - m_new).astype(BF16)

    for g in range(n_heads):
        s_scr[g, 0] = scores(g, 0)
        p_scr[g, 1] = jnp.zeros(p_scr.shape[2:], p_scr.dtype)
        acc_scr[g] = jnp.zeros(acc_scr.shape[1:], acc_scr.dtype)

    def stage(g, j, cur, oth, m):
        pv = _dot(vt_ref[0, g, jnp.maximum(j - 1, 0)], p_scr[g, oth])
        m, alpha, p = softmax(s_scr[g, cur], m)
        p_scr[g, cur] = p
        acc_scr[g] = alpha * (acc_scr[g] + pv)
        s_scr[g, oth] = scores(g, j + 1)
        return m

    def stage_pair(jj, carry):
        return tuple(stage(g, 2 * jj + 1, 1, 0, stage(g, 2 * jj, 0, 1, carry[g])) for g in range(n_heads))

    init = (jnp.full((1, 2 * tq), NEG_BIG, F32),) * n_heads
    carry = lax.fori_loop(0, i, stage_pair, init)

    r = 2 * i
    kchunk = lax.broadcasted_iota(jnp.int32, (tk, 2 * tq), 0) // CHUNK
    qchunk = (lax.broadcasted_iota(jnp.int32, (tk, 2 * tq), 1) % tq) // CHUNK
    lam = lam_ref[0]
    for g in range(n_heads):
        m = carry[g]
        pv = _dot(vt_ref[0, g, jnp.maximum(r - 1, 0)], p_scr[g, 1])
        m, alpha, p_r = softmax(jnp.where(kchunk <= qchunk, s_scr[g, 0], NEG_BIG), m)
        acc = alpha * (acc_scr[g] + pv)
        s_last = scores(g, r + 1)
        pv = _dot(vt_ref[0, g, r], p_r)
        m, alpha, p_last = softmax(jnp.where(kchunk + tk // CHUNK <= qchunk, s_last, NEG_BIG), m)
        acc = alpha * (acc + pv) + _dot(vt_ref[0, g, r + 1], p_last)
        l = acc[LANES:LANES + 1]
        acc = acc[:LANES]
        o = acc[:, :tq] / l[:, :tq] - lam * (acc[:, tq:] / l[:, tq:])
        on = o * lax.rsqrt(jnp.mean(o * o, axis=0, keepdims=True) + EPS) * sg_ref[...] * (1.0 - lam_init)
        o_ref[0, :, g * LANES:(g + 1) * LANES] = on.T.astype(o_ref.dtype)


def _flash_attention(lam, qt, kb, vt, sub_g_col, lam_init):
    n_batch, _, nq, _, tq = qt.shape
    nk, tk = vt.shape[2], vt.shape[4]
    seq = nq * tq
    assert tq == 2 * tk
    hg = FLASH_HEADS_PER_STEP
    return pl.pallas_call(
        functools.partial(_flash_kernel, lam_init=lam_init),
        grid=(n_batch, DIFF_HEADS // hg, nq),
        in_specs=[pl.BlockSpec(memory_space=pltpu.SMEM),
                  pl.BlockSpec((1, hg, 1, LANES, tq), lambda b, h, i: (b, h, i, 0, 0)),
                  pl.BlockSpec((1, hg, seq, LANES), lambda b, h, i: (b, h, 0, 0)),
                  pl.BlockSpec((1, hg, nk, VT_ROWS, tk), lambda b, h, i: (b, h, 0, 0, 0)),
                  pl.BlockSpec((LANES, 1), lambda b, h, i: (0, 0))],
        out_specs=pl.BlockSpec((1, tq, hg * LANES), lambda b, h, i: (b, i, h)),
        out_shape=jax.ShapeDtypeStruct((n_batch, seq, D_MODEL), BF16),
        scratch_shapes=[pltpu.VMEM((hg, 2, tk, 2 * tq), F32),
                        pltpu.VMEM((hg, 2, tk, 2 * tq), BF16),
                        pltpu.VMEM((hg, VT_ROWS, 2 * tq), F32)],
        compiler_params=_params(3), name="diff_flash",
    )(lam, qt, kb, vt, sub_g_col)


def _sample_attn_kernel(lam_ref, q_ref, kn_ref, vn_ref, ck_ref, cv_ref, sg_ref, o_ref, *, lam_init, past):
    q = q_ref[0]
    kn = kn_ref[0]
    vn = vn_ref[0]
    lam = lam_ref[0]
    lane = lax.broadcasted_iota(jnp.int32, (q.shape[0], LANES), 1)
    contract_last = (((1,), (1,)), ((), ()))
    outs = []
    for h in range(DIFF_HEADS):
        lanes = slice(h * LANES, (h + 1) * LANES)
        kc = ck_ref[0, pl.ds(h, past, stride=DIFF_HEADS), :].astype(BF16)
        vc = cv_ref[0, pl.ds(h, past, stride=DIFF_HEADS), :].astype(BF16)
        qh = q[:, lanes]
        knh = kn[:, lanes].astype(BF16)
        vnh = vn[:, lanes].astype(BF16)
        sub = []
        for c in range(2):
            keep = (lane < DIFF_HEAD_DIM) if c == 0 else (lane >= DIFF_HEAD_DIM)
            qc = jnp.where(keep, qh, 0.0).astype(BF16)
            s_old = lax.dot_general(qc, kc, contract_last, preferred_element_type=F32)
            s_new = lax.dot_general(qc, knh, contract_last, preferred_element_type=F32)
            m = jnp.maximum(jnp.max(s_old, axis=1, keepdims=True), jnp.max(s_new, axis=1, keepdims=True))
            p_old = jnp.exp(s_old - m)
            p_new = jnp.exp(s_new - m)
            l = jnp.sum(p_old, axis=1, keepdims=True) + jnp.sum(p_new, axis=1, keepdims=True)
            sub.append((_dot(p_old.astype(BF16), vc) + _dot(p_new.astype(BF16), vnh)) / l)
        o = sub[0] - lam * sub[1]
        outs.append(_rms(o, sg_ref[...]) * (1.0 - lam_init))
    o_ref[0] = jnp.concatenate(outs, axis=1).astype(o_ref.dtype)


def _sample_attention(lam, q, kn, vn, cache_k, cache_v, sub_g_row, lam_init):
    nb, rows, _ = q.shape
    past = cache_k.shape[1] // DIFF_HEADS
    new_spec = pl.BlockSpec((1, rows, D_MODEL), lambda b: (b, 0, 0))
    cache_spec = pl.BlockSpec((1, past * DIFF_HEADS, LANES), lambda b: (b, 0, 0))
    return pl.pallas_call(
        functools.partial(_sample_attn_kernel, lam_init=lam_init, past=past),
        grid=(nb,),
        in_specs=[pl.BlockSpec(memory_space=pltpu.SMEM), new_spec, new_spec, new_spec, cache_spec, cache_spec,
                  _const_spec((1, LANES))],
        out_specs=new_spec, out_shape=jax.ShapeDtypeStruct((nb, rows, D_MODEL), BF16),
        compiler_params=_params(), name="sample_attn",
    )(lam, q, kn, vn, cache_k, cache_v, sub_g_row)


def _attn_out_kernel(a_ref, x_ref, wout_ref, ffng_ref, wrhi_ref, wrlo_ref,
                     x1_ref, hn3_ref, route_ref, route_t_ref, counts_ref, base_ref):
    x1 = x_ref[...] + _dot(a_ref[...], wout_ref[...])
    x1_ref[...] = x1
    _route_epilogue(x1, ffng_ref[...], wrhi_ref[...], wrlo_ref[...], base_ref,
                    hn3_ref, route_ref, route_t_ref, counts_ref)


def _attn_out_layer(a, x, w_out, ffn_g, wr_hi, wr_lo):
    t = x.shape[0]
    tm = min(OUT_PROJ_TILE, t)
    row_spec = pl.BlockSpec((tm, D_MODEL), lambda i: (i, 0))
    return pl.pallas_call(
        _attn_out_kernel,
        grid=(t // tm,),
        in_specs=[row_spec, row_spec, _const_spec((D_MODEL, D_MODEL))] + _route_in_specs(),
        out_specs=(row_spec,) + _route_out_specs(tm),
        out_shape=(jax.ShapeDtypeStruct((t, D_MODEL), F32),) + _route_out_shapes(t),
        scratch_shapes=[pltpu.VMEM((1, ROUTE_LANES), F32)],
        compiler_params=_params(), name="attn_out",
    )(a, x, w_out, ffn_g, wr_hi, wr_lo)


def _index_tile_len(n_indices):
    return -(-n_indices // INDEX_SLICE_WORDS) * INDEX_SLICE_WORDS


def _per_step_dest(dest, tokens_per_step):
    steps = dest.shape[1] // tokens_per_step
    return dest.reshape(2, steps, tokens_per_step).transpose(1, 0, 2).reshape(steps, 2 * tokens_per_step)


def _tiled_indices(per_step):
    n = per_step.shape[1]
    return jnp.pad(per_step, ((0, 0), (0, _index_tile_len(n) - n))).reshape(-1)


def _dispatch_kernel(idx_hbm, *refs, plan):
    n_streams = len(plan)
    hn_refs = refs[:n_streams]
    xs3_hbm, idx_smem, zero_buf, idx_sem, row_sem = refs[n_streams:]
    i = pl.program_id(0)
    zero_buf[...] = jnp.zeros_like(zero_buf)

    for hn_ref, (first, steps, td, n_fill, offset, length) in zip(hn_refs, plan):
        @pl.when(jnp.logical_and(i >= first, i < first + steps))
        def _(hn_ref=hn_ref, first=first, td=td, n_fill=n_fill, offset=offset, length=length):
            start = pl.multiple_of(offset + (i - first) * length, INDEX_SLICE_WORDS)
            idx_copy = pltpu.make_async_copy(idx_hbm.at[pl.ds(start, length)], idx_smem.at[pl.ds(0, length)], idx_sem)
            idx_copy.start()
            idx_copy.wait()

            def scatter(t, carry):
                src = hn_ref.at[pl.ds(pl.multiple_of(t * ROW_TILES, ROW_TILES), ROW_TILES)]
                pltpu.make_async_copy(src, xs3_hbm.at[idx_smem[t]], row_sem).start(priority=0)
                pltpu.make_async_copy(src, xs3_hbm.at[idx_smem[td + t]], row_sem).start(priority=1)
                return carry

            def fill(p, carry):
                pltpu.make_async_copy(zero_buf, xs3_hbm.at[idx_smem[2 * td + p]], row_sem).start()
                return carry

            lax.fori_loop(0, td, scatter, 0)
            lax.fori_loop(0, n_fill, fill, 0)
            done = xs3_hbm.at[pl.ds(0, 2 * td + n_fill)]
            pltpu.make_async_copy(done, done, row_sem).wait()


def _dispatch(dests, pad_rows, hn2s, n_rows):
    plan, tables, in_specs = [], [], [pl.BlockSpec(memory_space=pl.ANY)]
    first = offset = 0
    for k, (dest, hn2) in enumerate(zip(dests, hn2s)):
        t = hn2.shape[0] // ROW_TILES
        td = min(ROW_DMA_TILE, t)
        steps = t // td
        fill = pad_rows if k == len(dests) - 1 else pad_rows[:0]
        n_fill = fill.shape[0] // steps
        table = _tiled_indices(jnp.concatenate([_per_step_dest(dest, td), fill.reshape(steps, n_fill)], axis=1))
        length = table.shape[0] // steps
        plan.append((first, steps, td, n_fill, offset, length))
        tables.append(table)
        in_specs.append(pl.BlockSpec((td * ROW_TILES, LANES),
                                     lambda i, first=first, steps=steps: (jnp.clip(i - first, 0, steps - 1), 0)))
        first += steps
        offset += table.shape[0]
    return pl.pallas_call(
        functools.partial(_dispatch_kernel, plan=tuple(plan)),
        grid=(first,),
        in_specs=in_specs,
        out_specs=pl.BlockSpec(memory_space=pl.ANY),
        out_shape=jax.ShapeDtypeStruct((n_rows, ROW_TILES, LANES), F32),
        scratch_shapes=[pltpu.SMEM((max(p[5] for p in plan),), jnp.int32),
                        pltpu.VMEM((ROW_TILES, LANES), F32),
                        pltpu.SemaphoreType.DMA, pltpu.SemaphoreType.DMA],
        compiler_params=_params(), name="moe_dispatch",
    )(jnp.concatenate(tables), *hn2s)


def _expert_kernel(bexp_ref, nvalid_ref, xs_ref, wg_ref, wu_ref, wd_ref, yb_ref, wg_bf, wu_bf, wd_bf):
    blk = xs_ref.shape[0] // ROW_TILES
    i = pl.program_id(0)
    nvalid = nvalid_ref[i]

    @pl.when(jnp.logical_or(i == 0, bexp_ref[i] != bexp_ref[jnp.maximum(i - 1, 0)]))
    def _():
        wg_bf[...] = wg_ref[0, 0].astype(BF16)
        wu_bf[...] = wu_ref[0, 0].astype(BF16)
        wd_bf[...] = wd_ref[0, 0].astype(BF16)

    @pl.when(nvalid > 0)
    def _():
        x = jnp.concatenate([xs_ref[pl.ds(s, blk, stride=ROW_TILES), :] for s in range(ROW_TILES)], axis=1)
        row = lax.broadcasted_iota(jnp.int32, (blk, 1), 0)
        xb = jnp.where(row < nvalid, x, 0.0).astype(BF16)
        hg = _dot(xb, wg_bf[...])
        hu = _dot(xb, wu_bf[...])
        act = (hg * (1.0 / (1.0 + jnp.exp(-hg))) * hu).astype(BF16)
        y = _dot(act, wd_bf[...])
        for s in range(ROW_TILES):
            yb_ref[pl.ds(s, blk, stride=ROW_TILES), :] = y[:, s * LANES:(s + 1) * LANES]

    @pl.when(nvalid <= 0)
    def _():
        yb_ref[...] = jnp.zeros_like(yb_ref)


def _experts(block_expert, block_nvalid, xs2, layer, wg, wu, wd):
    n_blocks = block_expert.shape[0]
    blk = xs2.shape[0] // ROW_TILES // n_blocks
    rows_spec = pl.BlockSpec((blk * ROW_TILES, LANES), lambda i, be, nv: (i, 0))
    up_spec = pl.BlockSpec((1, 1, D_MODEL, EXPERT_HIDDEN), lambda i, be, nv: (layer, be[i], 0, 0))
    down_spec = pl.BlockSpec((1, 1, EXPERT_HIDDEN, D_MODEL), lambda i, be, nv: (layer, be[i], 0, 0))
    return pl.pallas_call(
        _expert_kernel,
        grid_spec=pltpu.PrefetchScalarGridSpec(
            num_scalar_prefetch=2, grid=(n_blocks,),
            in_specs=[rows_spec, up_spec, up_spec, down_spec],
            out_specs=rows_spec,
            scratch_shapes=[pltpu.VMEM((D_MODEL, EXPERT_HIDDEN), BF16), pltpu.VMEM((D_MODEL, EXPERT_HIDDEN), BF16),
                            pltpu.VMEM((EXPERT_HIDDEN, D_MODEL), BF16)]),
        out_shape=jax.ShapeDtypeStruct(xs2.shape, F32),
        compiler_params=_params(), name="moe_experts",
    )(block_expert, block_nvalid, xs2, wg, wu, wd)


def _combine_kernel(idx_hbm, x1_ref, route_ref, yb3_hbm, out_ref, idx_smem, bufs, idx_sem, row_sems, *, n_steps):
    tc = x1_ref.shape[0]
    n_idx = idx_smem.shape[0] // 2
    i = pl.program_id(0)

    def idx_copy(step):
        slot = lax.rem(step, 2)
        return pltpu.make_async_copy(idx_hbm.at[pl.ds(pl.multiple_of(step * n_idx, n_idx), n_idx)],
                                     idx_smem.at[pl.ds(pl.multiple_of(slot * n_idx, n_idx), n_idx)], idx_sem)

    def start_rows(step):
        slot = lax.rem(step, 2)
        base = slot * n_idx

        def body(t, carry):
            dst = pl.ds(pl.multiple_of(t * ROW_TILES, ROW_TILES), ROW_TILES)
            pltpu.make_async_copy(yb3_hbm.at[idx_smem[base + t]], bufs.at[slot, 0, dst],
                                  row_sems.at[slot]).start(priority=0)
            pltpu.make_async_copy(yb3_hbm.at[idx_smem[base + tc + t]], bufs.at[slot, 1, dst],
                                  row_sems.at[slot]).start(priority=1)
            return carry

        lax.fori_loop(0, tc, body, 0)

    @pl.when(i == 0)
    def _():
        first = idx_copy(0)
        first.start()
        first.wait()
        start_rows(0)
        if n_steps > 1:
            idx_copy(1).start()

    @pl.when(i + 1 < n_steps)
    def _():
        idx_copy(i + 1).wait()
        start_rows(i + 1)

        @pl.when(i + 2 < n_steps)
        def _():
            idx_copy(i + 2).start()

    slot = lax.rem(i, 2)
    mine = bufs.at[slot]
    pltpu.make_async_copy(mine, mine, row_sems.at[slot]).wait()
    g1 = route_ref[:, 2:3]
    g2 = route_ref[:, 3:4]
    for s in range(ROW_TILES):
        lanes = slice(s * LANES, (s + 1) * LANES)
        y1 = bufs[slot, 0, pl.ds(s, tc, stride=ROW_TILES), :]
        y2 = bufs[slot, 1, pl.ds(s, tc, stride=ROW_TILES), :]
        out_ref[:, lanes] = x1_ref[:, lanes] + (g1 * y1 + g2 * y2)


def _combine(dest, x1, route, yb3):
    t = x1.shape[0]
    tc = min(ROW_DMA_TILE, t)
    dest_flat = _tiled_indices(_per_step_dest(dest, tc))
    return pl.pallas_call(
        functools.partial(_combine_kernel, n_steps=t // tc),
        grid=(t // tc,),
        in_specs=[pl.BlockSpec(memory_space=pl.ANY),
                  pl.BlockSpec((tc, D_MODEL), lambda i: (i, 0)),
                  pl.BlockSpec((tc, ROUTE_COLS), lambda i: (i, 0)),
                  pl.BlockSpec(memory_space=pl.ANY)],
        out_specs=pl.BlockSpec((tc, D_MODEL), lambda i: (i, 0)),
        out_shape=jax.ShapeDtypeStruct((t, D_MODEL), F32),
        scratch_shapes=[pltpu.SMEM((2 * _index_tile_len(2 * tc),), jnp.int32),
                        pltpu.VMEM((2, 2, tc * ROW_TILES, LANES), F32),
                        pltpu.SemaphoreType.DMA, pltpu.SemaphoreType.DMA((2,))],
        compiler_params=_params(), name="moe_combine",
    )(dest_flat, x1, route, yb3)


def _count_le(sorted_ends, values):
    ends = sorted_ends.reshape((-1,) + (1,) * values.ndim)
    return jnp.sum((ends <= values[None]).astype(jnp.int32), axis=0)


def _lookup(table, idx):
    keys = jnp.arange(table.shape[0], dtype=jnp.int32).reshape((-1,) + (1,) * idx.ndim)
    return jnp.sum(jnp.where(idx[None] == keys, table.reshape(keys.shape), 0), axis=0)


def _moe_block_rows(t):
    mean_rows_per_expert = max(2 * t // N_EXPERTS, 1)
    return min(MOE_BLOCK_ROWS, max(BF16_TILE_ROWS, 1 << (mean_rows_per_expert.bit_length() - 1)))


def _moe_experts(streams, layer, wg, wu, wd):
    t = sum(s[0].shape[0] for s in streams)
    blk = _moe_block_rows(t)
    cnts = [s[4][0, :N_EXPERTS].astype(jnp.int32) for s in streams]
    cnt = sum(cnts)
    padded = (cnt + blk - 1) // blk * blk
    pad_end = jnp.cumsum(padded)
    pad_start = pad_end - padded
    dests, seen = [], jnp.zeros_like(cnt)
    for s, c in zip(streams, cnts):
        expert = s[3][0:2].astype(jnp.int32)
        rank = s[3][4:6].astype(jnp.int32)
        dests.append(_lookup(pad_start + seen, expert) + rank)
        seen = seen + c
    n_blocks = -(-2 * t // blk) + N_EXPERTS
    blk_start = jnp.arange(n_blocks, dtype=jnp.int32) * blk
    block_expert = jnp.minimum(_count_le(pad_end, blk_start), N_EXPERTS - 1)
    block_nvalid = jnp.clip(_lookup(pad_start + cnt, block_expert) - blk_start, 0, blk).astype(jnp.int32)
    n_rows = n_blocks * blk
    gap_start = jnp.concatenate([pad_start + cnt, pad_end[-1:]])
    gap_len = jnp.concatenate([padded - cnt, n_rows - pad_end[-1:]])
    gap_end = jnp.cumsum(gap_len)
    p = jnp.arange(n_rows - 2 * t, dtype=jnp.int32)
    pad_rows = (p + _lookup(gap_start - gap_end + gap_len, _count_le(gap_end, p))).astype(jnp.int32)
    xs3 = _dispatch(dests, pad_rows, [s[1] for s in streams], n_rows)
    yb2 = _experts(block_expert, block_nvalid, xs3.reshape(n_rows * ROW_TILES, LANES), layer, wg, wu, wd)
    return yb2.reshape(n_rows, ROW_TILES, LANES), dests


def _router_weights(w_group, w_router):
    w = jnp.concatenate([w_router, w_group,
                         jnp.zeros((D_MODEL, ROUTE_LANES - N_EXPERTS - MOE_GROUPS), F32)], axis=1)
    hi = lax.bitcast_convert_type(lax.bitcast_convert_type(w, jnp.uint32) & jnp.uint32(0xFFFF0000), F32)
    hi, lo = hi.astype(BF16), (w - hi).astype(BF16)
    return hi, jnp.concatenate([hi, lo], axis=1)


def _rope_tables(pos):
    half = ROT_DIM // 2
    inv_freq = jnp.power(ROPE_THETA, -jnp.arange(half, dtype=F32) * (2.0 / ROT_DIM))
    ang = pos.astype(F32)[:, None] * inv_freq[None, :]
    cos, sin = jnp.cos(ang), jnp.sin(ang)
    n = pos.shape[0]
    ones = jnp.ones((n, DIFF_HEAD_DIM - ROT_DIM), F32)
    zeros = jnp.zeros((n, DIFF_HEAD_DIM - ROT_DIM), F32)
    zh = jnp.zeros((n, half), F32)
    sub_cos = jnp.concatenate([cos, cos, ones], axis=1)
    sub_lo = jnp.concatenate([-sin, zh, zeros], axis=1)
    sub_hi = jnp.concatenate([zh, sin, zeros], axis=1)
    return tuple(jnp.concatenate([a, a], axis=1) for a in (sub_cos, sub_lo, sub_hi))


def _spatial_weights(w_s, b_s, lc):
    pos = jnp.arange(lc)
    mask = (pos[None, :] // CHUNK) <= (pos[:, None] // CHUNK)
    ws = jnp.where(mask[None], w_s[:, :lc, :lc], 0.0)
    reps = GMLP_CHUNK // lc
    eye = jnp.eye(reps, dtype=F32)
    ws_eff = jnp.einsum("ab,gij->gaibj", eye, ws).reshape(GMLP_GROUPS, GMLP_CHUNK, GMLP_CHUNK)
    b_rows = jnp.tile(b_s[:, :lc], (1, reps))
    b_exp = jnp.repeat(b_rows.T, GMLP_GROUP_DIM, axis=1)
    return ws_eff, b_exp


def kernel(x_prompt, x_sample, cache_attn_k, cache_attn_v, mix_norm, ffn_norm, gmlp_w_in, gmlp_v_norm, gmlp_w_s, gmlp_b_s, gmlp_w_out, attn_w_in, attn_q_norm, attn_k_norm, attn_lam_q1, attn_lam_k1, attn_lam_q2, attn_lam_k2, attn_sub_norm, attn_w_out, moe_w_group, moe_w_router, moe_w_gate, moe_w_up, moe_w_down):
    nb_p, seq, _ = x_prompt.shape
    nb_s, dec, _ = x_sample.shape
    past = cache_attn_k.shape[2]
    assert cache_attn_k.shape[0] == DEPTH // 2 == 1 and mix_norm.shape[0] == DEPTH
    xp = x_prompt.reshape(nb_p * seq, D_MODEL)
    xs = x_sample.reshape(nb_s * dec, D_MODEL)
    row = lambda a: a.reshape(1, -1)

    router0 = _router_weights(moe_w_group[0], moe_w_router[0])
    gm_tail = (row(ffn_norm[0]),) + router0
    experts0 = (0, moe_w_gate, moe_w_up, moe_w_down)
    ws_p, b_p = _spatial_weights(gmlp_w_s[0], gmlp_b_s[0], GMLP_CHUNK)
    ws_s, b_s = _spatial_weights(gmlp_w_s[0], gmlp_b_s[0], dec)
    *mixed_p, gv_p = _gmlp_layer(
        xp, nb_p, row(mix_norm[0]), gmlp_w_in[0].astype(BF16), row(gmlp_v_norm[0]), ws_p.astype(BF16), b_p,
        gmlp_w_out[0].astype(BF16), *gm_tail)
    *mixed_s, gv_s = _gmlp_layer(
        xs, 1, row(mix_norm[0]), gmlp_w_in[0], row(gmlp_v_norm[0]), ws_s, b_s, gmlp_w_out[0], *gm_tail)
    yb3, (dest_p, dest_s) = _moe_experts([mixed_p, mixed_s], *experts0)
    xp = _combine(dest_p, mixed_p[0], mixed_p[2], yb3)
    xs = _combine(dest_s, mixed_s[0], mixed_s[2], yb3)

    lam_init = 0.8 - 0.6 * math.exp(-0.3 * 1)
    lam = (jnp.exp(jnp.sum(attn_lam_q1[0] * attn_lam_k1[0])) - jnp.exp(jnp.sum(attn_lam_q2[0] * attn_lam_k2[0]))
           + lam_init).reshape(1).astype(F32)
    grp = jnp.arange(D_MODEL) // DIFF_HEAD_DIM
    group_ones = (grp[:, None] == grp[None, :]).astype(BF16)
    qk = (row(mix_norm[1]), attn_w_in[0].astype(BF16), group_ones,
          row(jnp.tile(attn_q_norm[0], 2 * DIFF_HEADS)), row(jnp.tile(attn_k_norm[0], 2 * DIFF_HEADS)))
    router1 = _router_weights(moe_w_group[1], moe_w_router[1])
    at_tail = (attn_w_out[0].astype(BF16), row(ffn_norm[1])) + router1
    experts1 = (1, moe_w_gate, moe_w_up, moe_w_down)

    kp, vp, kb, vt, qt = _qkv_layer(xp, nb_p, *qk, *_rope_tables(jnp.arange(seq)), prompt=True)
    ap = _flash_attention(lam, qt, kb, vt, attn_sub_norm[0].reshape(LANES, 1), lam_init)
    mixed_p = _attn_out_layer(ap.reshape(nb_p * seq, D_MODEL), xp, *at_tail)

    pos_s = jnp.tile(past + jnp.arange(dec), nb_s)
    qs, ks, vs = _qkv_layer(xs, nb_s, *qk, *_rope_tables(pos_s), prompt=False)
    shp = (nb_s, dec, D_MODEL)
    a_s = _sample_attention(lam, qs.reshape(shp), ks.reshape(shp), vs.reshape(shp),
                            cache_attn_k.reshape(nb_s, past * DIFF_HEADS, LANES),
                            cache_attn_v.reshape(nb_s, past * DIFF_HEADS, LANES),
                            row(attn_sub_norm[0]), lam_init)
    mixed_s = _attn_out_layer(a_s.reshape(nb_s * dec, D_MODEL), xs, *at_tail)
    yb3, (dest_p, dest_s) = _moe_experts([mixed_p, mixed_s], *experts1)
    xp = _combine(dest_p, mixed_p[0], mixed_p[2], yb3)
    xs = _combine(dest_s, mixed_s[0], mixed_s[2], yb3)

    hv = (DIFF_HEADS, DIFF_VALUE_DIM)
    return (xp.reshape(nb_p, seq, D_MODEL), xs.reshape(nb_s, dec, D_MODEL),
            gv_p[None], gv_s.reshape(1, nb_s, dec, GMLP_WIDTH),
            kp.reshape(1, nb_p, seq, *hv), vp.reshape(1, nb_p, seq, *hv),
            ks.reshape(1, nb_s, dec, *hv), vs.reshape(1, nb_s, dec, *hv))
```

```python
import functools
import math

import jax
import jax.numpy as jnp
from jax import lax
from jax.experimental import pallas as pl
from jax.experimental.pallas import tpu as pltpu

D_MODEL = 1024
DEPTH = 2
CHUNK = 64
GMLP_CHUNK = 128
GMLP_WIDTH = 2 * D_MODEL
GMLP_GROUPS = 8
GMLP_GROUP_DIM = GMLP_WIDTH // GMLP_GROUPS
DIFF_HEADS = 8
DIFF_HEAD_DIM = D_MODEL // (2 * DIFF_HEADS)
DIFF_VALUE_DIM = 2 * DIFF_HEAD_DIM
ROT_DIM = DIFF_HEAD_DIM // 4
ROPE_THETA = 500000.0
MOE_GROUPS = 4
MOE_EXPERTS_PER_GROUP = 8
N_EXPERTS = MOE_GROUPS * MOE_EXPERTS_PER_GROUP
EXPERT_HIDDEN = D_MODEL // 2
EPS = 1e-6

LANES = 128
SUBLANES = 8
BF16_TILE_ROWS = 2 * SUBLANES
ROW_TILES = D_MODEL // LANES
VMEM_LIMIT_BYTES = 56 * 1024 * 1024

ROUTE_LANES = LANES
GROUP_LANE0 = N_EXPERTS
ROUTE_COLS = 8
NEG_BIG = -1e30

MIX_TILE = 512
OUT_PROJ_TILE = 256
ATT_K_TILE = 256
ATT_Q_TILE = 2 * ATT_K_TILE
VT_ROWS = LANES + BF16_TILE_ROWS
FLASH_HEADS_PER_STEP = 4
MOE_BLOCK_ROWS = 256
ROW_DMA_TILE = 1024
INDEX_SLICE_WORDS = 1024

F32 = jnp.float32
BF16 = jnp.bfloat16


def _params(n_axes=1):
    return pltpu.CompilerParams(dimension_semantics=("arbitrary",) * n_axes,
                                vmem_limit_bytes=VMEM_LIMIT_BYTES)


def _rms(x, g):
    return x * lax.rsqrt(jnp.mean(x * x, axis=-1, keepdims=True) + EPS) * g


def _dot(a, b):
    return jnp.dot(a, b, preferred_element_type=F32)


def _dot_f32(a, b):
    return jnp.dot(a, b, preferred_element_type=F32, precision=lax.Precision.HIGHEST)


def _const_spec(shape):
    return pl.BlockSpec(shape, lambda *_: (0,) * len(shape), pipeline_mode=pl.Buffered(1))


def _route_epilogue(x1, ffn_g, wr_hi, wr_hi_lo, base_ref, hn3_ref, route_ref, route_t_ref, counts_ref):
    tm = x1.shape[0]

    @pl.when(pl.program_id(0) == 0)
    def _():
        base_ref[...] = jnp.zeros_like(base_ref)

    hn = _rms(x1, ffn_g)
    for s in range(ROW_TILES):
        hn3_ref[pl.ds(s, tm, stride=ROW_TILES), :] = hn[:, s * LANES:(s + 1) * LANES]

    h_hi = hn.astype(BF16)
    h_lo = (hn - h_hi.astype(F32)).astype(BF16)
    both = _dot(h_hi, wr_hi_lo)
    logit = both[:, :ROUTE_LANES] + both[:, ROUTE_LANES:] + _dot(h_lo, wr_hi)

    lane = lax.broadcasted_iota(jnp.int32, (tm, ROUTE_LANES), 1)
    far = jnp.int32(4 * ROUTE_LANES)
    lg = jnp.where(lane >= GROUP_LANE0, jnp.where(lane < GROUP_LANE0 + MOE_GROUPS, logit, NEG_BIG), NEG_BIG)
    mg = jnp.max(lg, axis=1, keepdims=True)
    g_lane = jnp.min(jnp.where(lg == mg, lane, far), axis=1, keepdims=True)
    g_sel = g_lane - GROUP_LANE0
    p_g = 1.0 / jnp.sum(jnp.exp(lg - mg), axis=1, keepdims=True)

    lo_lane = g_sel * MOE_EXPERTS_PER_GROUP
    le = jnp.where(lane >= lo_lane, jnp.where(lane < lo_lane + MOE_EXPERTS_PER_GROUP, logit, NEG_BIG), NEG_BIG)
    m1 = jnp.max(le, axis=1, keepdims=True)
    j1 = jnp.min(jnp.where(le == m1, lane, far), axis=1, keepdims=True)
    le2 = jnp.where(lane == j1, NEG_BIG, le)
    m2 = jnp.max(le2, axis=1, keepdims=True)
    j2 = jnp.min(jnp.where(le2 == m2, lane, far), axis=1, keepdims=True)
    r = jnp.exp(m2 - m1)
    gate1 = p_g / (1.0 + r)
    gate2 = p_g * r / (1.0 + r)

    hit1 = lane == j1
    hit2 = lane == j2
    onehot = jnp.where(hit1, 1.0, jnp.where(hit2, 1.0, 0.0))
    row = lax.broadcasted_iota(jnp.int32, (tm, tm), 0)
    col = lax.broadcasted_iota(jnp.int32, (tm, tm), 1)
    earlier = jnp.where(row > col, 1.0, 0.0).astype(BF16)
    prefix = _dot(earlier, onehot.astype(BF16)) + base_ref[...]
    rank1 = jnp.sum(jnp.where(hit1, prefix, 0.0), axis=1, keepdims=True)
    rank2 = jnp.sum(jnp.where(hit2, prefix, 0.0), axis=1, keepdims=True)
    base_new = base_ref[...] + jnp.sum(onehot, axis=0, keepdims=True)
    base_ref[...] = base_new
    counts_ref[...] = base_new

    rec = jnp.where(lane == 0, j1.astype(F32),
          jnp.where(lane == 1, j2.astype(F32),
          jnp.where(lane == 2, gate1,
          jnp.where(lane == 3, gate2,
          jnp.where(lane == 4, rank1,
          jnp.where(lane == 5, rank2, 0.0))))))
    route_ref[...] = rec[:, :ROUTE_COLS]
    route_t_ref[...] = rec.T[:ROUTE_COLS]


def _route_out_shapes(t):
    return (jax.ShapeDtypeStruct((t * ROW_TILES, LANES), F32),
            jax.ShapeDtypeStruct((t, ROUTE_COLS), F32),
            jax.ShapeDtypeStruct((ROUTE_COLS, t), F32),
            jax.ShapeDtypeStruct((1, ROUTE_LANES), F32))


def _route_out_specs(tm):
    return (pl.BlockSpec((tm * ROW_TILES, LANES), lambda i: (i, 0)),
            pl.BlockSpec((tm, ROUTE_COLS), lambda i: (i, 0)),
            pl.BlockSpec((ROUTE_COLS, tm), lambda i: (0, i)),
            pl.BlockSpec((1, ROUTE_LANES), lambda i: (0, 0)))


def _route_in_specs():
    return [_const_spec((1, D_MODEL)), _const_spec((D_MODEL, ROUTE_LANES)), _const_spec((D_MODEL, 2 * ROUTE_LANES))]


def _gelu_tanh(x):
    cdf = 0.5 * (1.0 + jnp.tanh(math.sqrt(2.0 / math.pi) * (x + 0.044715 * (x * x * x))))
    return x * cdf


def _gmlp_kernel(x_ref, mixg_ref, win_ref, vng_ref, ws_ref, bexp_ref, wout_ref,
                 ffng_ref, wrhi_ref, wrlo_ref,
                 x1_ref, hn3_ref, route_ref, route_t_ref, counts_ref, vlast_ref, base_ref, *, tiles_per_batch):
    tm = x_ref.shape[0]
    mm = win_ref.dtype
    dot = _dot if mm == BF16 else _dot_f32
    x = x_ref[...]
    h = _rms(x, mixg_ref[...]).astype(mm)
    z = _gelu_tanh(dot(h, win_ref[...]))
    u = z[:, :GMLP_WIDTH]
    vn = _rms(z[:, GMLP_WIDTH:], vng_ref[...])
    vb = vn.astype(mm)
    gated = []
    for c in range(tm // GMLP_CHUNK):
        rows = slice(c * GMLP_CHUNK, (c + 1) * GMLP_CHUNK)
        s = jnp.concatenate(
            [dot(ws_ref[g], vb[rows, g * GMLP_GROUP_DIM:(g + 1) * GMLP_GROUP_DIM])
             for g in range(GMLP_GROUPS)], axis=1) + bexp_ref[...]
        gated.append((u[rows] * s).astype(mm))
    y = dot(jnp.concatenate(gated, axis=0), wout_ref[...])
    x1 = x + y
    x1_ref[...] = x1

    @pl.when(pl.program_id(0) % tiles_per_batch == tiles_per_batch - 1)
    def _():
        vlast_ref[0] = vn[tm - GMLP_CHUNK:]

    _route_epilogue(x1, ffng_ref[...], wrhi_ref[...], wrlo_ref[...], base_ref,
                    hn3_ref, route_ref, route_t_ref, counts_ref)


def _gmlp_layer(x, n_batch, mix_g, w_in, vn_g, ws_eff, b_exp, w_out, ffn_g, wr_hi, wr_lo):
    t = x.shape[0]
    tm = min(MIX_TILE, t)
    tiles_per_batch = t // n_batch // tm
    gw = GMLP_WIDTH
    kern = functools.partial(_gmlp_kernel, tiles_per_batch=tiles_per_batch)
    return pl.pallas_call(
        kern,
        grid=(t // tm,),
        in_specs=[pl.BlockSpec((tm, D_MODEL), lambda i: (i, 0)),
                  _const_spec((1, D_MODEL)), _const_spec((D_MODEL, 2 * gw)), _const_spec((1, gw)),
                  _const_spec((GMLP_GROUPS, GMLP_CHUNK, GMLP_CHUNK)), _const_spec((GMLP_CHUNK, gw)),
                  _const_spec((gw, D_MODEL))] + _route_in_specs(),
        out_specs=(pl.BlockSpec((tm, D_MODEL), lambda i: (i, 0)),) + _route_out_specs(tm)
                  + (pl.BlockSpec((1, GMLP_CHUNK, gw), lambda i: (i // tiles_per_batch, 0, 0)),),
        out_shape=(jax.ShapeDtypeStruct((t, D_MODEL), F32),) + _route_out_shapes(t)
                  + (jax.ShapeDtypeStruct((n_batch, GMLP_CHUNK, gw), F32),),
        scratch_shapes=[pltpu.VMEM((1, ROUTE_LANES), F32)],
        compiler_params=_params(),
        name="gmlp_mixer",
    )(x, mix_g, w_in, vn_g, ws_eff, b_exp, w_out, ffn_g, wr_hi, wr_lo)


def _qk_norm_rope(t, group_ones, gain, cos, sin_lo, sin_hi):
    ms = _dot((t * t).astype(BF16), group_ones) * (1.0 / DIFF_HEAD_DIM)
    tn = t * lax.rsqrt(ms + EPS) * gain
    heads = []
    for h in range(DIFF_HEADS):
        th = tn[:, h * LANES:(h + 1) * LANES]
        heads.append(th * cos + pltpu.roll(th, LANES - ROT_DIM // 2, 1) * sin_lo
                     + pltpu.roll(th, ROT_DIM // 2, 1) * sin_hi)
    return jnp.concatenate(heads, axis=1)


def _qkv_kernel(x_ref, mixg_ref, win_ref, ones_ref, qg_ref, kg_ref, cos_ref, slo_ref, shi_ref, *outs, prompt):
    tm = x_ref.shape[0]
    h = _rms(x_ref[...], mixg_ref[...]).astype(BF16)
    qkv = _dot(h, win_ref[...])
    rope = (cos_ref[...], slo_ref[...], shi_ref[...])
    q_scale = DIFF_HEAD_DIM ** -0.5 * (math.log2(math.e) if prompt else 1.0)
    q = _qk_norm_rope(qkv[:, :D_MODEL], ones_ref[...], qg_ref[...], *rope) * q_scale
    k = _qk_norm_rope(qkv[:, D_MODEL:2 * D_MODEL], ones_ref[...], kg_ref[...], *rope)
    v = qkv[:, 2 * D_MODEL:]
    if not prompt:
        q_ref, kout_ref, vout_ref = outs
        q_ref[...] = q
        kout_ref[...] = k
        vout_ref[...] = v
        return
    kout_ref, vout_ref, kb_ref, vt_ref, qt_ref = outs
    kout_ref[...] = k
    vout_ref[...] = v
    pad_row = lax.broadcasted_iota(jnp.int32, (VT_ROWS - LANES, ATT_K_TILE), 0)
    ones_row = jnp.where(pad_row == 0, 1.0, 0.0).astype(BF16)
    for hd in range(DIFF_HEADS):
        lanes = slice(hd * LANES, (hd + 1) * LANES)
        kb_ref[0, hd] = k[:, lanes].astype(BF16)
        for c in range(tm // ATT_K_TILE):
            vt_ref[0, hd, c, :LANES] = v[c * ATT_K_TILE:(c + 1) * ATT_K_TILE, lanes].T.astype(BF16)
            vt_ref[0, hd, c, LANES:] = ones_row
        for c in range(tm // ATT_Q_TILE):
            qt_ref[0, hd, c] = q[c * ATT_Q_TILE:(c + 1) * ATT_Q_TILE, lanes].T.astype(BF16)


def _qkv_layer(x, n_batch, mix_g, w_in, group_ones, q_g, k_g, cos, sin_lo, sin_hi, prompt):
    t = x.shape[0]
    tm = min(MIX_TILE, t)
    seq = t // n_batch
    tpb = seq // tm if prompt else 1
    n_tab = cos.shape[0] // tm
    tab_spec = pl.BlockSpec((tm, LANES), lambda i: (i % n_tab, 0))
    row_spec = pl.BlockSpec((tm, D_MODEL), lambda i: (i, 0))
    in_specs = [row_spec, _const_spec((1, D_MODEL)), _const_spec((D_MODEL, 3 * D_MODEL)),
                _const_spec((D_MODEL, D_MODEL)), _const_spec((1, D_MODEL)), _const_spec((1, D_MODEL)),
                tab_spec, tab_spec, tab_spec]
    row_shape = jax.ShapeDtypeStruct((t, D_MODEL), F32)
    if prompt:
        tq, tk = ATT_Q_TILE, ATT_K_TILE
        tiled = lambda rows, tile: pl.BlockSpec((1, DIFF_HEADS, tm // tile, rows, tile),
                                                lambda i: (i // tpb, 0, i % tpb, 0, 0))
        out_specs = (row_spec, row_spec,
                     pl.BlockSpec((1, DIFF_HEADS, tm, LANES), lambda i: (i // tpb, 0, i % tpb, 0)),
                     tiled(VT_ROWS, tk), tiled(LANES, tq))
        out_shape = (row_shape, row_shape,
                     jax.ShapeDtypeStruct((n_batch, DIFF_HEADS, seq, LANES), BF16),
                     jax.ShapeDtypeStruct((n_batch, DIFF_HEADS, seq // tk, VT_ROWS, tk), BF16),
                     jax.ShapeDtypeStruct((n_batch, DIFF_HEADS, seq // tq, LANES, tq), BF16))
    else:
        out_specs = (row_spec, row_spec, row_spec)
        out_shape = (row_shape, row_shape, row_shape)
    return pl.pallas_call(
        functools.partial(_qkv_kernel, prompt=prompt),
        grid=(t // tm,), in_specs=in_specs, out_specs=out_specs, out_shape=out_shape,
        compiler_params=_params(), name="qkv_prompt" if prompt else "qkv_sample",
    )(x, mix_g, w_in, group_ones, q_g, k_g, cos, sin_lo, sin_hi)


def _flash_kernel(lam_ref, qt_ref, k_ref, vt_ref, sg_ref, o_ref, s_scr, p_scr, acc_scr, *, lam_init):
    tq, tk = ATT_Q_TILE, ATT_K_TILE
    n_heads = qt_ref.shape[1]
    i = pl.program_id(2)
    frow = lax.broadcasted_iota(jnp.int32, (LANES, tq), 0)
    qbd = []
    for g in range(n_heads):
        qt = qt_ref[0, g, 0]
        zero = jnp.zeros_like(qt)
        qbd.append(jnp.concatenate([jnp.where(frow < DIFF_HEAD_DIM, qt, zero),
                                    jnp.where(frow >= DIFF_HEAD_DIM, qt, zero)], axis=1))

    def scores(g, j):
        return _dot(k_ref[0, g, pl.ds(pl.multiple_of(j * tk, tk), tk), :], qbd[g])

    def softmax(s, m):
        m_new = jnp.maximum(m, jnp.max(s, axis=0, keepdims=True))
        return m_new, jnp.exp2(m - m_new), jnp.exp2(s - m_new).astype(BF16)

    for g in range(n_heads):
        s_scr[g, 0] = scores(g, 0)
        p_scr[g, 1] = jnp.zeros(p_scr.shape[2:], p_scr.dtype)
        acc_scr[g] = jnp.zeros(acc_scr.shape[1:], acc_scr.dtype)

    def stage(g, j, cur, oth, m):
        pv = _dot(vt_ref[0, g, jnp.maximum(j - 1, 0)], p_scr[g, oth])
        m, alpha, p = softmax(s_scr[g, cur], m)
        p_scr[g, cur] = p
        acc_scr[g] = alpha * (acc_scr[g] + pv)
        s_scr[g, oth] = scores(g, j + 1)
        return m

    def stage_pair(jj, carry):
        return tuple(stage(g, 2 * jj + 1, 1, 0, stage(g, 2 * jj, 0, 1, carry[g])) for g in range(n_heads))

    init = (jnp.full((1, 2 * tq), NEG_BIG, F32),) * n_heads
    carry = lax.fori_loop(0, i, stage_pair, init)

    r = 2 * i
    kchunk = lax.broadcasted_iota(jnp.int32, (tk, 2 * tq), 0) // CHUNK
    qchunk = (lax.broadcasted_iota(jnp.int32, (tk, 2 * tq), 1) % tq) // CHUNK
    lam = lam_ref[0]
    for g in range(n_heads):
        m = carry[g]
        pv = _dot(vt_ref[0, g, jnp.maximum(r - 1, 0)], p_scr[g, 1])
        m, alpha, p_r = softmax(jnp.where(kchunk <= qchunk, s_scr[g, 0], NEG_BIG), m)
        acc = alpha * (acc_scr[g] + pv)
        s_last = scores(g, r + 1)
        pv = _dot(vt_ref[0, g, r], p_r)
        m, alpha, p_last = softmax(jnp.where(kchunk + tk // CHUNK <= qchunk, s_last, NEG_BIG), m)
        acc = alpha * (acc + pv) + _dot(vt_ref[0, g, r + 1], p_last)
        l = acc[LANES:LANES + 1]
        acc = acc[:LANES]
        o = acc[:, :tq] / l[:, :tq] - lam * (acc[:, tq:] / l[:, tq:])
        on = o * lax.rsqrt(jnp.mean(o * o, axis=0, keepdims=True) + EPS) * sg_ref[...] * (1.0 - lam_init)
        o_ref[0, :, g * LANES:(g + 1) * LANES] = on.T.astype(o_ref.dtype)


def _flash_attention(lam, qt, kb, vt, sub_g_col, lam_init):
    n_batch, _, nq, _, tq = qt.shape
    nk, tk = vt.shape[2], vt.shape[4]
    seq = nq * tq
    assert tq == 2 * tk
    hg = FLASH_HEADS_PER_STEP
    return pl.pallas_call(
        functools.partial(_flash_kernel, lam_init=lam_init),
        grid=(n_batch, DIFF_HEADS // hg, nq),
        in_specs=[pl.BlockSpec(memory_space=pltpu.SMEM),
                  pl.BlockSpec((1, hg, 1, LANES, tq), lambda b, h, i: (b, h, i, 0, 0)),
                  pl.BlockSpec((1, hg, seq, LANES), lambda b, h, i: (b, h, 0, 0)),
                  pl.BlockSpec((1, hg, nk, VT_ROWS, tk), lambda b, h, i: (b, h, 0, 0, 0)),
                  pl.BlockSpec((LANES, 1), lambda b, h, i: (0, 0))],
        out_specs=pl.BlockSpec((1, tq, hg * LANES), lambda b, h, i: (b, i, h)),
        out_shape=jax.ShapeDtypeStruct((n_batch, seq, D_MODEL), BF16),
        scratch_shapes=[pltpu.VMEM((hg, 2, tk, 2 * tq), F32),
                        pltpu.VMEM((hg, 2, tk, 2 * tq), BF16),
                        pltpu.VMEM((hg, VT_ROWS, 2 * tq), F32)],
        compiler_params=_params(3), name="diff_flash",
    )(lam, qt, kb, vt, sub_g_col)


def _sample_attn_kernel(lam_ref, q_ref, kn_ref, vn_ref, ck_ref, cv_ref, sg_ref, o_ref, *, lam_init, past):
    q = q_ref[0]
    kn = kn_ref[0]
    vn = vn_ref[0]
    lam = lam_ref[0]
    lane = lax.broadcasted_iota(jnp.int32, (q.shape[0], LANES), 1)
    contract_last = (((1,), (1,)), ((), ()))
    outs = []
    for h in range(DIFF_HEADS):
        lanes = slice(h * LANES, (h + 1) * LANES)
        kc = ck_ref[0, pl.ds(h, past, stride=DIFF_HEADS), :].astype(BF16)
        vc = cv_ref[0, pl.ds(h, past, stride=DIFF_HEADS), :].astype(BF16)
        qh = q[:, lanes]
        knh = kn[:, lanes].astype(BF16)
        vnh = vn[:, lanes].astype(BF16)
        sub = []
        for c in range(2):
            keep = (lane < DIFF_HEAD_DIM) if c == 0 else (lane >= DIFF_HEAD_DIM)
            qc = jnp.where(keep, qh, 0.0).astype(BF16)
            s_old = lax.dot_general(qc, kc, contract_last, preferred_element_type=F32)
            s_new = lax.dot_general(qc, knh, contract_last, preferred_element_type=F32)
            m = jnp.maximum(jnp.max(s_old, axis=1, keepdims=True), jnp.max(s_new, axis=1, keepdims=True))
            p_old = jnp.exp(s_old - m)
            p_new = jnp.exp(s_new - m)
            l = jnp.sum(p_old, axis=1, keepdims=True) + jnp.sum(p_new, axis=1, keepdims=True)
            sub.append((_dot(p_old.astype(BF16), vc) + _dot(p_new.astype(BF16), vnh)) / l)
        o = sub[0] - lam * sub[1]
        outs.append(_rms(o, sg_ref[...]) * (1.0 - lam_init))
    o_ref[0] = jnp.concatenate(outs, axis=1).astype(o_ref.dtype)


def _sample_attention(lam, q, kn, vn, cache_k, cache_v, sub_g_row, lam_init):
    nb, rows, _ = q.shape
    past = cache_k.shape[1] // DIFF_HEADS
    new_spec = pl.BlockSpec((1, rows, D_MODEL), lambda b: (b, 0, 0))
    cache_spec = pl.BlockSpec((1, past * DIFF_HEADS, LANES), lambda b: (b, 0, 0))
    return pl.pallas_call(
        functools.partial(_sample_attn_kernel, lam_init=lam_init, past=past),
        grid=(nb,),
        in_specs=[pl.BlockSpec(memory_space=pltpu.SMEM), new_spec, new_spec, new_spec, cache_spec, cache_spec,
                  _const_spec((1, LANES))],
        out_specs=new_spec, out_shape=jax.ShapeDtypeStruct((nb, rows, D_MODEL), BF16),
        compiler_params=_params(), name="sample_attn",
    )(lam, q, kn, vn, cache_k, cache_v, sub_g_row)


def _attn_out_kernel(a_ref, x_ref, wout_ref, ffng_ref, wrhi_ref, wrlo_ref,
                     x1_ref, hn3_ref, route_ref, route_t_ref, counts_ref, base_ref):
    x1 = x_ref[...] + _dot(a_ref[...], wout_ref[...])
    x1_ref[...] = x1
    _route_epilogue(x1, ffng_ref[...], wrhi_ref[...], wrlo_ref[...], base_ref,
                    hn3_ref, route_ref, route_t_ref, counts_ref)


def _attn_out_layer(a, x, w_out, ffn_g, wr_hi, wr_lo):
    t = x.shape[0]
    tm = min(OUT_PROJ_TILE, t)
    row_spec = pl.BlockSpec((tm, D_MODEL), lambda i: (i, 0))
    return pl.pallas_call(
        _attn_out_kernel,
        grid=(t // tm,),
        in_specs=[row_spec, row_spec, _const_spec((D_MODEL, D_MODEL))] + _route_in_specs(),
        out_specs=(row_spec,) + _route_out_specs(tm),
        out_shape=(jax.ShapeDtypeStruct((t, D_MODEL), F32),) + _route_out_shapes(t),
        scratch_shapes=[pltpu.VMEM((1, ROUTE_LANES), F32)],
        compiler_params=_params(), name="attn_out",
    )(a, x, w_out, ffn_g, wr_hi, wr_lo)


def _index_tile_len(n_indices):
    return -(-n_indices // INDEX_SLICE_WORDS) * INDEX_SLICE_WORDS


def _per_step_dest(dest, tokens_per_step):
    steps = dest.shape[1] // tokens_per_step
    return dest.reshape(2, steps, tokens_per_step).transpose(1, 0, 2).reshape(steps, 2 * tokens_per_step)


def _tiled_indices(per_step):
    n = per_step.shape[1]
    return jnp.pad(per_step, ((0, 0), (0, _index_tile_len(n) - n))).reshape(-1)


def _dispatch_kernel(idx_hbm, *refs, plan):
    n_streams = len(plan)
    hn_refs = refs[:n_streams]
    xs3_hbm, idx_smem, zero_buf, idx_sem, row_sem = refs[n_streams:]
    i = pl.program_id(0)
    zero_buf[...] = jnp.zeros_like(zero_buf)

    for hn_ref, (first, steps, td, n_fill, offset, length) in zip(hn_refs, plan):
        @pl.when(jnp.logical_and(i >= first, i < first + steps))
        def _(hn_ref=hn_ref, first=first, td=td, n_fill=n_fill, offset=offset, length=length):
            start = pl.multiple_of(offset + (i - first) * length, INDEX_SLICE_WORDS)
            idx_copy = pltpu.make_async_copy(idx_hbm.at[pl.ds(start, length)], idx_smem.at[pl.ds(0, length)], idx_sem)
            idx_copy.start()
            idx_copy.wait()

            def scatter(t, carry):
                src = hn_ref.at[pl.ds(pl.multiple_of(t * ROW_TILES, ROW_TILES), ROW_TILES)]
                pltpu.make_async_copy(src, xs3_hbm.at[idx_smem[t]], row_sem).start(priority=0)
                pltpu.make_async_copy(src, xs3_hbm.at[idx_smem[td + t]], row_sem).start(priority=1)
                return carry

            def fill(p, carry):
                pltpu.make_async_copy(zero_buf, xs3_hbm.at[idx_smem[2 * td + p]], row_sem).start()
                return carry

            lax.fori_loop(0, td, scatter, 0)
            lax.fori_loop(0, n_fill, fill, 0)
            done = xs3_hbm.at[pl.ds(0, 2 * td + n_fill)]
            pltpu.make_async_copy(done, done, row_sem).wait()


def _dispatch(dests, pad_rows, hn2s, n_rows):
    plan, tables, in_specs = [], [], [pl.BlockSpec(memory_space=pl.ANY)]
    first = offset = 0
    for k, (dest, hn2) in enumerate(zip(dests, hn2s)):
        t = hn2.shape[0] // ROW_TILES
        td = min(ROW_DMA_TILE, t)
        steps = t // td
        fill = pad_rows if k == len(dests) - 1 else pad_rows[:0]
        n_fill = fill.shape[0] // steps
        table = _tiled_indices(jnp.concatenate([_per_step_dest(dest, td), fill.reshape(steps, n_fill)], axis=1))
        length = table.shape[0] // steps
        plan.append((first, steps, td, n_fill, offset, length))
        tables.append(table)
        in_specs.append(pl.BlockSpec((td * ROW_TILES, LANES),
                                     lambda i, first=first, steps=steps: (jnp.clip(i - first, 0, steps - 1), 0)))
        first += steps
        offset += table.shape[0]
    return pl.pallas_call(
        functools.partial(_dispatch_kernel, plan=tuple(plan)),
        grid=(first,),
        in_specs=in_specs,
        out_specs=pl.BlockSpec(memory_space=pl.ANY),
        out_shape=jax.ShapeDtypeStruct((n_rows, ROW_TILES, LANES), F32),
        scratch_shapes=[pltpu.SMEM((max(p[5] for p in plan),), jnp.int32),
                        pltpu.VMEM((ROW_TILES, LANES), F32),
                        pltpu.SemaphoreType.DMA, pltpu.SemaphoreType.DMA],
        compiler_params=_params(), name="moe_dispatch",
    )(jnp.concatenate(tables), *hn2s)


def _expert_kernel(bexp_ref, nvalid_ref, xs_ref, wg_ref, wu_ref, wd_ref, yb_ref, wg_bf, wu_bf, wd_bf):
    blk = xs_ref.shape[0] // ROW_TILES
    i = pl.program_id(0)
    nvalid = nvalid_ref[i]

    @pl.when(jnp.logical_or(i == 0, bexp_ref[i] != bexp_ref[jnp.maximum(i - 1, 0)]))
    def _():
        wg_bf[...] = wg_ref[0, 0].astype(BF16)
        wu_bf[...] = wu_ref[0, 0].astype(BF16)
        wd_bf[...] = wd_ref[0, 0].astype(BF16)

    @pl.when(nvalid > 0)
    def _():
        x = jnp.concatenate([xs_ref[pl.ds(s, blk, stride=ROW_TILES), :] for s in range(ROW_TILES)], axis=1)
        row = lax.broadcasted_iota(jnp.int32, (blk, 1), 0)
        xb = jnp.where(row < nvalid, x, 0.0).astype(BF16)
        hg = _dot(xb, wg_bf[...])
        hu = _dot(xb, wu_bf[...])
        act = (hg * (1.0 / (1.0 + jnp.exp(-hg))) * hu).astype(BF16)
        y = _dot(act, wd_bf[...])
        for s in range(ROW_TILES):
            yb_ref[pl.ds(s, blk, stride=ROW_TILES), :] = y[:, s * LANES:(s + 1) * LANES]

    @pl.when(nvalid <= 0)
    def _():
        yb_ref[...] = jnp.zeros_like(yb_ref)


def _experts(block_expert, block_nvalid, xs2, layer, wg, wu, wd):
    n_blocks = block_expert.shape[0]
    blk = xs2.shape[0] // ROW_TILES // n_blocks
    rows_spec = pl.BlockSpec((blk * ROW_TILES, LANES), lambda i, be, nv: (i, 0))
    up_spec = pl.BlockSpec((1, 1, D_MODEL, EXPERT_HIDDEN), lambda i, be, nv: (layer, be[i], 0, 0))
    down_spec = pl.BlockSpec((1, 1, EXPERT_HIDDEN, D_MODEL), lambda i, be, nv: (layer, be[i], 0, 0))
    return pl.pallas_call(
        _expert_kernel,
        grid_spec=pltpu.PrefetchScalarGridSpec(
            num_scalar_prefetch=2, grid=(n_blocks,),
            in_specs=[rows_spec, up_spec, up_spec, down_spec],
            out_specs=rows_spec,
            scratch_shapes=[pltpu.VMEM((D_MODEL, EXPERT_HIDDEN), BF16), pltpu.VMEM((D_MODEL, EXPERT_HIDDEN), BF16),
                            pltpu.VMEM((EXPERT_HIDDEN, D_MODEL), BF16)]),
        out_shape=jax.ShapeDtypeStruct(xs2.shape, F32),
        compiler_params=_params(), name="moe_experts",
    )(block_expert, block_nvalid, xs2, wg, wu, wd)


def _combine_kernel(idx_hbm, x1_ref, route_ref, yb3_hbm, out_ref, idx_smem, bufs, idx_sem, row_sems, *, n_steps):
    tc = x1_ref.shape[0]
    n_idx = idx_smem.shape[0] // 2
    i = pl.program_id(0)

    def idx_copy(step):
        slot = lax.rem(step, 2)
        return pltpu.make_async_copy(idx_hbm.at[pl.ds(pl.multiple_of(step * n_idx, n_idx), n_idx)],
                                     idx_smem.at[pl.ds(pl.multiple_of(slot * n_idx, n_idx), n_idx)], idx_sem)

    def start_rows(step):
        slot = lax.rem(step, 2)
        base = slot * n_idx

        def body(t, carry):
            dst = pl.ds(pl.multiple_of(t * ROW_TILES, ROW_TILES), ROW_TILES)
            pltpu.make_async_copy(yb3_hbm.at[idx_smem[base + t]], bufs.at[slot, 0, dst],
                                  row_sems.at[slot]).start(priority=0)
            pltpu.make_async_copy(yb3_hbm.at[idx_smem[base + tc + t]], bufs.at[slot, 1, dst],
                                  row_sems.at[slot]).start(priority=1)
            return carry

        lax.fori_loop(0, tc, body, 0)

    @pl.when(i == 0)
    def _():
        first = idx_copy(0)
        first.start()
        first.wait()
        start_rows(0)
        if n_steps > 1:
            idx_copy(1).start()

    @pl.when(i + 1 < n_steps)
    def _():
        idx_copy(i + 1).wait()
        start_rows(i + 1)

        @pl.when(i + 2 < n_steps)
        def _():
            idx_copy(i + 2).start()

    slot = lax.rem(i, 2)
    mine = bufs.at[slot]
    pltpu.make_async_copy(mine, mine, row_sems.at[slot]).wait()
    g1 = route_ref[:, 2:3]
    g2 = route_ref[:, 3:4]
    for s in range(ROW_TILES):
        lanes = slice(s * LANES, (s + 1) * LANES)
        y1 = bufs[slot, 0, pl.ds(s, tc, stride=ROW_TILES), :]
        y2 = bufs[slot, 1, pl.ds(s, tc, stride=ROW_TILES), :]
        out_ref[:, lanes] = x1_ref[:, lanes] + (g1 * y1 + g2 * y2)


def _combine(dest, x1, route, yb3):
    t = x1.shape[0]
    tc = min(ROW_DMA_TILE, t)
    dest_flat = _tiled_indices(_per_step_dest(dest, tc))
    return pl.pallas_call(
        functools.partial(_combine_kernel, n_steps=t // tc),
        grid=(t // tc,),
        in_specs=[pl.BlockSpec(memory_space=pl.ANY),
                  pl.BlockSpec((tc, D_MODEL), lambda i: (i, 0)),
                  pl.BlockSpec((tc, ROUTE_COLS), lambda i: (i, 0)),
                  pl.BlockSpec(memory_space=pl.ANY)],
        out_specs=pl.BlockSpec((tc, D_MODEL), lambda i: (i, 0)),
        out_shape=jax.ShapeDtypeStruct((t, D_MODEL), F32),
        scratch_shapes=[pltpu.SMEM((2 * _index_tile_len(2 * tc),), jnp.int32),
                        pltpu.VMEM((2, 2, tc * ROW_TILES, LANES), F32),
                        pltpu.SemaphoreType.DMA, pltpu.SemaphoreType.DMA((2,))],
        compiler_params=_params(), name="moe_combine",
    )(dest_flat, x1, route, yb3)


def _count_le(sorted_ends, values):
    ends = sorted_ends.reshape((-1,) + (1,) * values.ndim)
    return jnp.sum((ends <= values[None]).astype(jnp.int32), axis=0)


def _lookup(table, idx):
    keys = jnp.arange(table.shape[0], dtype=jnp.int32).reshape((-1,) + (1,) * idx.ndim)
    return jnp.sum(jnp.where(idx[None] == keys, table.reshape(keys.shape), 0), axis=0)


def _moe_block_rows(t):
    mean_rows_per_expert = max(2 * t // N_EXPERTS, 1)
    return min(MOE_BLOCK_ROWS, max(BF16_TILE_ROWS, 1 << (mean_rows_per_expert.bit_length() - 1)))


def _moe_experts(streams, layer, wg, wu, wd):
    t = sum(s[0].shape[0] for s in streams)
    blk = _moe_block_rows(t)
    cnts = [s[4][0, :N_EXPERTS].astype(jnp.int32) for s in streams]
    cnt = sum(cnts)
    padded = (cnt + blk - 1) // blk * blk
    pad_end = jnp.cumsum(padded)
    pad_start = pad_end - padded
    dests, seen = [], jnp.zeros_like(cnt)
    for s, c in zip(streams, cnts):
        expert = s[3][0:2].astype(jnp.int32)
        rank = s[3][4:6].astype(jnp.int32)
        dests.append(_lookup(pad_start + seen, expert) + rank)
        seen = seen + c
    n_blocks = -(-2 * t // blk) + N_EXPERTS
    blk_start = jnp.arange(n_blocks, dtype=jnp.int32) * blk
    block_expert = jnp.minimum(_count_le(pad_end, blk_start), N_EXPERTS - 1)
    block_nvalid = jnp.clip(_lookup(pad_start + cnt, block_expert) - blk_start, 0, blk).astype(jnp.int32)
    n_rows = n_blocks * blk
    gap_start = jnp.concatenate([pad_start + cnt, pad_end[-1:]])
    gap_len = jnp.concatenate([padded - cnt, n_rows - pad_end[-1:]])
    gap_end = jnp.cumsum(gap_len)
    p = jnp.arange(n_rows - 2 * t, dtype=jnp.int32)
    pad_rows = (p + _lookup(gap_start - gap_end + gap_len, _count_le(gap_end, p))).astype(jnp.int32)
    xs3 = _dispatch(dests, pad_rows, [s[1] for s in streams], n_rows)
    yb2 = _experts(block_expert, block_nvalid, xs3.reshape(n_rows * ROW_TILES, LANES), layer, wg, wu, wd)
    return yb2.reshape(n_rows, ROW_TILES, LANES), dests


def _router_weights(w_group, w_router):
    w = jnp.concatenate([w_router, w_group,
                         jnp.zeros((D_MODEL, ROUTE_LANES - N_EXPERTS - MOE_GROUPS), F32)], axis=1)
    hi = lax.bitcast_convert_type(lax.bitcast_convert_type(w, jnp.uint32) & jnp.uint32(0xFFFF0000), F32)
    hi, lo = hi.astype(BF16), (w - hi).astype(BF16)
    return hi, jnp.concatenate([hi, lo], axis=1)


def _rope_tables(pos):
    half = ROT_DIM // 2
    inv_freq = jnp.power(ROPE_THETA, -jnp.arange(half, dtype=F32) * (2.0 / ROT_DIM))
    ang = pos.astype(F32)[:, None] * inv_freq[None, :]
    cos, sin = jnp.cos(ang), jnp.sin(ang)
    n = pos.shape[0]
    ones = jnp.ones((n, DIFF_HEAD_DIM - ROT_DIM), F32)
    zeros = jnp.zeros((n, DIFF_HEAD_DIM - ROT_DIM), F32)
    zh = jnp.zeros((n, half), F32)
    sub_cos = jnp.concatenate([cos, cos, ones], axis=1)
    sub_lo = jnp.concatenate([-sin, zh, zeros], axis=1)
    sub_hi = jnp.concatenate([zh, sin, zeros], axis=1)
    return tuple(jnp.concatenate([a, a], axis=1) for a in (sub_cos, sub_lo, sub_hi))


def _spatial_weights(w_s, b_s, lc):
    pos = jnp.arange(lc)
    mask = (pos[None, :] // CHUNK) <= (pos[:, None] // CHUNK)
    ws = jnp.where(mask[None], w_s[:, :lc, :lc], 0.0)
    reps = GMLP_CHUNK // lc
    eye = jnp.eye(reps, dtype=F32)
    ws_eff = jnp.einsum("ab,gij->gaibj", eye, ws).reshape(GMLP_GROUPS, GMLP_CHUNK, GMLP_CHUNK)
    b_rows = jnp.tile(b_s[:, :lc], (1, reps))
    b_exp = jnp.repeat(b_rows.T, GMLP_GROUP_DIM, axis=1)
    return ws_eff, b_exp


def kernel(x_prompt, x_sample, cache_attn_k, cache_attn_v, mix_norm, ffn_norm, gmlp_w_in, gmlp_v_norm, gmlp_w_s, gmlp_b_s, gmlp_w_out, attn_w_in, attn_q_norm, attn_k_norm, attn_lam_q1, attn_lam_k1, attn_lam_q2, attn_lam_k2, attn_sub_norm, attn_w_out, moe_w_group, moe_w_router, moe_w_gate, moe_w_up, moe_w_down):
    nb_p, seq, _ = x_prompt.shape
    nb_s, dec, _ = x_sample.shape
    past = cache_attn_k.shape[2]
    assert cache_attn_k.shape[0] == DEPTH // 2 == 1 and mix_norm.shape[0] == DEPTH
    xp = x_prompt.reshape(nb_p * seq, D_MODEL)
    xs = x_sample.reshape(nb_s * dec, D_MODEL)
    row = lambda a: a.reshape(1, -1)

    router0 = _router_weights(moe_w_group[0], moe_w_router[0])
    gm_tail = (row(ffn_norm[0]),) + router0
    experts0 = (0, moe_w_gate, moe_w_up, moe_w_down)
    ws_p, b_p = _spatial_weights(gmlp_w_s[0], gmlp_b_s[0], GMLP_CHUNK)
    ws_s, b_s = _spatial_weights(gmlp_w_s[0], gmlp_b_s[0], dec)
    *mixed_p, gv_p = _gmlp_layer(
        xp, nb_p, row(mix_norm[0]), gmlp_w_in[0].astype(BF16), row(gmlp_v_norm[0]), ws_p.astype(BF16), b_p,
        gmlp_w_out[0].astype(BF16), *gm_tail)
    *mixed_s, gv_s = _gmlp_layer(
        xs, 1, row(mix_norm[0]), gmlp_w_in[0], row(gmlp_v_norm[0]), ws_s, b_s, gmlp_w_out[0], *gm_tail)
    yb3, (dest_p, dest_s) = _moe_experts([mixed_p, mixed_s], *experts0)
    xp = _combine(dest_p, mixed_p[0], mixed_p[2], yb3)
    xs = _combine(dest_s, mixed_s[0], mixed_s[2], yb3)

    lam_init = 0.8 - 0.6 * math.exp(-0.3 * 1)
    lam = (jnp.exp(jnp.sum(attn_lam_q1[0] * attn_lam_k1[0])) - jnp.exp(jnp.sum(attn_lam_q2[0] * attn_lam_k2[0]))
           + lam_init).reshape(1).astype(F32)
    grp = jnp.arange(D_MODEL) // DIFF_HEAD_DIM
    group_ones = (grp[:, None] == grp[None, :]).astype(BF16)
    qk = (row(mix_norm[1]), attn_w_in[0].astype(BF16), group_ones,
          row(jnp.tile(attn_q_norm[0], 2 * DIFF_HEADS)), row(jnp.tile(attn_k_norm[0], 2 * DIFF_HEADS)))
    router1 = _router_weights(moe_w_group[1], moe_w_router[1])
    at_tail = (attn_w_out[0].astype(BF16), row(ffn_norm[1])) + router1
    experts1 = (1, moe_w_gate, moe_w_up, moe_w_down)

    kp, vp, kb, vt, qt = _qkv_layer(xp, nb_p, *qk, *_rope_tables(jnp.arange(seq)), prompt=True)
    ap = _flash_attention(lam, qt, kb, vt, attn_sub_norm[0].reshape(LANES, 1), lam_init)
    mixed_p = _attn_out_layer(ap.reshape(nb_p * seq, D_MODEL), xp, *at_tail)

    pos_s = jnp.tile(past + jnp.arange(dec), nb_s)
    qs, ks, vs = _qkv_layer(xs, nb_s, *qk, *_rope_tables(pos_s), prompt=False)
    shp = (nb_s, dec, D_MODEL)
    a_s = _sample_attention(lam, qs.reshape(shp), ks.reshape(shp), vs.reshape(shp),
                            cache_attn_k.reshape(nb_s, past * DIFF_HEADS, LANES),
                            cache_attn_v.reshape(nb_s, past * DIFF_HEADS, LANES),
                            row(attn_sub_norm[0]), lam_init)
    mixed_s = _attn_out_layer(a_s.reshape(nb_s * dec, D_MODEL), xs, *at_tail)
    yb3, (dest_p, dest_s) = _moe_experts([mixed_p, mixed_s], *experts1)
    xp = _combine(dest_p, mixed_p[0], mixed_p[2], yb3)
    xs = _combine(dest_s, mixed_s[0], mixed_s[2], yb3)

    hv = (DIFF_HEADS, DIFF_VALUE_DIM)
    return (xp.reshape(nb_p, seq, D_MODEL), xs.reshape(nb_s, dec, D_MODEL),
            gv_p[None], gv_s.reshape(1, nb_s, dec, GMLP_WIDTH),
            kp.reshape(1, nb_p, seq, *hv), vp.reshape(1, nb_p, seq, *hv),
            ks.reshape(1, nb_s, dec, *hv), vs.reshape(1, nb_s, dec, *hv))
```
